```python
import jax, jax.numpy as jnp
from jax import lax
import numpy as np


D_MODEL = 2048
BATCH = 8
SEQ = 2048
DEPTH = 2

GRID_W = 64
CTX_LEN = 256

HD = 64
H_A = 8
D_A = H_A * HD
H_B = 12
HKV_B = 4
D_B = H_B * HD
KV_B = HKV_B * HD
H_C = 6
NOPE_C = 128
ROPE_DIM = 64
QK_C = NOPE_C + ROPE_DIM
V_C = 128
D_C = H_C * V_C
Q_LORA = 768
KV_LORA = 512
D_MIX = D_A + D_B + D_C
N_IN = 3 * D_A + D_B + 2 * KV_B + Q_LORA + KV_LORA + ROPE_DIM + D_MIX

NA_ROWS = 8
NA_COLS = 16
SW_WINDOW = 128
SW_BLOCK = 128
Q_BLOCK = 128

ROPE_BASE = 10000.0
ADA_SCALE = 0.5
EPS = 1e-6
NEG_INF = -1e30

kernel_name = "hymba_grid_hybrid_dit_block"


def rms(x, w):
    xf = x.astype(jnp.float32)
    y = xf * lax.rsqrt(jnp.mean(xf * xf, axis=-1, keepdims=True) + EPS)
    return (y * w.astype(jnp.float32)).astype(x.dtype)


def axial_rope(L, dim, dtype):
    t = jnp.arange(L)
    row = (t // GRID_W).astype(jnp.float32)
    col = (t % GRID_W).astype(jnp.float32)
    n_freq = dim // 4
    inv = ROPE_BASE ** (-jnp.arange(n_freq, dtype=jnp.float32) / n_freq)
    ar = row[:, None] * inv
    ac = col[:, None] * inv
    ang = jnp.concatenate([ar, ar, ac, ac], axis=-1)
    return jnp.cos(ang).astype(dtype), jnp.sin(ang).astype(dtype)


def apply_rope2d(x, cos, sin):
    x1, x2, x3, x4 = jnp.split(x, 4, axis=-1)
    rot = jnp.concatenate([-x2, x1, -x4, x3], axis=-1)
    return x * cos[None, :, None, :] + rot * sin[None, :, None, :]


def split_points():
    sizes = (D_A, D_A, D_A, D_B, KV_B, KV_B, Q_LORA, KV_LORA, ROPE_DIM, D_MIX)
    return [int(p) for p in np.cumsum(sizes)[:-1]]


def project_stream(h, w_in, qn_a, kn_a, qn_b, kn_b, qa_norm, kva_norm, w_qb, w_kvb, qn_c, kn_c, rope):
    B, L, _ = h.shape
    u = h @ w_in
    qa, ka, va, qb, kb, vb, cq, ckv, kpe, z = jnp.split(u, split_points(), axis=-1)

    def heads(t, n):
        return t.reshape(B, L, n, -1)

    qa = rms(heads(qa, H_A), qn_a)
    ka = rms(heads(ka, H_A), kn_a)
    va = heads(va, H_A)
    qb = rms(heads(qb, H_B), qn_b)
    kb = rms(heads(kb, HKV_B), kn_b)
    vb = heads(vb, HKV_B)
    qc = rms(heads(rms(cq, qa_norm) @ w_qb, H_C), qn_c)
    kvc = heads(rms(ckv, kva_norm) @ w_kvb, H_C)
    kc = jnp.concatenate(
        [kvc[..., :NOPE_C], jnp.broadcast_to(kpe[:, :, None, :], (B, L, H_C, ROPE_DIM))], axis=-1)
    kc = rms(kc, kn_c)
    vc = kvc[..., NOPE_C:]
    if rope is not None:
        cos, sin = rope
        qb = apply_rope2d(qb, cos, sin)
        kb = apply_rope2d(kb, cos, sin)
        qc = jnp.concatenate([qc[..., :NOPE_C], apply_rope2d(qc[..., NOPE_C:], cos, sin)], axis=-1)
        kc = jnp.concatenate([kc[..., :NOPE_C], apply_rope2d(kc[..., NOPE_C:], cos, sin)], axis=-1)
    return (qa, ka, va, qb, kb, vb, qc, kc, vc, z)


def natten_latent(q, k, v, kc, vc, rpb):
    B, L, H, D = q.shape
    rows = L // GRID_W
    wr = min(NA_ROWS, rows)
    ncb = GRID_W // NA_COLS
    span = 2 * NA_COLS
    cb = np.arange(ncb)
    kcol = np.clip(cb * NA_COLS - NA_COLS // 2, 0, GRID_W - span)[:, None] + np.arange(span)
    qcol = cb[:, None] * NA_COLS + np.arange(NA_COLS)
    qstart = np.clip(qcol - NA_COLS // 2, 0, GRID_W - NA_COLS)
    col_ok = (kcol[:, None, :] >= qstart[:, :, None]) & (kcol[:, None, :] < qstart[:, :, None] + NA_COLS)
    dc_idx = (np.clip(kcol[:, None, :] - qcol[:, :, None], -(NA_COLS - 1), NA_COLS - 1)
              + NA_COLS - 1).astype(np.int32)
    qg = q.reshape(B, rows, ncb, NA_COLS, H, D)
    kg = k.reshape(B, rows, GRID_W, H, D)[:, :, kcol]
    vg = v.reshape(B, rows, GRID_W, H, D)[:, :, kcol]
    scale = D ** -0.5
    n_loc = wr * span

    def row_fn(r):
        rs = jnp.clip(r - wr // 2, 0, rows - wr)
        qr = lax.dynamic_index_in_dim(qg, r, axis=1, keepdims=False)
        kr = lax.dynamic_slice_in_dim(kg, rs, wr, axis=1)
        vr = lax.dynamic_slice_in_dim(vg, rs, wr, axis=1)
        dr_idx = rs + jnp.arange(wr) - r + NA_ROWS - 1
        bias = rpb[:, dr_idx[None, None, :, None], dc_idx[:, :, None, :]]
        s_loc = jnp.einsum('bnqhd,brnkhd->bhnqrk', qr, kr,
                           preferred_element_type=jnp.float32) * scale + bias
        s_loc = jnp.where(col_ok[:, :, None, :], s_loc, NEG_INF)
        s_ctx = jnp.einsum('bnqhd,bchd->bhnqc', qr, kc, preferred_element_type=jnp.float32) * scale
        s = jnp.concatenate([s_loc.reshape(B, H, ncb, NA_COLS, n_loc), s_ctx], axis=-1)
        p = jax.nn.softmax(s, axis=-1).astype(v.dtype)
        p_loc = p[..., :n_loc].reshape(B, H, ncb, NA_COLS, wr, span)
        o = (jnp.einsum('bhnqrk,brnkhd->bnqhd', p_loc, vr)
             + jnp.einsum('bhnqc,bchd->bnqhd', p[..., n_loc:], vc))
        return o.reshape(B, GRID_W, H, D)

    o = lax.map(row_fn, jnp.arange(rows))
    return o.transpose(1, 0, 2, 3, 4).reshape(B, L, H, D)


def swa_latent(q, k, v, kc, vc, sink):
    B, L, H, D = q.shape
    Hkv = k.shape[2]
    G = H // Hkv
    Lc = kc.shape[1]
    nb = L // SW_BLOCK
    span = SW_BLOCK + 2 * SW_WINDOW
    pad = ((0, 0), (SW_WINDOW, SW_WINDOW), (0, 0), (0, 0))
    kp = jnp.pad(k, pad)
    vp = jnp.pad(v, pad)
    qb = q.reshape(B, nb, SW_BLOCK, Hkv, G, D).transpose(1, 0, 2, 3, 4, 5)
    snk = sink.reshape(Hkv, G).astype(jnp.float32)[None, :, :, None, None]
    scale = D ** -0.5

    def blk(args):
        i, qi = args
        k0 = i * SW_BLOCK
        ki = lax.dynamic_slice_in_dim(kp, k0, span, axis=1)
        vi = lax.dynamic_slice_in_dim(vp, k0, span, axis=1)
        qpos = k0 + jnp.arange(SW_BLOCK)
        kpos = k0 - SW_WINDOW + jnp.arange(span)
        ok = (kpos >= 0) & (kpos < L) & (jnp.abs(qpos[:, None] - kpos[None, :]) <= SW_WINDOW)
        s_loc = jnp.einsum('bqkgd,bjkd->bkgqj', qi, ki, preferred_element_type=jnp.float32) * scale
        s_loc = jnp.where(ok, s_loc, NEG_INF)
        s_ctx = jnp.einsum('bqkgd,bckd->bkgqc', qi, kc, preferred_element_type=jnp.float32) * scale
        s = jnp.concatenate([s_loc, s_ctx, jnp.broadcast_to(snk, s_loc.shape[:-1] + (1,))], axis=-1)
        p = jax.nn.softmax(s, axis=-1).astype(v.dtype)
        o = (jnp.einsum('bkgqj,bjkd->bqkgd', p[..., :span], vi)
             + jnp.einsum('bkgqc,bckd->bqkgd', p[..., span:span + Lc], vc))
        return o.reshape(B, SW_BLOCK, H, D)

    o = lax.map(blk, (jnp.arange(nb), qb))
    return o.transpose(1, 0, 2, 3, 4).reshape(B, L, H, D)


def mla_latent(q, k, v, kc, vc):
    B, L, H, D = q.shape
    Dv = v.shape[-1]
    nb = L // Q_BLOCK
    k_all = jnp.concatenate([kc, k], axis=1)
    v_all = jnp.concatenate([vc, v], axis=1)
    qb = q.reshape(B, nb, Q_BLOCK, H, D).transpose(1, 0, 2, 3, 4)
    scale = D ** -0.5

    def blk(qi):
        s = jnp.einsum('bqhd,bkhd->bhqk', qi, k_all, preferred_element_type=jnp.float32) * scale
        p = jax.nn.softmax(s, axis=-1).astype(v.dtype)
        return jnp.einsum('bhqk,bkhd->bqhd', p, v_all)

    o = lax.map(blk, qb)
    return o.transpose(1, 0, 2, 3, 4).reshape(B, L, H, Dv)


def ctx_attn(q, k, v, sink=None):
    B, Lc, H, D = q.shape
    Hkv = k.shape[2]
    G = H // Hkv
    qg = q.reshape(B, Lc, Hkv, G, D)
    s = jnp.einsum('bqkgd,bckd->bkgqc', qg, k, preferred_element_type=jnp.float32) * (D ** -0.5)
    if sink is not None:
        snk = sink.reshape(Hkv, G).astype(jnp.float32)[None, :, :, None, None]
        s = jnp.concatenate([s, jnp.broadcast_to(snk, s.shape[:-1] + (1,))], axis=-1)
    p = jax.nn.softmax(s, axis=-1)[..., :Lc].astype(v.dtype)
    return jnp.einsum('bkgqc,bckd->bqkgd', p, v).reshape(B, Lc, H, v.shape[-1])


def merge_out(o_a, o_b, o_c, z, w_out):
    B, L = o_a.shape[:2]
    y = jnp.concatenate([o_a.reshape(B, L, -1), o_b.reshape(B, L, -1), o_c.reshape(B, L, -1)], axis=-1)
    return (y * jax.nn.silu(z)) @ w_out


def setup_inputs(seed: int = 0) -> dict:
    key = jax.random.key(seed)
    ks = jax.random.split(key, 21)
    f32 = jnp.float32

    def nrm(k, shape, s):
        return jax.random.normal(k, shape, f32) * s

    def gain(k, shape):
        return 1.0 + 0.02 * jax.random.normal(k, shape, f32)

    return {
        "x": nrm(ks[0], (BATCH, SEQ, D_MODEL), 1.0),
        "c": nrm(ks[1], (BATCH, D_MODEL), 1.0),
        "ctx": nrm(ks[2], (BATCH, CTX_LEN, D_MODEL), 1.0),
        "c_ctx": nrm(ks[3], (D_MODEL,), 1.0),
        "norm_w": gain(ks[4], (DEPTH, D_MODEL)),
        "w_ada": nrm(ks[5], (DEPTH, D_MODEL, 3 * D_MODEL), ADA_SCALE * D_MODEL ** -0.5),
        "b_ada": nrm(ks[6], (DEPTH, 3 * D_MODEL), 0.01),
        "w_in": nrm(ks[7], (DEPTH, D_MODEL, N_IN), D_MODEL ** -0.5),
        "qn_a": gain(ks[8], (DEPTH, HD)),
        "kn_a": gain(ks[9], (DEPTH, HD)),
        "rpb_a": nrm(ks[10], (DEPTH, H_A, 2 * NA_ROWS - 1, 2 * NA_COLS - 1), 0.1),
        "qn_b": gain(ks[11], (DEPTH, HD)),
        "kn_b": gain(ks[12], (DEPTH, HD)),
        "sink_b": nrm(ks[13], (DEPTH, H_B), 0.5),
        "qa_norm": gain(ks[14], (DEPTH, Q_LORA)),
        "kva_norm": gain(ks[15], (DEPTH, KV_LORA)),
        "w_qb": nrm(ks[16], (DEPTH, Q_LORA, H_C * QK_C), Q_LORA ** -0.5),
        "w_kvb": nrm(ks[17], (DEPTH, KV_LORA, H_C * (NOPE_C + V_C)), KV_LORA ** -0.5),
        "qn_c": gain(ks[18], (DEPTH, QK_C)),
        "kn_c": gain(ks[19], (DEPTH, QK_C)),
        "w_out": nrm(ks[20], (DEPTH, D_MIX, D_MODEL), D_MIX ** -0.5),
    }


def reference(x, c, ctx, c_ctx, norm_w, w_ada, b_ada, w_in, qn_a, kn_a, rpb_a, qn_b, kn_b, sink_b,
              qa_norm, kva_norm, w_qb, w_kvb, qn_c, kn_c, w_out):
    B, L, _ = x.shape
    cos, sin = axial_rope(L, ROPE_DIM, x.dtype)
    sc = jax.nn.silu(c)
    scc = jax.nn.silu(c_ctx)
    xc = ctx
    for l in range(DEPTH):
        last = l == DEPTH - 1
        shift, scale, gate = jnp.split(sc @ w_ada[l] + b_ada[l], 3, axis=-1)
        shift_c, scale_c, gate_c = jnp.split(scc @ w_ada[l] + b_ada[l], 3, axis=-1)
        h = rms(x, norm_w[l]) * (1.0 + scale[:, None]) + shift[:, None]
        hc = rms(xc, norm_w[l]) * (1.0 + scale_c) + shift_c
        weights = (w_in[l], qn_a[l], kn_a[l], qn_b[l], kn_b[l], qa_norm[l], kva_norm[l],
                   w_qb[l], w_kvb[l], qn_c[l], kn_c[l])
        qa, ka, va, qb, kb, vb, qc, kc, vc, z = project_stream(h, *weights, rope=(cos, sin))
        qa_c, ka_c, va_c, qb_c, kb_c, vb_c, qc_c, kc_c, vc_c, z_c = project_stream(hc, *weights, rope=None)
        o_a = natten_latent(qa, ka, va, ka_c, va_c, rpb_a[l])
        o_b = swa_latent(qb, kb, vb, kb_c, vb_c, sink_b[l])
        o_c = mla_latent(qc, kc, vc, kc_c, vc_c)
        x = x + gate[:, None] * merge_out(o_a, o_b, o_c, z, w_out[l])
        if not last:
            oa_c = ctx_attn(qa_c, ka_c, va_c)
            ob_c = ctx_attn(qb_c, kb_c, vb_c, sink_b[l])
            oc_c = ctx_attn(qc_c, kc_c, vc_c)
            xc = xc + gate_c * merge_out(oa_c, ob_c, oc_c, z_c, w_out[l])
    return x
```

```python
import functools

import numpy as np
import jax
import jax.numpy as jnp
from jax import lax
from jax.experimental import pallas as pl
from jax.experimental.pallas import tpu as pltpu

D_MODEL = 2048
BATCH = 8
SEQ = 2048
DEPTH = 2
GRID_W = 64
CTX_LEN = 256
HD = 64
H_A = 8
D_A = H_A * HD
H_B = 12
HKV_B = 4
D_B = H_B * HD
KV_B = HKV_B * HD
H_C = 6
NOPE_C = 128
ROPE_DIM = 64
QK_C = NOPE_C + ROPE_DIM
V_C = 128
D_C = H_C * V_C
Q_LORA = 768
KV_LORA = 512
D_MIX = D_A + D_B + D_C
NA_ROWS = 8
NA_COLS = 16
SW_WINDOW = 128
ROPE_BASE = 10000.0
EPS = 1e-6
NEG_INF = -1e30

S_ALL = SEQ + CTX_LEN
R_ALL = BATCH * S_ALL
GRID_ROWS = SEQ // GRID_W
LANES = 128
MXU_DIM = 256
QC_PAD = 256
MOD_ROWS = 16
VMEM_LIMIT = 48 * 1024 * 1024

bf16 = jnp.bfloat16
f32 = jnp.float32


def _cparams(*sem):
    return pltpu.CompilerParams(dimension_semantics=sem, vmem_limit_bytes=VMEM_LIMIT)


def _silu(v):
    return v / (1.0 + jnp.exp(-v))


def _lane_iota(shape):
    return lax.broadcasted_iota(jnp.int32, shape, len(shape) - 1)


def _rope_rotate(xn, cos, sin_signed):
    lane = _lane_iota(xn.shape)
    take_next = (lane % 32) < 16
    rot = jnp.where(take_next, pltpu.roll(xn, LANES - 16, 1), pltpu.roll(xn, 16, 1))
    return xn * cos + rot * sin_signed


def _dot_nt(a, b):
    return lax.dot_general(a, b, (((1,), (1,)), ((), ())), preferred_element_type=f32)


def _dot(a, b):
    return jnp.dot(a, b, preferred_element_type=f32)


def _mod_kernel(c_ref, w_ref, b_ref, o_ref):
    sc = _silu(c_ref[...]).astype(bf16)
    o_ref[0] = _dot(sc, w_ref[0].astype(bf16)) + b_ref[0]


def _modulation(cvec, w_ada, b_ada):
    tn = 1024
    n = 3 * D_MODEL
    return pl.pallas_call(
        _mod_kernel,
        grid=(DEPTH, n // tn),
        in_specs=[
            pl.BlockSpec((MOD_ROWS, D_MODEL), lambda l, j: (0, 0)),
            pl.BlockSpec((1, D_MODEL, tn), lambda l, j: (l, 0, j)),
            pl.BlockSpec((1, 1, tn), lambda l, j: (l, 0, j)),
        ],
        out_specs=pl.BlockSpec((1, MOD_ROWS, tn), lambda l, j: (l, 0, j)),
        out_shape=jax.ShapeDtypeStruct((DEPTH, MOD_ROWS, n), f32),
        compiler_params=_cparams("arbitrary", "arbitrary"),
        name="modulation",
    )(cvec, w_ada, b_ada.reshape(DEPTH, 1, n))


H_ROWS = 256
N_LAT_BLOCKS = SEQ // H_ROWS


def _hnorm_kernel(x_ref, ctx_ref, nw_ref, mod_ref, h_ref):
    s = pl.program_id(1)

    def body(src):
        x = src[0]
        ms = jnp.mean(x * x, axis=-1, keepdims=True)
        y = x * lax.rsqrt(ms + EPS) * nw_ref[...]
        shift = mod_ref[0, :, 0:D_MODEL]
        scale = mod_ref[0, :, D_MODEL:2 * D_MODEL]
        h_ref[0] = (y * (1.0 + scale) + shift).astype(bf16)

    @pl.when(s < N_LAT_BLOCKS)
    def _():
        body(x_ref)

    @pl.when(s >= N_LAT_BLOCKS)
    def _():
        body(ctx_ref)


def _hnorm(x, xc, norm_w, mod3):
    return pl.pallas_call(
        _hnorm_kernel,
        grid=(BATCH, S_ALL // H_ROWS),
        in_specs=[
            pl.BlockSpec((1, H_ROWS, D_MODEL), lambda b, s: (b, jnp.minimum(s, N_LAT_BLOCKS - 1), 0)),
            pl.BlockSpec((1, H_ROWS, D_MODEL), lambda b, s: (b, 0, 0)),
            pl.BlockSpec((1, D_MODEL), lambda b, s: (0, 0)),
            pl.BlockSpec((1, 1, 3 * D_MODEL), lambda b, s: (jnp.where(s < N_LAT_BLOCKS, b, BATCH), 0, 0)),
        ],
        out_specs=pl.BlockSpec((1, H_ROWS, D_MODEL), lambda b, s: (b, s, 0)),
        out_shape=jax.ShapeDtypeStruct((BATCH, S_ALL, D_MODEL), bf16),
        compiler_params=_cparams("arbitrary", "arbitrary"),
        name="hnorm",
    )(x, xc, norm_w.reshape(1, D_MODEL), mod3)


PROJ_TM = 768


def _group_mean_sq(u, g_ref):
    sq = u * u
    hi = sq.astype(bf16)
    lo = (sq - hi.astype(f32)).astype(bf16)
    g = g_ref[...]
    return _dot(hi, g) + _dot(lo, g)


def _proj_kernel(mode, tn, h_ref, w_ref, *rest):
    o_ref = rest[-1]
    u = _dot(h_ref[...], w_ref[...])
    if mode == "plain":
        o_ref[...] = u.astype(o_ref.dtype)
    elif mode == "silu":
        o_ref[...] = _silu(u).astype(o_ref.dtype)
    elif mode == "rowrms":
        gain_ref = rest[0]
        ms = jnp.mean(u * u, axis=-1, keepdims=True)
        o_ref[...] = (u * lax.rsqrt(ms + EPS) * gain_ref[...]).astype(o_ref.dtype)
    else:
        gain_ref, g_ref = rest[0], rest[1]
        for c0 in range(0, tn, MXU_DIM):
            uc = u[:, c0:c0 + MXU_DIM]
            ms = _group_mean_sq(uc, g_ref)
            xn = uc * lax.rsqrt(ms + EPS) * gain_ref[:, c0:c0 + MXU_DIM]
            if mode == "headnorm_rope":
                cos = rest[2][...]
                sin = rest[3][...]
                for c1 in range(0, MXU_DIM, LANES):
                    o_ref[:, c0 + c1:c0 + c1 + LANES] = _rope_rotate(
                        xn[:, c1:c1 + LANES], cos, sin).astype(o_ref.dtype)
            else:
                o_ref[:, c0:c0 + MXU_DIM] = xn.astype(o_ref.dtype)


def _group_matrix():
    idx = np.arange(MXU_DIM) // HD
    return jnp.asarray((idx[:, None] == idx[None, :]).astype(np.float32) / HD, dtype=bf16)


def _proj(h2d, w, mode, tn, out_dtype, gain=None, cos=None, sin=None, name="proj"):
    n = w.shape[1]
    tm = PROJ_TM
    in_specs = [
        pl.BlockSpec((tm, D_MODEL), lambda i, j: (i, 0)),
        pl.BlockSpec((D_MODEL, tn), lambda i, j: (0, j)),
    ]
    args = [h2d, w]
    if mode in ("rowrms", "headnorm", "headnorm_rope"):
        in_specs.append(pl.BlockSpec((1, tn), lambda i, j: (0, j)))
        args.append(gain.reshape(1, n))
    if mode in ("headnorm", "headnorm_rope"):
        in_specs.append(pl.BlockSpec((MXU_DIM, MXU_DIM), lambda i, j: (0, 0)))
        args.append(_group_matrix())
    if mode == "headnorm_rope":
        nt = S_ALL // tm
        in_specs.append(pl.BlockSpec((tm, LANES), lambda i, j: (i % nt, 0)))
        in_specs.append(pl.BlockSpec((tm, LANES), lambda i, j: (i % nt, 0)))
        args += [cos, sin]
    return pl.pallas_call(
        functools.partial(_proj_kernel, mode, tn),
        grid=(R_ALL // tm, n // tn),
        in_specs=in_specs,
        out_specs=pl.BlockSpec((tm, tn), lambda i, j: (i, j)),
        out_shape=jax.ShapeDtypeStruct((R_ALL, n), out_dtype),
        compiler_params=_cparams("arbitrary", "arbitrary"),
        name=name,
    )(*args)


def _mla_up_kernel(cq_ref, ckv_ref, kpe_ref, wq_ref, wkn_ref, wv_ref, gq_ref, gk_ref,
                   cos_ref, sin_ref, qc_ref, kc_ref, vc_ref):
    cos = cos_ref[...]
    sin = sin_ref[...]
    ckv = ckv_ref[...]
    vc_ref[...] = _dot(ckv, wv_ref[...]).astype(bf16)

    uq = _dot(cq_ref[...], wq_ref[...])
    for h in range(H_C):
        blk = uq[:, h * QC_PAD:(h + 1) * QC_PAD]
        ms = jnp.sum(blk * blk, axis=-1, keepdims=True) * (1.0 / QK_C)
        xn = blk * lax.rsqrt(ms + EPS) * gq_ref[:, h * QC_PAD:(h + 1) * QC_PAD]
        qc_ref[:, h * QC_PAD:h * QC_PAD + LANES] = xn[:, :LANES].astype(bf16)
        qc_ref[:, h * QC_PAD + LANES:(h + 1) * QC_PAD] = _rope_rotate(xn[:, LANES:], cos, sin).astype(bf16)

    kn = _dot(ckv, wkn_ref[...])
    kpe = kpe_ref[...]
    ss_pe = jnp.sum(kpe * kpe, axis=-1, keepdims=True)
    for h in range(H_C):
        blk = kn[:, h * NOPE_C:(h + 1) * NOPE_C]
        ms = (jnp.sum(blk * blk, axis=-1, keepdims=True) + ss_pe) * (1.0 / QK_C)
        r = lax.rsqrt(ms + EPS)
        kc_ref[:, h * QC_PAD:h * QC_PAD + LANES] = (blk * r * gk_ref[:, 0:LANES]).astype(bf16)
        kc_ref[:, h * QC_PAD + LANES:(h + 1) * QC_PAD] = _rope_rotate(
            kpe * r * gk_ref[:, LANES:], cos, sin).astype(bf16)


def _mla_up(cqn, ckvn, kpe, wq, wkn, wv, gq, gk, cos, sin):
    tm = PROJ_TM
    nt = S_ALL // tm
    row = lambda i: (i, 0)
    fixed = lambda i: (0, 0)
    return pl.pallas_call(
        _mla_up_kernel,
        grid=(R_ALL // tm,),
        in_specs=[
            pl.BlockSpec((tm, Q_LORA), row),
            pl.BlockSpec((tm, KV_LORA), row),
            pl.BlockSpec((tm, LANES), row),
            pl.BlockSpec((Q_LORA, H_C * QC_PAD), fixed),
            pl.BlockSpec((KV_LORA, H_C * NOPE_C), fixed),
            pl.BlockSpec((KV_LORA, H_C * V_C), fixed),
            pl.BlockSpec((1, H_C * QC_PAD), fixed),
            pl.BlockSpec((1, QC_PAD), fixed),
            pl.BlockSpec((tm, LANES), lambda i: (i % nt, 0)),
            pl.BlockSpec((tm, LANES), lambda i: (i % nt, 0)),
        ],
        out_specs=[
            pl.BlockSpec((tm, H_C * QC_PAD), row),
            pl.BlockSpec((tm, H_C * QC_PAD), row),
            pl.BlockSpec((tm, H_C * V_C), row),
        ],
        out_shape=[
            jax.ShapeDtypeStruct((R_ALL, H_C * QC_PAD), bf16),
            jax.ShapeDtypeStruct((R_ALL, H_C * QC_PAD), bf16),
            jax.ShapeDtypeStruct((R_ALL, H_C * V_C), bf16),
        ],
        compiler_params=_cparams("arbitrary"),
        name="mla_up",
    )(cqn, ckvn, kpe, wq, wkn, wv, gq, gk, cos, sin)


def _split_pair(q):
    low = _lane_iota(q.shape) < HD
    zero = jnp.zeros_like(q)
    return jnp.concatenate([jnp.where(low, q, zero), jnp.where(low, zero, q)], axis=0)


def _merge_pair(o, rows):
    low = _lane_iota((rows, LANES)) < HD
    return jnp.where(low, o[:rows], o[rows:])


def _softmax_pv(parts, extra_logit=None):
    m = None
    for s, _ in parts:
        mi = jnp.max(s, axis=-1, keepdims=True)
        m = mi if m is None else jnp.maximum(m, mi)
    if extra_logit is not None:
        m = jnp.maximum(m, extra_logit)
    l = None
    o = None
    for s, v in parts:
        e = jnp.exp(s - m)
        li = jnp.sum(e, axis=-1, keepdims=True)
        oi = _dot(e.astype(bf16), v)
        l = li if l is None else l + li
        o = oi if o is None else o + oi
    if extra_logit is not None:
        l = l + jnp.exp(extra_logit - m)
    return o / l


def _natten_kernel(q_ref, k_ref, v_ref, bias_ref, sz_ref, o_ref):
    r = pl.program_id(1)
    rs = jnp.clip(r - NA_ROWS // 2, 0, GRID_ROWS - NA_ROWS)
    k0 = pl.multiple_of(rs * GRID_W, GRID_W)
    n_loc = NA_ROWS * GRID_W
    for p in range(H_A // 2):
        cols = slice(p * LANES, (p + 1) * LANES)
        q2 = _split_pair(q_ref[0, :, cols])
        k_loc = k_ref[0, pl.ds(k0, n_loc), cols]
        v_loc = v_ref[0, pl.ds(k0, n_loc), cols]
        k_ctx = k_ref[0, SEQ:S_ALL, cols]
        v_ctx = v_ref[0, SEQ:S_ALL, cols]
        s_loc = _dot_nt(q2, k_loc) + bias_ref[0, p * LANES:(p + 1) * LANES, :]
        s_ctx = _dot_nt(q2, k_ctx)
        o = _softmax_pv([(s_loc, v_loc), (s_ctx, v_ctx)])
        o_ref[0, :, cols] = (_merge_pair(o, GRID_W) * sz_ref[0, :, cols]).astype(bf16)


def _natten(qk, v, bias, sz):
    def variant(b, r):
        return (jnp.clip(r - NA_ROWS // 2, 0, GRID_ROWS - NA_ROWS) - r + NA_ROWS - 1, 0, 0)

    return pl.pallas_call(
        _natten_kernel,
        grid=(BATCH, GRID_ROWS),
        in_specs=[
            pl.BlockSpec((1, GRID_W, D_A), lambda b, r: (b, r, 0)),
            pl.BlockSpec((1, S_ALL, D_A), lambda b, r: (b, 0, 1)),
            pl.BlockSpec((1, S_ALL, D_A), lambda b, r: (b, 0, 0)),
            pl.BlockSpec((1, H_A * GRID_W, NA_ROWS * GRID_W), variant),
            pl.BlockSpec((1, GRID_W, D_A), lambda b, r: (b, r, 0)),
        ],
        out_specs=pl.BlockSpec((1, GRID_W, D_A), lambda b, r: (b, r, 0)),
        out_shape=jax.ShapeDtypeStruct((BATCH, SEQ, D_A), bf16),
        compiler_params=_cparams("arbitrary", "arbitrary"),
        name="natten",
    )(qk, qk, v, bias, sz)


def _natten_bias(rpb):
    var = np.arange(NA_ROWS)
    kr = np.arange(NA_ROWS)
    c = np.arange(GRID_W)
    kc = np.arange(GRID_W)
    dr = var[:, None] + kr[None, :]
    qstart = np.clip(c - NA_COLS // 2, 0, GRID_W - NA_COLS)
    ok = (kc[None, :] >= qstart[:, None]) & (kc[None, :] < qstart[:, None] + NA_COLS)
    dc = np.clip(kc[None, :] - c[:, None], -(NA_COLS - 1), NA_COLS - 1) + NA_COLS - 1
    t = rpb[:, dr[:, :, None, None], dc[None, None, :, :]]
    t = jnp.where(ok[None, None, None, :, :], t, NEG_INF)
    t = t.transpose(1, 0, 3, 2, 4)
    return t.reshape(NA_ROWS, H_A * GRID_W, NA_ROWS * GRID_W)


SW_Q = 128
SW_SPAN = SW_Q + 2 * SW_WINDOW
G_B = H_B // HKV_B
QCOLS_PAIR = 2 * G_B * HD


def _gqa_stack(q3):
    c0, c1, c2 = q3
    low = _lane_iota(c0.shape) < HD
    zero = jnp.zeros_like(c0)
    parts = [
        jnp.where(low, c0, zero),
        jnp.where(low, pltpu.roll(c0, HD, 1), zero),
        jnp.where(low, c1, zero),
        jnp.where(low, zero, c1),
        jnp.where(low, zero, pltpu.roll(c2, HD, 1)),
        jnp.where(low, zero, c2),
    ]
    return jnp.concatenate(parts, axis=0).astype(bf16)


def _gqa_unstack(o, rows):
    low = _lane_iota((rows, LANES)) < HD
    blk = [o[i * rows:(i + 1) * rows] for i in range(2 * G_B)]
    t0 = jnp.where(low, blk[0], pltpu.roll(blk[1], HD, 1))
    t1 = jnp.where(low, blk[2], blk[3])
    t2 = jnp.where(low, pltpu.roll(blk[4], HD, 1), blk[5])
    return jnp.concatenate([t0, t1, t2], axis=1)


def _swa_kernel(q_ref, k_ref, v_ref, sink_ref, sz_ref, o_ref):
    n = pl.program_id(1)
    start = pl.multiple_of(jnp.clip((n - 1) * SW_Q, 0, SEQ - SW_SPAN), SW_Q)
    rows = 2 * G_B * SW_Q
    qpos = n * SW_Q + (lax.broadcasted_iota(jnp.int32, (rows, SW_SPAN), 0) % SW_Q)
    kpos = start + lax.broadcasted_iota(jnp.int32, (rows, SW_SPAN), 1)
    ok = jnp.abs(qpos - kpos) <= SW_WINDOW
    for kp in range(HKV_B // 2):
        kcols = slice(kp * LANES, (kp + 1) * LANES)
        q3 = [q_ref[0, :, kp * QCOLS_PAIR + i * LANES:kp * QCOLS_PAIR + (i + 1) * LANES].astype(f32)
              for i in range(G_B)]
        qs = _gqa_stack(q3)
        k_loc = k_ref[0, pl.ds(start, SW_SPAN), kcols]
        v_loc = v_ref[0, pl.ds(start, SW_SPAN), kcols]
        k_ctx = k_ref[0, SEQ:S_ALL, kcols]
        v_ctx = v_ref[0, SEQ:S_ALL, kcols]
        s_loc = jnp.where(ok, _dot_nt(qs, k_loc), NEG_INF)
        s_ctx = _dot_nt(qs, k_ctx)
        o = _softmax_pv([(s_loc, v_loc), (s_ctx, v_ctx)], extra_logit=sink_ref[kp])
        ocols = slice(kp * QCOLS_PAIR, (kp + 1) * QCOLS_PAIR)
        o_ref[0, :, ocols] = (_gqa_unstack(o, SW_Q) * sz_ref[0, :, ocols]).astype(bf16)


def _sink_rows(sink, rows):
    return jnp.repeat(sink.astype(f32).reshape(HKV_B // 2, 2 * G_B), rows, axis=1)[..., None]


def _swa(qk, v, sink, sz):
    return pl.pallas_call(
        _swa_kernel,
        grid=(BATCH, SEQ // SW_Q),
        in_specs=[
            pl.BlockSpec((1, SW_Q, D_B), lambda b, n: (b, n, 0)),
            pl.BlockSpec((1, S_ALL, KV_B), lambda b, n: (b, 0, D_B // KV_B)),
            pl.BlockSpec((1, S_ALL, KV_B), lambda b, n: (b, 0, D_A // KV_B)),
            pl.BlockSpec((HKV_B // 2, 2 * G_B * SW_Q, 1), lambda b, n: (0, 0, 0)),
            pl.BlockSpec((1, SW_Q, D_B), lambda b, n: (b, n, 0)),
        ],
        out_specs=pl.BlockSpec((1, SW_Q, D_B), lambda b, n: (b, n, 0)),
        out_shape=jax.ShapeDtypeStruct((BATCH, SEQ, D_B), bf16),
        compiler_params=_cparams("arbitrary", "arbitrary"),
        name="swa",
    )(qk, qk, v, _sink_rows(sink, SW_Q), sz)


MLA_TQ = 256
MLA_SCALE = float(QK_C) ** -0.5


def _mla_kernel(q_ref, k_ref, v_ref, sz_ref, o_ref):
    s = _dot_nt(q_ref[0], k_ref[0]) * MLA_SCALE
    o = _softmax_pv([(s, v_ref[0])])
    o_ref[0] = (o * sz_ref[0]).astype(bf16)


def _mla(qc, kc, vc, sz):
    return pl.pallas_call(
        _mla_kernel,
        grid=(BATCH, H_C, SEQ // MLA_TQ),
        in_specs=[
            pl.BlockSpec((1, MLA_TQ, QC_PAD), lambda b, h, i: (b, i, h)),
            pl.BlockSpec((1, S_ALL, QC_PAD), lambda b, h, i: (b, 0, h)),
            pl.BlockSpec((1, S_ALL, V_C), lambda b, h, i: (b, 0, h)),
            pl.BlockSpec((1, MLA_TQ, V_C), lambda b, h, i: (b, i, h)),
        ],
        out_specs=pl.BlockSpec((1, MLA_TQ, V_C), lambda b, h, i: (b, i, h)),
        out_shape=jax.ShapeDtypeStruct((BATCH, SEQ, D_C), bf16),
        compiler_params=_cparams("arbitrary", "arbitrary", "arbitrary"),
        name="mla",
    )(qc, kc, vc, sz)


def _ctx_kernel(qka_ref, ka_ref, va_ref, qb_ref, kb_ref, vb_ref, qc_ref, kc_ref, vc_ref,
                sink_ref, sza_ref, szb_ref, szc_ref, oa_ref, ob_ref, oc_ref):
    for p in range(H_A // 2):
        cols = slice(p * LANES, (p + 1) * LANES)
        q2 = _split_pair(qka_ref[0, :, cols])
        o = _softmax_pv([(_dot_nt(q2, ka_ref[0, :, cols]), va_ref[0, :, cols])])
        oa_ref[0, :, cols] = (_merge_pair(o, CTX_LEN) * sza_ref[0, :, cols]).astype(bf16)
    for kp in range(HKV_B // 2):
        kcols = slice(kp * LANES, (kp + 1) * LANES)
        q3 = [qb_ref[0, :, kp * QCOLS_PAIR + i * LANES:kp * QCOLS_PAIR + (i + 1) * LANES].astype(f32)
              for i in range(G_B)]
        qs = _gqa_stack(q3)
        o = _softmax_pv([(_dot_nt(qs, kb_ref[0, :, kcols]), vb_ref[0, :, kcols])], extra_logit=sink_ref[kp])
        ocols = slice(kp * QCOLS_PAIR, (kp + 1) * QCOLS_PAIR)
        ob_ref[0, :, ocols] = (_gqa_unstack(o, CTX_LEN) * szb_ref[0, :, ocols]).astype(bf16)
    for h in range(H_C):
        qcols = slice(h * QC_PAD, (h + 1) * QC_PAD)
        vcols = slice(h * V_C, (h + 1) * V_C)
        s = _dot_nt(qc_ref[0, :, qcols], kc_ref[0, :, qcols]) * MLA_SCALE
        o = _softmax_pv([(s, vc_ref[0, :, vcols])])
        oc_ref[0, :, vcols] = (o * szc_ref[0, :, vcols]).astype(bf16)


def _ctx_attention(qka, va_vb, qkb, qc, kc, vc, sink, sza, szb, szc):
    cb = SEQ // CTX_LEN

    def spec(width, col):
        return pl.BlockSpec((1, CTX_LEN, width), lambda b: (b, cb, col))

    return pl.pallas_call(
        _ctx_kernel,
        grid=(BATCH,),
        in_specs=[
            spec(D_A, 0), spec(D_A, 1), spec(D_A, 0),
            spec(D_B, 0), spec(KV_B, D_B // KV_B), spec(KV_B, D_A // KV_B),
            spec(H_C * QC_PAD, 0), spec(H_C * QC_PAD, 0), spec(D_C, 0),
            pl.BlockSpec((HKV_B // 2, 2 * G_B * CTX_LEN, 1), lambda b: (0, 0, 0)),
            spec(D_A, 0), spec(D_B, 0), spec(D_C, 0),
        ],
        out_specs=[
            pl.BlockSpec((1, CTX_LEN, D_A), lambda b: (b, 0, 0)),
            pl.BlockSpec((1, CTX_LEN, D_B), lambda b: (b, 0, 0)),
            pl.BlockSpec((1, CTX_LEN, D_C), lambda b: (b, 0, 0)),
        ],
        out_shape=[
            jax.ShapeDtypeStruct((BATCH, CTX_LEN, D_A), bf16),
            jax.ShapeDtypeStruct((BATCH, CTX_LEN, D_B), bf16),
            jax.ShapeDtypeStruct((BATCH, CTX_LEN, D_C), bf16),
        ],
        compiler_params=_cparams("arbitrary"),
        name="ctx_attention",
    )(qka, qka, va_vb, qkb, qkb, va_vb, qc, kc, vc, _sink_rows(sink, CTX_LEN), sza, szb, szc)


def _out_kernel(x_ref, ga_ref, gb_ref, gc_ref, wa_ref, wb_ref, wc_ref, mod_ref, o_ref):
    y = _dot(ga_ref[0], wa_ref[...]) + _dot(gb_ref[0], wb_ref[...]) + _dot(gc_ref[0], wc_ref[...])
    o_ref[0] = x_ref[0] + mod_ref[0] * y


def _out_proj(x, ga, gb, gc, w_out, mod3, is_ctx):
    t = x.shape[1]
    tm = min(t, 1024)
    tn = 512
    gate_col0 = 2 * D_MODEL // tn
    wa = w_out[:D_A].astype(bf16)
    wb = w_out[D_A:D_A + D_B].astype(bf16)
    wc = w_out[D_A + D_B:].astype(bf16)
    mod_row = (lambda b: BATCH) if is_ctx else (lambda b: b)
    return pl.pallas_call(
        _out_kernel,
        grid=(BATCH, t // tm, D_MODEL // tn),
        in_specs=[
            pl.BlockSpec((1, tm, tn), lambda b, i, j: (b, i, j)),
            pl.BlockSpec((1, tm, D_A), lambda b, i, j: (b, i, 0)),
            pl.BlockSpec((1, tm, D_B), lambda b, i, j: (b, i, 0)),
            pl.BlockSpec((1, tm, D_C), lambda b, i, j: (b, i, 0)),
            pl.BlockSpec((D_A, tn), lambda b, i, j: (0, j)),
            pl.BlockSpec((D_B, tn), lambda b, i, j: (0, j)),
            pl.BlockSpec((D_C, tn), lambda b, i, j: (0, j)),
            pl.BlockSpec((1, 1, tn), lambda b, i, j: (mod_row(b), 0, gate_col0 + j)),
        ],
        out_specs=pl.BlockSpec((1, tm, tn), lambda b, i, j: (b, i, j)),
        out_shape=jax.ShapeDtypeStruct(x.shape, f32),
        compiler_params=_cparams("arbitrary", "arbitrary", "arbitrary"),
        name="out_proj_ctx" if is_ctx else "out_proj",
    )(x, ga, gb, gc, wa, wb, wc, mod3)


def _rope_tables():
    t = jnp.arange(SEQ)
    row = (t // GRID_W).astype(f32)
    col = (t % GRID_W).astype(f32)
    n_freq = ROPE_DIM // 4
    inv = ROPE_BASE ** (-jnp.arange(n_freq, dtype=f32) / n_freq)
    ar = row[:, None] * inv
    ac = col[:, None] * inv
    ang = jnp.concatenate([ar, ar, ac, ac], axis=-1)
    cos = jnp.cos(ang).astype(f32)
    sin = jnp.sin(ang).astype(f32)
    sign = jnp.asarray(np.where((np.arange(ROPE_DIM) % 32) < 16, -1.0, 1.0), dtype=f32)
    sin = sin * sign
    cos = jnp.concatenate([cos, jnp.ones((CTX_LEN, ROPE_DIM), f32)], axis=0)
    sin = jnp.concatenate([sin, jnp.zeros((CTX_LEN, ROPE_DIM), f32)], axis=0)
    pair = (jnp.concatenate([cos, cos], axis=1), jnp.concatenate([sin, sin], axis=1))
    single = (jnp.concatenate([cos, jnp.ones_like(cos)], axis=1), jnp.concatenate([sin, jnp.zeros_like(sin)], axis=1))
    return pair, single


def _pad_heads(w, width, padded):
    lead = w.shape[:-1]
    w = w.reshape(lead + (H_C, width))
    w = jnp.pad(w, [(0, 0)] * len(lead) + [(0, 0), (0, padded - width)])
    return w.reshape(lead + (H_C * padded,))


def kernel(x, c, ctx, c_ctx, norm_w, w_ada, b_ada, w_in, qn_a, kn_a, rpb_a, qn_b, kn_b, sink_b,
           qa_norm, kva_norm, w_qb, w_kvb, qn_c, kn_c, w_out):
    (cos2, sin2), (cos1, sin1) = _rope_tables()
    cvec = jnp.concatenate([c, c_ctx[None, :], jnp.zeros((MOD_ROWS - BATCH - 1, D_MODEL), f32)], axis=0)
    mod = _modulation(cvec, w_ada, b_ada)
    att_scale = HD ** -0.5

    o_qa, o_ka, o_va, o_qb, o_kb, o_vb, o_cq, o_ckv, o_kpe, o_z = np.cumsum(
        (0, D_A, D_A, D_A, D_B, KV_B, KV_B, Q_LORA, KV_LORA, ROPE_DIM)).tolist()

    xc = ctx
    for l in range(DEPTH):
        last = l == DEPTH - 1
        mod3 = mod[l].reshape(MOD_ROWS, 1, 3 * D_MODEL)
        h = _hnorm(x, xc, norm_w[l], mod3).reshape(R_ALL, D_MODEL)
        wl = w_in[l]

        def wseg(a, b):
            return wl[:, a:b].astype(bf16)

        gain_a = jnp.concatenate([jnp.tile(qn_a[l] * att_scale, H_A), jnp.tile(kn_a[l], H_A)])
        gain_b = jnp.concatenate([jnp.tile(qn_b[l] * att_scale, H_B), jnp.tile(kn_b[l], HKV_B)])
        qka = _proj(h, wseg(o_qa, o_va), "headnorm", 512, bf16, gain=gain_a, name="proj_qk_a")
        qkb = _proj(h, wseg(o_qb, o_vb), "headnorm_rope", 512, bf16, gain=gain_b, cos=cos2, sin=sin2,
                    name="proj_qk_b")
        w_v = jnp.concatenate([wl[:, o_va:o_qb], wl[:, o_vb:o_cq]], axis=1).astype(bf16)
        vab = _proj(h, w_v, "plain", D_A + KV_B, bf16, name="proj_v_ab")
        cqn = _proj(h, wseg(o_cq, o_ckv), "rowrms", Q_LORA, bf16, gain=qa_norm[l], name="proj_cq")
        ckvn = _proj(h, wseg(o_ckv, o_kpe), "rowrms", KV_LORA, bf16, gain=kva_norm[l], name="proj_ckv")
        w_pe = jnp.pad(wl[:, o_kpe:o_z], ((0, 0), (0, LANES - ROPE_DIM))).astype(bf16)
        kpe = _proj(h, w_pe, "plain", LANES, f32, name="proj_kpe")
        sza = _proj(h, wseg(o_z, o_z + D_A), "silu", 512, f32, name="proj_z_a")
        szb = _proj(h, wseg(o_z + D_A, o_z + D_A + D_B), "silu", 768, f32, name="proj_z_b")
        szc = _proj(h, wseg(o_z + D_A + D_B, o_z + D_MIX), "silu", 768, f32, name="proj_z_c")

        wq = _pad_heads(w_qb[l], QK_C, QC_PAD).astype(bf16)
        wkv = w_kvb[l].reshape(KV_LORA, H_C, NOPE_C + V_C)
        wkn = wkv[:, :, :NOPE_C].reshape(KV_LORA, H_C * NOPE_C).astype(bf16)
        wv = wkv[:, :, NOPE_C:].reshape(KV_LORA, H_C * V_C).astype(bf16)
        gq = _pad_heads(jnp.tile(qn_c[l], H_C), QK_C, QC_PAD).reshape(1, H_C * QC_PAD)
        gk = jnp.pad(kn_c[l], (0, QC_PAD - QK_C)).reshape(1, QC_PAD)
        qc, kc, vc = _mla_up(cqn, ckvn, kpe, wq, wkn, wv, gq, gk, cos1, sin1)

        def b3(a):
            return a.reshape(BATCH, S_ALL, a.shape[-1])

        qka, qkb, vab, qc, kc, vc, sza, szb, szc = map(b3, (qka, qkb, vab, qc, kc, vc, sza, szb, szc))

        ga = _natten(qka, vab, _natten_bias(rpb_a[l]), sza)
        gb = _swa(qkb, vab, sink_b[l], szb)
        gc = _mla(qc, kc, vc, szc)
        if not last:
            ga_c, gb_c, gc_c = _ctx_attention(qka, vab, qkb, qc, kc, vc, sink_b[l], sza, szb, szc)
            xc = _out_proj(xc, ga_c, gb_c, gc_c, w_out[l], mod3, True)
        x = _out_proj(x, ga, gb, gc, w_out[l], mod3, False)
    return x
```

```python
import functools
import math

import numpy as np
import jax
import jax.numpy as jnp
from jax import lax
from jax.experimental import pallas as pl
from jax.experimental.pallas import tpu as pltpu

D_MODEL = 2048
BATCH = 8
SEQ = 2048
DEPTH = 2
GRID_W = 64
CTX_LEN = 256
HD = 64
H_A = 8
D_A = H_A * HD
H_B = 12
HKV_B = 4
D_B = H_B * HD
KV_B = HKV_B * HD
H_C = 6
NOPE_C = 128
ROPE_DIM = 64
QK_C = NOPE_C + ROPE_DIM
V_C = 128
D_C = H_C * V_C
Q_LORA = 768
KV_LORA = 512
D_MIX = D_A + D_B + D_C
NA_ROWS = 8
NA_COLS = 16
SW_WINDOW = 128
ROPE_BASE = 10000.0
EPS = 1e-6
NEG_INF = -1e30

S_ALL = SEQ + CTX_LEN
R_ALL = BATCH * S_ALL
GRID_ROWS = SEQ // GRID_W
LANES = 128
MXU_DIM = 256
BF16_ROWS = 16
QC_PAD = 256
VT_ROWS = LANES + BF16_ROWS
MOD_ROWS = 16
VMEM_LIMIT = 48 * 1024 * 1024
LOG2E = math.log2(math.e)

bf16 = jnp.bfloat16
f32 = jnp.float32


def _cparams(*sem):
    return pltpu.CompilerParams(dimension_semantics=sem, vmem_limit_bytes=VMEM_LIMIT)


def _silu(v):
    return v / (1.0 + jnp.exp(-v))


def _lane_iota(shape):
    return lax.broadcasted_iota(jnp.int32, shape, len(shape) - 1)


def _rope_rotate(xn, cos, sin_signed):
    lane = _lane_iota(xn.shape)
    take_next = (lane % 32) < 16
    rot = jnp.where(take_next, pltpu.roll(xn, LANES - 16, 1), pltpu.roll(xn, 16, 1))
    return xn * cos + rot * sin_signed


def _dot_nt(a, b):
    return lax.dot_general(a, b, (((1,), (1,)), ((), ())), preferred_element_type=f32)


def _dot(a, b):
    return jnp.dot(a, b, preferred_element_type=f32)


def _mod_kernel(c_ref, w_ref, b_ref, o_ref):
    sc = _silu(c_ref[...]).astype(bf16)
    o_ref[0] = _dot(sc, w_ref[0].astype(bf16)) + b_ref[0]


def _modulation(cvec, w_ada, b_ada):
    tn = 1024
    n = 3 * D_MODEL
    return pl.pallas_call(
        _mod_kernel,
        grid=(DEPTH, n // tn),
        in_specs=[
            pl.BlockSpec((MOD_ROWS, D_MODEL), lambda l, j: (0, 0)),
            pl.BlockSpec((1, D_MODEL, tn), lambda l, j: (l, 0, j)),
            pl.BlockSpec((1, 1, tn), lambda l, j: (l, 0, j)),
        ],
        out_specs=pl.BlockSpec((1, MOD_ROWS, tn), lambda l, j: (l, 0, j)),
        out_shape=jax.ShapeDtypeStruct((DEPTH, MOD_ROWS, n), f32),
        compiler_params=_cparams("arbitrary", "arbitrary"),
        name="modulation",
    )(cvec, w_ada, b_ada.reshape(DEPTH, 1, n))


H_ROWS = 256
N_LAT_BLOCKS = SEQ // H_ROWS


def _hnorm_kernel(x_ref, ctx_ref, nw_ref, mod_ref, h_ref):
    s = pl.program_id(1)

    def body(src):
        x = src[0]
        ms = jnp.mean(x * x, axis=-1, keepdims=True)
        y = x * lax.rsqrt(ms + EPS) * nw_ref[...]
        shift = mod_ref[0, :, 0:D_MODEL]
        scale = mod_ref[0, :, D_MODEL:2 * D_MODEL]
        h_ref[0] = (y * (1.0 + scale) + shift).astype(bf16)

    @pl.when(s < N_LAT_BLOCKS)
    def _():
        body(x_ref)

    @pl.when(s >= N_LAT_BLOCKS)
    def _():
        body(ctx_ref)


def _hnorm(x, xc, norm_w, mod3):
    return pl.pallas_call(
        _hnorm_kernel,
        grid=(BATCH, S_ALL // H_ROWS),
        in_specs=[
            pl.BlockSpec((1, H_ROWS, D_MODEL), lambda b, s: (b, jnp.minimum(s, N_LAT_BLOCKS - 1), 0)),
            pl.BlockSpec((1, H_ROWS, D_MODEL), lambda b, s: (b, 0, 0)),
            pl.BlockSpec((1, D_MODEL), lambda b, s: (0, 0)),
            pl.BlockSpec((1, 1, 3 * D_MODEL), lambda b, s: (jnp.where(s < N_LAT_BLOCKS, b, BATCH), 0, 0)),
        ],
        out_specs=pl.BlockSpec((1, H_ROWS, D_MODEL), lambda b, s: (b, s, 0)),
        out_shape=jax.ShapeDtypeStruct((BATCH, S_ALL, D_MODEL), bf16),
        compiler_params=_cparams("arbitrary", "arbitrary"),
        name="hnorm",
    )(x, xc, norm_w.reshape(1, D_MODEL), mod3)


PROJ_TM = 768


def _group_mean_sq(u, g_ref):
    sq = u * u
    hi = sq.astype(bf16)
    lo = (sq - hi.astype(f32)).astype(bf16)
    g = g_ref[...]
    return _dot(hi, g) + _dot(lo, g)


def _proj_kernel(mode, tn, h_ref, w_ref, *rest):
    o_ref = rest[-1]
    u = _dot(h_ref[...], w_ref[...])
    if mode == "plain":
        o_ref[...] = u.astype(o_ref.dtype)
    elif mode == "silu":
        o_ref[...] = _silu(u).astype(o_ref.dtype)
    elif mode == "rowrms":
        gain_ref = rest[0]
        ms = jnp.mean(u * u, axis=-1, keepdims=True)
        o_ref[...] = (u * lax.rsqrt(ms + EPS) * gain_ref[...]).astype(o_ref.dtype)
    else:
        gain_ref, g_ref = rest[0], rest[1]
        for c0 in range(0, tn, MXU_DIM):
            uc = u[:, c0:c0 + MXU_DIM]
            ms = _group_mean_sq(uc, g_ref)
            xn = uc * lax.rsqrt(ms + EPS) * gain_ref[:, c0:c0 + MXU_DIM]
            if mode == "headnorm_rope":
                cos = rest[2][...]
                sin = rest[3][...]
                for c1 in range(0, MXU_DIM, LANES):
                    o_ref[:, c0 + c1:c0 + c1 + LANES] = _rope_rotate(
                        xn[:, c1:c1 + LANES], cos, sin).astype(o_ref.dtype)
            else:
                o_ref[:, c0:c0 + MXU_DIM] = xn.astype(o_ref.dtype)


def _group_matrix():
    idx = np.arange(MXU_DIM) // HD
    return jnp.asarray((idx[:, None] == idx[None, :]).astype(np.float32) / HD, dtype=bf16)


def _proj(h2d, w, mode, tn, out_dtype, gain=None, cos=None, sin=None, name="proj"):
    n = w.shape[1]
    tm = PROJ_TM
    in_specs = [
        pl.BlockSpec((tm, D_MODEL), lambda i, j: (i, 0)),
        pl.BlockSpec((D_MODEL, tn), lambda i, j: (0, j)),
    ]
    args = [h2d, w]
    if mode in ("rowrms", "headnorm", "headnorm_rope"):
        in_specs.append(pl.BlockSpec((1, tn), lambda i, j: (0, j)))
        args.append(gain.reshape(1, n))
    if mode in ("headnorm", "headnorm_rope"):
        in_specs.append(pl.BlockSpec((MXU_DIM, MXU_DIM), lambda i, j: (0, 0)))
        args.append(_group_matrix())
    if mode == "headnorm_rope":
        nt = S_ALL // tm
        in_specs.append(pl.BlockSpec((tm, LANES), lambda i, j: (i % nt, 0)))
        in_specs.append(pl.BlockSpec((tm, LANES), lambda i, j: (i % nt, 0)))
        args += [cos, sin]
    return pl.pallas_call(
        functools.partial(_proj_kernel, mode, tn),
        grid=(R_ALL // tm, n // tn),
        in_specs=in_specs,
        out_specs=pl.BlockSpec((tm, tn), lambda i, j: (i, j)),
        out_shape=jax.ShapeDtypeStruct((R_ALL, n), out_dtype),
        compiler_params=_cparams("arbitrary", "arbitrary"),
        name=name,
    )(*args)


def _store_vt(o_ref, ut):
    for p in range(o_ref.shape[0]):
        o_ref[p, 0:LANES, :] = ut[p * LANES:(p + 1) * LANES].astype(bf16)
        o_ref[p, LANES:VT_ROWS, :] = jnp.ones((VT_ROWS - LANES, ut.shape[1]), bf16)


def _projt_kernel(wt_ref, h_ref, o_ref):
    _store_vt(o_ref, _dot_nt(wt_ref[...], h_ref[...]))


def _proj_vt(h2d, wt, name):
    tm = PROJ_TM
    tiles = wt.shape[0] // LANES
    return pl.pallas_call(
        _projt_kernel,
        grid=(R_ALL // tm,),
        in_specs=[
            pl.BlockSpec(wt.shape, lambda i: (0, 0)),
            pl.BlockSpec((tm, D_MODEL), lambda i: (i, 0)),
        ],
        out_specs=pl.BlockSpec((tiles, VT_ROWS, tm), lambda i: (0, 0, i)),
        out_shape=jax.ShapeDtypeStruct((tiles, VT_ROWS, R_ALL), bf16),
        compiler_params=_cparams("arbitrary"),
        name=name,
    )(wt, h2d)


def _mla_up_kernel(cq_ref, ckv_ref, kpe_ref, wq_ref, wkn_ref, wvt_ref, gq_ref, gk_ref,
                   cos_ref, sin_ref, qc_ref, kc_ref, vt_ref):
    cos = cos_ref[...]
    sin = sin_ref[...]
    ckv = ckv_ref[...]
    _store_vt(vt_ref, _dot_nt(wvt_ref[...], ckv))

    uq = _dot(cq_ref[...], wq_ref[...])
    for h in range(H_C):
        blk = uq[:, h * QC_PAD:(h + 1) * QC_PAD]
        ms = jnp.sum(blk * blk, axis=-1, keepdims=True) * (1.0 / QK_C)
        xn = blk * lax.rsqrt(ms + EPS) * gq_ref[:, h * QC_PAD:(h + 1) * QC_PAD]
        qc_ref[:, h * QC_PAD:h * QC_PAD + LANES] = xn[:, :LANES].astype(bf16)
        qc_ref[:, h * QC_PAD + LANES:(h + 1) * QC_PAD] = _rope_rotate(xn[:, LANES:], cos, sin).astype(bf16)

    kn = _dot(ckv, wkn_ref[...])
    kpe = kpe_ref[...]
    ss_pe = jnp.sum(kpe * kpe, axis=-1, keepdims=True)
    for h in range(H_C):
        blk = kn[:, h * NOPE_C:(h + 1) * NOPE_C]
        ms = (jnp.sum(blk * blk, axis=-1, keepdims=True) + ss_pe) * (1.0 / QK_C)
        r = lax.rsqrt(ms + EPS)
        kc_ref[:, h * QC_PAD:h * QC_PAD + LANES] = (blk * r * gk_ref[:, 0:LANES]).astype(bf16)
        kc_ref[:, h * QC_PAD + LANES:(h + 1) * QC_PAD] = _rope_rotate(
            kpe * r * gk_ref[:, LANES:], cos, sin).astype(bf16)


def _mla_up(cqn, ckvn, kpe, wq, wkn, wvt, gq, gk, cos, sin):
    tm = PROJ_TM
    nt = S_ALL // tm
    row = lambda i: (i, 0)
    fixed = lambda i: (0, 0)
    return pl.pallas_call(
        _mla_up_kernel,
        grid=(R_ALL // tm,),
        in_specs=[
            pl.BlockSpec((tm, Q_LORA), row),
            pl.BlockSpec((tm, KV_LORA), row),
            pl.BlockSpec((tm, LANES), row),
            pl.BlockSpec((Q_LORA, H_C * QC_PAD), fixed),
            pl.BlockSpec((KV_LORA, H_C * NOPE_C), fixed),
            pl.BlockSpec((H_C * V_C, KV_LORA), fixed),
            pl.BlockSpec((1, H_C * QC_PAD), fixed),
            pl.BlockSpec((1, QC_PAD), fixed),
            pl.BlockSpec((tm, LANES), lambda i: (i % nt, 0)),
            pl.BlockSpec((tm, LANES), lambda i: (i % nt, 0)),
        ],
        out_specs=[
            pl.BlockSpec((tm, H_C * QC_PAD), row),
            pl.BlockSpec((tm, H_C * QC_PAD), row),
            pl.BlockSpec((H_C, VT_ROWS, tm), lambda i: (0, 0, i)),
        ],
        out_shape=[
            jax.ShapeDtypeStruct((R_ALL, H_C * QC_PAD), bf16),
            jax.ShapeDtypeStruct((R_ALL, H_C * QC_PAD), bf16),
            jax.ShapeDtypeStruct((H_C, VT_ROWS, R_ALL), bf16),
        ],
        compiler_params=_cparams("arbitrary"),
        name="mla_up",
    )(cqn, ckvn, kpe, wq, wkn, wvt, gq, gk, cos, sin)


def _pipelined(n, score_fn, value_fn):
    m = score_fn(0)
    for j in range(n):
        m_next = score_fn(j + 1) if j + 1 < n else None
        value_fn(j, m)
        m = m_next


def _col_max(s):
    return jnp.max(s, axis=0, keepdims=True)


def _prob(s, m):
    return jnp.exp2((s - m).astype(bf16))


def _pair_q(qv, half_even, half_odd):
    low = _lane_iota(qv.shape) < HD
    zero = jnp.zeros_like(qv)
    qa = jnp.where(low, qv, zero) if half_even == 0 else jnp.where(low, zero, pltpu.roll(qv, HD, 1))
    qb = jnp.where(low, zero, qv) if half_odd == 1 else jnp.where(low, pltpu.roll(qv, HD, 1), zero)
    return jnp.concatenate([qa, qb], axis=0).astype(bf16)


def _pair_out(acc, l, n, half_even, half_odd):
    o = acc[:LANES] / l
    t = jnp.concatenate([o[half_even * HD:(half_even + 1) * HD, 0:n],
                         o[half_odd * HD:(half_odd + 1) * HD, n:2 * n]], axis=0)
    return t.T


NA_QROWS = 2
NA_Q = NA_QROWS * GRID_W
NA_KROWS = NA_ROWS + NA_QROWS
NA_KEYS = NA_KROWS * GRID_W
NA_VARIANTS = ((0, 0, 0, -1), (0, -2, 0, -3), (0, -4, 1, -5), (0, -4, 0, -5), (0, -6, 0, -7))


def _na_window_row(j):
    return jnp.clip(NA_QROWS * j - NA_ROWS // 2, 0, GRID_ROWS - NA_ROWS)


def _natten_kernel(q_ref, k_ref, vt_ref, bias_ref, sz_ref, o_ref, s_scr):
    j = pl.program_id(1)
    k0 = pl.multiple_of(_na_window_row(j) * GRID_W, LANES)

    def scores(p):
        cols = slice(p * LANES, (p + 1) * LANES)
        q2 = _pair_q(q_ref[0, :, cols].astype(f32), 0, 1)
        s_loc = _dot_nt(k_ref[0, pl.ds(k0, NA_KEYS), cols], q2) + bias_ref[0, p]
        s_ctx = _dot_nt(k_ref[0, SEQ:S_ALL, cols], q2)
        s_scr[p % 2, 0:NA_KEYS, :] = s_loc
        s_scr[p % 2, NA_KEYS:, :] = s_ctx
        return jnp.maximum(_col_max(s_loc), _col_max(s_ctx))

    def values(p, m):
        cols = slice(p * LANES, (p + 1) * LANES)
        acc = (_dot(vt_ref[p, :, pl.ds(k0, NA_KEYS)], _prob(s_scr[p % 2, 0:NA_KEYS, :], m))
               + _dot(vt_ref[p, :, SEQ:S_ALL], _prob(s_scr[p % 2, NA_KEYS:, :], m)))
        o = _pair_out(acc, acc[LANES:LANES + 1], NA_Q, 0, 1)
        o_ref[0, :, cols] = (o * sz_ref[0, :, cols]).astype(bf16)

    _pipelined(H_A // 2, scores, values)


def _natten(qk, vt, bias, sz):
    def variant(b, j):
        v = jnp.where(j <= 1, j, jnp.where(j >= GRID_ROWS // NA_QROWS - 2, j - (GRID_ROWS // NA_QROWS - 5), 2))
        return (v, 0, 0, 0)

    return pl.pallas_call(
        _natten_kernel,
        grid=(BATCH, GRID_ROWS // NA_QROWS),
        in_specs=[
            pl.BlockSpec((1, NA_Q, D_A), lambda b, j: (b, j, 0)),
            pl.BlockSpec((1, S_ALL, D_A), lambda b, j: (b, 0, 1)),
            pl.BlockSpec((H_A // 2, VT_ROWS, S_ALL), lambda b, j: (0, 0, b)),
            pl.BlockSpec((1, H_A // 2, NA_KEYS, 2 * NA_Q), variant),
            pl.BlockSpec((1, NA_Q, D_A), lambda b, j: (b, j, 0)),
        ],
        out_specs=pl.BlockSpec((1, NA_Q, D_A), lambda b, j: (b, j, 0)),
        out_shape=jax.ShapeDtypeStruct((BATCH, SEQ, D_A), bf16),
        scratch_shapes=[pltpu.VMEM((2, NA_KEYS + CTX_LEN, 2 * NA_Q), f32)],
        compiler_params=_cparams("arbitrary", "arbitrary"),
        name="natten",
    )(qk, qk, vt, bias, sz)


def _natten_bias(rpb):
    n_dc = 2 * NA_COLS - 1
    c = np.arange(GRID_W)
    kc = np.arange(GRID_W)
    qstart = np.clip(c - NA_COLS // 2, 0, GRID_W - NA_COLS)
    col_ok = (kc[:, None] >= qstart[None, :]) & (kc[:, None] < qstart[None, :] + NA_COLS)
    dc = np.clip(kc[:, None] - c[None, :], -(NA_COLS - 1), NA_COLS - 1) + NA_COLS - 1
    onehot = jnp.asarray((dc[None] == np.arange(n_dc)[:, None, None]).astype(np.float32))
    exp = jnp.einsum("hrd,dkc->hrkc", rpb.astype(f32) * LOG2E, onehot, precision=lax.Precision.HIGHEST)
    pad = NA_KROWS + NA_ROWS - 1 - exp.shape[1]
    exp = jnp.pad(exp, ((0, 0), (0, pad), (0, 0), (0, 0)))
    kr = np.arange(NA_KROWS)
    tables = []
    for lo0, off0, lo1, off1 in NA_VARIANTS:
        per_row = []
        for lo, off in ((lo0, off0), (lo1, off1)):
            d0 = off + NA_ROWS - 1
            t = exp[:, d0:d0 + NA_KROWS]
            ok = ((kr >= lo) & (kr < lo + NA_ROWS))[:, None, None] & col_ok[None]
            per_row.append(jnp.where(jnp.asarray(ok)[None], t, NEG_INF))
        t = jnp.stack(per_row, axis=3)
        t = t.reshape(H_A // 2, 2, NA_KROWS, GRID_W, NA_QROWS, GRID_W)
        t = t.transpose(0, 2, 3, 1, 4, 5)
        tables.append(t.reshape(H_A // 2, NA_KEYS, 2 * NA_Q))
    return jnp.stack(tables, axis=0)


SW_Q = 128
SW_SPAN = SW_Q + 2 * SW_WINDOW
G_B = H_B // HKV_B
N_CHUNK_B = H_B // 2


def _gqa_halves(c):
    return ((2 * c) // G_B) % 2, ((2 * c + 1) // G_B) % 2


def _swa_kernel(q_ref, k_ref, vt_ref, mask_ref, sink_ref, sz_ref, o_ref, s_scr):
    n = pl.program_id(1)
    start = pl.multiple_of(jnp.clip((n - 1) * SW_Q, 0, SEQ - SW_SPAN), SW_Q)

    def scores(c):
        kp = c // G_B
        cols = slice(c * LANES, (c + 1) * LANES)
        kcols = slice(kp * LANES, (kp + 1) * LANES)
        q2 = _pair_q(q_ref[0, :, cols].astype(f32), *_gqa_halves(c))
        s_loc = _dot_nt(k_ref[0, pl.ds(start, SW_SPAN), kcols], q2) + mask_ref[0]
        s_ctx = _dot_nt(k_ref[0, SEQ:S_ALL, kcols], q2)
        s_scr[c % 2, 0:SW_SPAN, :] = s_loc
        s_scr[c % 2, SW_SPAN:, :] = s_ctx
        return jnp.maximum(jnp.maximum(_col_max(s_loc), _col_max(s_ctx)), sink_ref[c])

    def values(c, m):
        kp = c // G_B
        cols = slice(c * LANES, (c + 1) * LANES)
        acc = (_dot(vt_ref[kp, :, pl.ds(start, SW_SPAN)], _prob(s_scr[c % 2, 0:SW_SPAN, :], m))
               + _dot(vt_ref[kp, :, SEQ:S_ALL], _prob(s_scr[c % 2, SW_SPAN:, :], m)))
        l = acc[LANES:LANES + 1] + jnp.exp2(sink_ref[c] - m)
        o = _pair_out(acc, l, SW_Q, *_gqa_halves(c))
        o_ref[0, :, cols] = (o * sz_ref[0, :, cols]).astype(bf16)

    _pipelined(N_CHUNK_B, scores, values)


def _swa_mask():
    i = np.arange(SW_Q)
    j = np.arange(SW_SPAN)
    out = []
    for d0 in (0, SW_Q, 2 * SW_Q):
        ok = np.abs(d0 + i[None, :] - j[:, None]) <= SW_WINDOW
        m = np.where(ok, 0.0, NEG_INF).astype(np.float32)
        out.append(np.concatenate([m, m], axis=1))
    return jnp.asarray(np.stack(out))


def _sink_cols(sink, n):
    return jnp.repeat(sink.astype(f32).reshape(N_CHUNK_B, 2) * LOG2E, n, axis=1)[:, None, :]


def _swa(qk, vt, sink, sz):
    nb = SEQ // SW_Q

    def variant(b, n):
        return (jnp.where(n == 0, 0, jnp.where(n == nb - 1, 2, 1)), 0, 0)

    return pl.pallas_call(
        _swa_kernel,
        grid=(BATCH, nb),
        in_specs=[
            pl.BlockSpec((1, SW_Q, D_B), lambda b, n: (b, n, 0)),
            pl.BlockSpec((1, S_ALL, KV_B), lambda b, n: (b, 0, D_B // KV_B)),
            pl.BlockSpec((HKV_B // 2, VT_ROWS, S_ALL), lambda b, n: (H_A // HKV_B, 0, b)),
            pl.BlockSpec((1, SW_SPAN, 2 * SW_Q), variant),
            pl.BlockSpec((N_CHUNK_B, 1, 2 * SW_Q), lambda b, n: (0, 0, 0)),
            pl.BlockSpec((1, SW_Q, D_B), lambda b, n: (b, n, 0)),
        ],
        out_specs=pl.BlockSpec((1, SW_Q, D_B), lambda b, n: (b, n, 0)),
        out_shape=jax.ShapeDtypeStruct((BATCH, SEQ, D_B), bf16),
        scratch_shapes=[pltpu.VMEM((2, SW_SPAN + CTX_LEN, 2 * SW_Q), f32)],
        compiler_params=_cparams("arbitrary", "arbitrary"),
        name="swa",
    )(qk, qk, vt, _swa_mask(), _sink_cols(sink, SW_Q), sz)


MLA_TQ = 1024
MLA_SUB = 256
MLA_KCHUNK = 256
MLA_SCALE = float(QK_C) ** -0.5


def _mla_kernel(q_ref, k_ref, vt_ref, sz_ref, o_ref, s_scr):
    chunks = [slice(c * MLA_KCHUNK, (c + 1) * MLA_KCHUNK) for c in range(S_ALL // MLA_KCHUNK)]

    def scores(j):
        q = q_ref[0, j * MLA_SUB:(j + 1) * MLA_SUB, :]
        m = None
        for ck in chunks:
            st = _dot_nt(k_ref[0, ck, :], q)
            s_scr[j % 2, ck, :] = st
            mc = _col_max(st)
            m = mc if m is None else jnp.maximum(m, mc)
        return m

    def values(j, m):
        acc = None
        for ck in chunks:
            d = _dot(vt_ref[0, :, ck], _prob(s_scr[j % 2, ck, :], m))
            acc = d if acc is None else acc + d
        o = acc[:V_C] / acc[V_C:V_C + 1]
        rows = slice(j * MLA_SUB, (j + 1) * MLA_SUB)
        o_ref[0, rows, :] = (o.T * sz_ref[0, rows, :]).astype(bf16)

    _pipelined(MLA_TQ // MLA_SUB, scores, values)


def _mla(qc, kc, vt, sz):
    return pl.pallas_call(
        _mla_kernel,
        grid=(BATCH, H_C, SEQ // MLA_TQ),
        in_specs=[
            pl.BlockSpec((1, MLA_TQ, QC_PAD), lambda b, h, i: (b, i, h)),
            pl.BlockSpec((1, S_ALL, QC_PAD), lambda b, h, i: (b, 0, h)),
            pl.BlockSpec((1, VT_ROWS, S_ALL), lambda b, h, i: (h, 0, b)),
            pl.BlockSpec((1, MLA_TQ, V_C), lambda b, h, i: (b, i, h)),
        ],
        out_specs=pl.BlockSpec((1, MLA_TQ, V_C), lambda b, h, i: (b, i, h)),
        out_shape=jax.ShapeDtypeStruct((BATCH, SEQ, D_C), bf16),
        scratch_shapes=[pltpu.VMEM((2, S_ALL, MLA_SUB), f32)],
        compiler_params=_cparams("arbitrary", "arbitrary", "arbitrary"),
        name="mla",
    )(qc, kc, vt, sz)


def _ctx_kernel(qa_ref, ka_ref, qb_ref, kb_ref, vtab_ref, qc_ref, kc_ref, vtc_ref,
                sink_ref, sza_ref, szb_ref, szc_ref, oa_ref, ob_ref, oc_ref):
    n = CTX_LEN
    for p in range(H_A // 2):
        cols = slice(p * LANES, (p + 1) * LANES)
        s = _dot_nt(ka_ref[0, :, cols], _pair_q(qa_ref[0, :, cols].astype(f32), 0, 1))
        acc = _dot(vtab_ref[p], _prob(s, _col_max(s)))
        o = _pair_out(acc, acc[LANES:LANES + 1], n, 0, 1)
        oa_ref[0, :, cols] = (o * sza_ref[0, :, cols]).astype(bf16)
    for c in range(N_CHUNK_B):
        kp = c // G_B
        cols = slice(c * LANES, (c + 1) * LANES)
        s = _dot_nt(kb_ref[0, :, kp * LANES:(kp + 1) * LANES],
                    _pair_q(qb_ref[0, :, cols].astype(f32), *_gqa_halves(c)))
        m = jnp.maximum(_col_max(s), sink_ref[c])
        acc = _dot(vtab_ref[H_A // 2 + kp], _prob(s, m))
        l = acc[LANES:LANES + 1] + jnp.exp2(sink_ref[c] - m)
        o = _pair_out(acc, l, n, *_gqa_halves(c))
        ob_ref[0, :, cols] = (o * szb_ref[0, :, cols]).astype(bf16)
    for h in range(H_C):
        qcols = slice(h * QC_PAD, (h + 1) * QC_PAD)
        vcols = slice(h * V_C, (h + 1) * V_C)
        s = _dot_nt(kc_ref[0, :, qcols], qc_ref[0, :, qcols])
        acc = _dot(vtc_ref[h], _prob(s, _col_max(s)))
        o = acc[:V_C] / acc[V_C:V_C + 1]
        oc_ref[0, :, vcols] = (o.T * szc_ref[0, :, vcols]).astype(bf16)


def _ctx_attention(qka, qkb, vt_ab, qc, kc, vt_c, sink, sza, szb, szc):
    cb = SEQ // CTX_LEN

    def spec(width, col):
        return pl.BlockSpec((1, CTX_LEN, width), lambda b: (b, cb, col))

    def vt_spec(tiles):
        return pl.BlockSpec((tiles, VT_ROWS, CTX_LEN), lambda b: (0, 0, b * (S_ALL // CTX_LEN) + cb))

    return pl.pallas_call(
        _ctx_kernel,
        grid=(BATCH,),
        in_specs=[
            spec(D_A, 0), spec(D_A, 1),
            spec(D_B, 0), spec(KV_B, D_B // KV_B),
            vt_spec((H_A + HKV_B) // 2),
            spec(H_C * QC_PAD, 0), spec(H_C * QC_PAD, 0),
            vt_spec(H_C),
            pl.BlockSpec((N_CHUNK_B, 1, 2 * CTX_LEN), lambda b: (0, 0, 0)),
            spec(D_A, 0), spec(D_B, 0), spec(D_C, 0),
        ],
        out_specs=[
            pl.BlockSpec((1, CTX_LEN, D_A), lambda b: (b, 0, 0)),
            pl.BlockSpec((1, CTX_LEN, D_B), lambda b: (b, 0, 0)),
            pl.BlockSpec((1, CTX_LEN, D_C), lambda b: (b, 0, 0)),
        ],
        out_shape=[
            jax.ShapeDtypeStruct((BATCH, CTX_LEN, D_A), bf16),
            jax.ShapeDtypeStruct((BATCH, CTX_LEN, D_B), bf16),
            jax.ShapeDtypeStruct((BATCH, CTX_LEN, D_C), bf16),
        ],
        compiler_params=_cparams("arbitrary"),
        name="ctx_attention",
    )(qka, qka, qkb, qkb, vt_ab, qc, kc, vt_c, _sink_cols(sink, CTX_LEN), sza, szb, szc)


def _out_kernel(x_ref, ga_ref, gb_ref, gc_ref, wa_ref, wb_ref, wc_ref, mod_ref, o_ref):
    y = _dot(ga_ref[0], wa_ref[...]) + _dot(gb_ref[0], wb_ref[...]) + _dot(gc_ref[0], wc_ref[...])
    o_ref[0] = x_ref[0] + mod_ref[0] * y


def _out_proj(x, ga, gb, gc, w_out, mod3, is_ctx):
    t = x.shape[1]
    tm = min(t, 1024)
    tn = 512
    gate_col0 = 2 * D_MODEL // tn
    wa = w_out[:D_A].astype(bf16)
    wb = w_out[D_A:D_A + D_B].astype(bf16)
    wc = w_out[D_A + D_B:].astype(bf16)
    mod_row = (lambda b: BATCH) if is_ctx else (lambda b: b)
    return pl.pallas_call(
        _out_kernel,
        grid=(BATCH, t // tm, D_MODEL // tn),
        in_specs=[
            pl.BlockSpec((1, tm, tn), lambda b, i, j: (b, i, j)),
            pl.BlockSpec((1, tm, D_A), lambda b, i, j: (b, i, 0)),
            pl.BlockSpec((1, tm, D_B), lambda b, i, j: (b, i, 0)),
            pl.BlockSpec((1, tm, D_C), lambda b, i, j: (b, i, 0)),
            pl.BlockSpec((D_A, tn), lambda b, i, j: (0, j)),
            pl.BlockSpec((D_B, tn), lambda b, i, j: (0, j)),
            pl.BlockSpec((D_C, tn), lambda b, i, j: (0, j)),
            pl.BlockSpec((1, 1, tn), lambda b, i, j: (mod_row(b), 0, gate_col0 + j)),
        ],
        out_specs=pl.BlockSpec((1, tm, tn), lambda b, i, j: (b, i, j)),
        out_shape=jax.ShapeDtypeStruct(x.shape, f32),
        compiler_params=_cparams("arbitrary", "arbitrary", "arbitrary"),
        name="out_proj_ctx" if is_ctx else "out_proj",
    )(x, ga, gb, gc, wa, wb, wc, mod3)


def _rope_tables():
    t = jnp.arange(SEQ)
    row = (t // GRID_W).astype(f32)
    col = (t % GRID_W).astype(f32)
    n_freq = ROPE_DIM // 4
    inv = ROPE_BASE ** (-jnp.arange(n_freq, dtype=f32) / n_freq)
    ar = row[:, None] * inv
    ac = col[:, None] * inv
    ang = jnp.concatenate([ar, ar, ac, ac], axis=-1)
    cos = jnp.cos(ang).astype(f32)
    sin = jnp.sin(ang).astype(f32)
    sign = jnp.asarray(np.where((np.arange(ROPE_DIM) % 32) < 16, -1.0, 1.0), dtype=f32)
    sin = sin * sign
    cos = jnp.concatenate([cos, jnp.ones((CTX_LEN, ROPE_DIM), f32)], axis=0)
    sin = jnp.concatenate([sin, jnp.zeros((CTX_LEN, ROPE_DIM), f32)], axis=0)
    pair = (jnp.concatenate([cos, cos], axis=1), jnp.concatenate([sin, sin], axis=1))
    single = (jnp.concatenate([cos, jnp.ones_like(cos)], axis=1), jnp.concatenate([sin, jnp.zeros_like(sin)], axis=1))
    return pair, single


def _pad_heads(w, width, padded):
    lead = w.shape[:-1]
    w = w.reshape(lead + (H_C, width))
    w = jnp.pad(w, [(0, 0)] * len(lead) + [(0, 0), (0, padded - width)])
    return w.reshape(lead + (H_C * padded,))


def kernel(x, c, ctx, c_ctx, norm_w, w_ada, b_ada, w_in, qn_a, kn_a, rpb_a, qn_b, kn_b, sink_b,
           qa_norm, kva_norm, w_qb, w_kvb, qn_c, kn_c, w_out):
    (cos2, sin2), (cos1, sin1) = _rope_tables()
    cvec = jnp.concatenate([c, c_ctx[None, :], jnp.zeros((MOD_ROWS - BATCH - 1, D_MODEL), f32)], axis=0)
    mod = _modulation(cvec, w_ada, b_ada)
    q_scale_ab = HD ** -0.5 * LOG2E
    q_scale_c = MLA_SCALE * LOG2E

    o_qa, o_ka, o_va, o_qb, o_kb, o_vb, o_cq, o_ckv, o_kpe, o_z = np.cumsum(
        (0, D_A, D_A, D_A, D_B, KV_B, KV_B, Q_LORA, KV_LORA, ROPE_DIM)).tolist()

    xc = ctx
    for l in range(DEPTH):
        last = l == DEPTH - 1
        mod3 = mod[l].reshape(MOD_ROWS, 1, 3 * D_MODEL)
        h = _hnorm(x, xc, norm_w[l], mod3).reshape(R_ALL, D_MODEL)
        wl = w_in[l]

        def wseg(a, b):
            return wl[:, a:b].astype(bf16)

        gain_a = jnp.concatenate([jnp.tile(qn_a[l] * q_scale_ab, H_A), jnp.tile(kn_a[l], H_A)])
        gain_b = jnp.concatenate([jnp.tile(qn_b[l] * q_scale_ab, H_B), jnp.tile(kn_b[l], HKV_B)])
        qka = _proj(h, wseg(o_qa, o_va), "headnorm", 512, bf16, gain=gain_a, name="proj_qk_a")
        qkb = _proj(h, wseg(o_qb, o_vb), "headnorm_rope", 512, bf16, gain=gain_b, cos=cos2, sin=sin2,
                    name="proj_qk_b")
        wt_v = jnp.concatenate([wl[:, o_va:o_qb], wl[:, o_vb:o_cq]], axis=1).T.astype(bf16)
        vt_ab = _proj_vt(h, wt_v, "proj_vt_ab")
        cqn = _proj(h, wseg(o_cq, o_ckv), "rowrms", Q_LORA, bf16, gain=qa_norm[l], name="proj_cq")
        ckvn = _proj(h, wseg(o_ckv, o_kpe), "rowrms", KV_LORA, bf16, gain=kva_norm[l], name="proj_ckv")
        w_pe = jnp.pad(wl[:, o_kpe:o_z], ((0, 0), (0, LANES - ROPE_DIM))).astype(bf16)
        kpe = _proj(h, w_pe, "plain", LANES, f32, name="proj_kpe")
        sza = _proj(h, wseg(o_z, o_z + D_A), "silu", 512, f32, name="proj_z_a")
        szb = _proj(h, wseg(o_z + D_A, o_z + D_A + D_B), "silu", 768, f32, name="proj_z_b")
        szc = _proj(h, wseg(o_z + D_A + D_B, o_z + D_MIX), "silu", 768, f32, name="proj_z_c")

        wq = _pad_heads(w_qb[l], QK_C, QC_PAD).astype(bf16)
        wkv = w_kvb[l].reshape(KV_LORA, H_C, NOPE_C + V_C)
        wkn = wkv[:, :, :NOPE_C].reshape(KV_LORA, H_C * NOPE_C).astype(bf16)
        wvt = wkv[:, :, NOPE_C:].reshape(KV_LORA, H_C * V_C).T.astype(bf16)
        gq = _pad_heads(jnp.tile(qn_c[l] * q_scale_c, H_C), QK_C, QC_PAD).reshape(1, H_C * QC_PAD)
        gk = jnp.pad(kn_c[l], (0, QC_PAD - QK_C)).reshape(1, QC_PAD)
        qc, kc, vt_c = _mla_up(cqn, ckvn, kpe, wq, wkn, wvt, gq, gk, cos1, sin1)

        def b3(a):
            return a.reshape(BATCH, S_ALL, a.shape[-1])

        qka, qkb, qc, kc, sza, szb, szc = map(b3, (qka, qkb, qc, kc, sza, szb, szc))

        ga = _natten(qka, vt_ab, _natten_bias(rpb_a[l]), sza)
        gb = _swa(qkb, vt_ab, sink_b[l], szb)
        gc = _mla(qc, kc, vt_c, szc)
        if not last:
            ga_c, gb_c, gc_c = _ctx_attention(qka, qkb, vt_ab, qc, kc, vt_c, sink_b[l], sza, szb, szc)
            xc = _out_proj(xc, ga_c, gb_c, gc_c, w_out[l], mod3, True)
        x = _out_proj(x, ga, gb, gc, w_out[l], mod3, False)
    return x
```

```python
import functools
import math

import numpy as np
import jax
import jax.numpy as jnp
from jax import lax
from jax.experimental import pallas as pl
from jax.experimental.pallas import tpu as pltpu

D_MODEL = 2048
BATCH = 8
SEQ = 2048
DEPTH = 2
GRID_W = 64
CTX_LEN = 256
HD = 64
H_A = 8
D_A = H_A * HD
H_B = 12
HKV_B = 4
D_B = H_B * HD
KV_B = HKV_B * HD
H_C = 6
NOPE_C = 128
ROPE_DIM = 64
QK_C = NOPE_C + ROPE_DIM
V_C = 128
D_C = H_C * V_C
Q_LORA = 768
KV_LORA = 512
D_MIX = D_A + D_B + D_C
NA_ROWS = 8
NA_COLS = 16
SW_WINDOW = 128
ROPE_BASE = 10000.0
EPS = 1e-6
NEG_INF = -1e30

S_ALL = SEQ + CTX_LEN
R_ALL = BATCH * S_ALL
GRID_ROWS = SEQ // GRID_W
LANES = 128
MXU_DIM = 256
BF16_ROWS = 16
QC_PAD = 256
VT_ROWS = LANES + BF16_ROWS
MOD_ROWS = 16
VMEM_LIMIT = 48 * 1024 * 1024
LOG2E = math.log2(math.e)

bf16 = jnp.bfloat16
f32 = jnp.float32


def _cparams(*sem):
    return pltpu.CompilerParams(dimension_semantics=sem, vmem_limit_bytes=VMEM_LIMIT)


def _silu(v):
    return v / (1.0 + jnp.exp(-v))


def _lane_iota(shape):
    return lax.broadcasted_iota(jnp.int32, shape, len(shape) - 1)


def _rope_rotate(xn, cos, sin_signed):
    lane = _lane_iota(xn.shape)
    take_next = (lane % 32) < 16
    rot = jnp.where(take_next, pltpu.roll(xn, LANES - 16, 1), pltpu.roll(xn, 16, 1))
    return xn * cos + rot * sin_signed


def _dot_nt(a, b):
    return lax.dot_general(a, b, (((1,), (1,)), ((), ())), preferred_element_type=f32)


def _dot(a, b):
    return jnp.dot(a, b, preferred_element_type=f32)


def _mod_kernel(c_ref, w_ref, b_ref, o_ref):
    sc = _silu(c_ref[...]).astype(bf16)
    o_ref[0] = _dot(sc, w_ref[0].astype(bf16)) + b_ref[0]


def _modulation(cvec, w_ada, b_ada):
    tn = 1024
    n = 3 * D_MODEL
    return pl.pallas_call(
        _mod_kernel,
        grid=(DEPTH, n // tn),
        in_specs=[
            pl.BlockSpec((MOD_ROWS, D_MODEL), lambda l, j: (0, 0)),
            pl.BlockSpec((1, D_MODEL, tn), lambda l, j: (l, 0, j)),
            pl.BlockSpec((1, 1, tn), lambda l, j: (l, 0, j)),
        ],
        out_specs=pl.BlockSpec((1, MOD_ROWS, tn), lambda l, j: (l, 0, j)),
        out_shape=jax.ShapeDtypeStruct((DEPTH, MOD_ROWS, n), f32),
        compiler_params=_cparams("arbitrary", "arbitrary"),
        name="modulation",
    )(cvec, w_ada, b_ada.reshape(DEPTH, 1, n))


H_ROWS = 256
N_LAT_BLOCKS = SEQ // H_ROWS


def _hnorm_kernel(x_ref, ctx_ref, nw_ref, mod_ref, h_ref):
    s = pl.program_id(1)

    def body(src):
        x = src[0]
        ms = jnp.mean(x * x, axis=-1, keepdims=True)
        y = x * lax.rsqrt(ms + EPS) * nw_ref[...]
        shift = mod_ref[0, :, 0:D_MODEL]
        scale = mod_ref[0, :, D_MODEL:2 * D_MODEL]
        h_ref[0] = (y * (1.0 + scale) + shift).astype(bf16)

    @pl.when(s < N_LAT_BLOCKS)
    def _():
        body(x_ref)

    @pl.when(s >= N_LAT_BLOCKS)
    def _():
        body(ctx_ref)


def _hnorm(x, xc, norm_w, mod3):
    return pl.pallas_call(
        _hnorm_kernel,
        grid=(BATCH, S_ALL // H_ROWS),
        in_specs=[
            pl.BlockSpec((1, H_ROWS, D_MODEL), lambda b, s: (b, jnp.minimum(s, N_LAT_BLOCKS - 1), 0)),
            pl.BlockSpec((1, H_ROWS, D_MODEL), lambda b, s: (b, 0, 0)),
            pl.BlockSpec((1, D_MODEL), lambda b, s: (0, 0)),
            pl.BlockSpec((1, 1, 3 * D_MODEL), lambda b, s: (jnp.where(s < N_LAT_BLOCKS, b, BATCH), 0, 0)),
        ],
        out_specs=pl.BlockSpec((1, H_ROWS, D_MODEL), lambda b, s: (b, s, 0)),
        out_shape=jax.ShapeDtypeStruct((BATCH, S_ALL, D_MODEL), bf16),
        compiler_params=_cparams("arbitrary", "arbitrary"),
        name="hnorm",
    )(x, xc, norm_w.reshape(1, D_MODEL), mod3)


PROJ_TM = 768


def _group_mean_sq(u, g_ref):
    sq = u * u
    hi = sq.astype(bf16)
    lo = (sq - hi.astype(f32)).astype(bf16)
    g = g_ref[...]
    return _dot(hi, g) + _dot(lo, g)


def _proj_kernel(mode, tn, h_ref, w_ref, *rest):
    o_ref = rest[-1]
    u = _dot(h_ref[...], w_ref[...])
    if mode == "plain":
        o_ref[...] = u.astype(o_ref.dtype)
    elif mode == "silu":
        o_ref[...] = _silu(u).astype(o_ref.dtype)
    elif mode == "rowrms":
        gain_ref = rest[0]
        ms = jnp.mean(u * u, axis=-1, keepdims=True)
        o_ref[...] = (u * lax.rsqrt(ms + EPS) * gain_ref[...]).astype(o_ref.dtype)
    else:
        gain_ref, g_ref = rest[0], rest[1]
        for c0 in range(0, tn, MXU_DIM):
            uc = u[:, c0:c0 + MXU_DIM]
            ms = _group_mean_sq(uc, g_ref)
            xn = uc * lax.rsqrt(ms + EPS) * gain_ref[:, c0:c0 + MXU_DIM]
            if mode == "headnorm_rope":
                cos = rest[2][...]
                sin = rest[3][...]
                for c1 in range(0, MXU_DIM, LANES):
                    o_ref[:, c0 + c1:c0 + c1 + LANES] = _rope_rotate(
                        xn[:, c1:c1 + LANES], cos, sin).astype(o_ref.dtype)
            else:
                o_ref[:, c0:c0 + MXU_DIM] = xn.astype(o_ref.dtype)


def _group_matrix():
    idx = np.arange(MXU_DIM) // HD
    return jnp.asarray((idx[:, None] == idx[None, :]).astype(np.float32) / HD, dtype=bf16)


def _proj(h2d, w, mode, tn, out_dtype, gain=None, cos=None, sin=None, name="proj"):
    n = w.shape[1]
    tm = PROJ_TM
    in_specs = [
        pl.BlockSpec((tm, D_MODEL), lambda i, j: (i, 0)),
        pl.BlockSpec((D_MODEL, tn), lambda i, j: (0, j)),
    ]
    args = [h2d, w]
    if mode in ("rowrms", "headnorm", "headnorm_rope"):
        in_specs.append(pl.BlockSpec((1, tn), lambda i, j: (0, j)))
        args.append(gain.reshape(1, n))
    if mode in ("headnorm", "headnorm_rope"):
        in_specs.append(pl.BlockSpec((MXU_DIM, MXU_DIM), lambda i, j: (0, 0)))
        args.append(_group_matrix())
    if mode == "headnorm_rope":
        nt = S_ALL // tm
        in_specs.append(pl.BlockSpec((tm, LANES), lambda i, j: (i % nt, 0)))
        in_specs.append(pl.BlockSpec((tm, LANES), lambda i, j: (i % nt, 0)))
        args += [cos, sin]
    return pl.pallas_call(
        functools.partial(_proj_kernel, mode, tn),
        grid=(R_ALL // tm, n // tn),
        in_specs=in_specs,
        out_specs=pl.BlockSpec((tm, tn), lambda i, j: (i, j)),
        out_shape=jax.ShapeDtypeStruct((R_ALL, n), out_dtype),
        compiler_params=_cparams("arbitrary", "arbitrary"),
        name=name,
    )(*args)


def _store_vt(o_ref, ut):
    for p in range(o_ref.shape[0]):
        o_ref[p, 0:LANES, :] = ut[p * LANES:(p + 1) * LANES].astype(bf16)
        o_ref[p, LANES:VT_ROWS, :] = jnp.ones((VT_ROWS - LANES, ut.shape[1]), bf16)


def _projt_kernel(wt_ref, h_ref, o_ref):
    _store_vt(o_ref, _dot_nt(wt_ref[...], h_ref[...]))


def _proj_vt(h2d, wt, name):
    tm = PROJ_TM
    tiles = wt.shape[0] // LANES
    return pl.pallas_call(
        _projt_kernel,
        grid=(R_ALL // tm,),
        in_specs=[
            pl.BlockSpec(wt.shape, lambda i: (0, 0)),
            pl.BlockSpec((tm, D_MODEL), lambda i: (i, 0)),
        ],
        out_specs=pl.BlockSpec((tiles, VT_ROWS, tm), lambda i: (0, 0, i)),
        out_shape=jax.ShapeDtypeStruct((tiles, VT_ROWS, R_ALL), bf16),
        compiler_params=_cparams("arbitrary"),
        name=name,
    )(wt, h2d)


def _mla_up_kernel(cq_ref, ckv_ref, kpe_ref, wq_ref, wkn_ref, wvt_ref, gq_ref, gk_ref,
                   cos_ref, sin_ref, qc_ref, kc_ref, vt_ref):
    cos = cos_ref[...]
    sin = sin_ref[...]
    kpe = kpe_ref[...]
    ss_pe = jnp.sum(kpe * kpe, axis=-1, keepdims=True)
    k_rot = _rope_rotate(kpe * gk_ref[:, LANES:], cos, sin)

    def q_head(h):
        cols = slice(h * QC_PAD, (h + 1) * QC_PAD)
        return lambda: _dot(cq_ref[...], wq_ref[:, cols]), functools.partial(q_epilogue, h)

    def q_epilogue(h, u):
        ms = jnp.sum(u * u, axis=-1, keepdims=True) * (1.0 / QK_C)
        xn = u * lax.rsqrt(ms + EPS) * gq_ref[:, h * QC_PAD:(h + 1) * QC_PAD]
        qc_ref[:, h * QC_PAD:h * QC_PAD + LANES] = xn[:, :LANES].astype(bf16)
        qc_ref[:, h * QC_PAD + LANES:(h + 1) * QC_PAD] = _rope_rotate(xn[:, LANES:], cos, sin).astype(bf16)

    def k_head(h):
        cols = slice(h * NOPE_C, (h + 1) * NOPE_C)
        return lambda: _dot(ckv_ref[...], wkn_ref[:, cols]), functools.partial(k_epilogue, h)

    def k_epilogue(h, u):
        ms = (jnp.sum(u * u, axis=-1, keepdims=True) + ss_pe) * (1.0 / QK_C)
        r = lax.rsqrt(ms + EPS)
        kc_ref[:, h * QC_PAD:h * QC_PAD + LANES] = (u * r * gk_ref[:, 0:LANES]).astype(bf16)
        kc_ref[:, h * QC_PAD + LANES:(h + 1) * QC_PAD] = (k_rot * r).astype(bf16)

    def v_tile(p):
        rows = slice(p * LANES, (p + 1) * LANES)

        def epilogue(ut):
            vt_ref[p, 0:LANES, :] = ut.astype(bf16)
            vt_ref[p, LANES:VT_ROWS, :] = jnp.ones((VT_ROWS - LANES, ut.shape[1]), bf16)

        return lambda: _dot_nt(wvt_ref[rows, :], ckv_ref[...]), epilogue

    items = [q_head(h) for h in range(H_C)] + [k_head(h) for h in range(H_C)] + [v_tile(p) for p in range(H_C)]
    u = items[0][0]()
    for i, (_, epilogue) in enumerate(items):
        u_next = items[i + 1][0]() if i + 1 < len(items) else None
        epilogue(u)
        u = u_next


def _mla_up(cqn, ckvn, kpe, wq, wkn, wvt, gq, gk, cos, sin):
    tm = PROJ_TM
    nt = S_ALL // tm
    row = lambda i: (i, 0)
    fixed = lambda i: (0, 0)
    return pl.pallas_call(
        _mla_up_kernel,
        grid=(R_ALL // tm,),
        in_specs=[
            pl.BlockSpec((tm, Q_LORA), row),
            pl.BlockSpec((tm, KV_LORA), row),
            pl.BlockSpec((tm, LANES), row),
            pl.BlockSpec((Q_LORA, H_C * QC_PAD), fixed),
            pl.BlockSpec((KV_LORA, H_C * NOPE_C), fixed),
            pl.BlockSpec((H_C * V_C, KV_LORA), fixed),
            pl.BlockSpec((1, H_C * QC_PAD), fixed),
            pl.BlockSpec((1, QC_PAD), fixed),
            pl.BlockSpec((tm, LANES), lambda i: (i % nt, 0)),
            pl.BlockSpec((tm, LANES), lambda i: (i % nt, 0)),
        ],
        out_specs=[
            pl.BlockSpec((tm, H_C * QC_PAD), row),
            pl.BlockSpec((tm, H_C * QC_PAD), row),
            pl.BlockSpec((H_C, VT_ROWS, tm), lambda i: (0, 0, i)),
        ],
        out_shape=[
            jax.ShapeDtypeStruct((R_ALL, H_C * QC_PAD), bf16),
            jax.ShapeDtypeStruct((R_ALL, H_C * QC_PAD), bf16),
            jax.ShapeDtypeStruct((H_C, VT_ROWS, R_ALL), bf16),
        ],
        compiler_params=_cparams("arbitrary"),
        name="mla_up",
    )(cqn, ckvn, kpe, wq, wkn, wvt, gq, gk, cos, sin)


def _pipelined(n, score_fn, value_fn):
    m = score_fn(0)
    for j in range(n):
        m_next = score_fn(j + 1) if j + 1 < n else None
        value_fn(j, m)
        m = m_next


def _col_max(s):
    return jnp.max(s, axis=0, keepdims=True)


def _prob(s, m):
    return jnp.exp2((s - m).astype(bf16))


def _pair_q(qv, half_even, half_odd):
    low = _lane_iota(qv.shape) < HD
    zero = jnp.zeros_like(qv)
    qa = jnp.where(low, qv, zero) if half_even == 0 else jnp.where(low, zero, pltpu.roll(qv, HD, 1))
    qb = jnp.where(low, zero, qv) if half_odd == 1 else jnp.where(low, pltpu.roll(qv, HD, 1), zero)
    return jnp.concatenate([qa, qb], axis=0).astype(bf16)


def _pair_out(acc, l, n, half_even, half_odd):
    o = acc[:LANES] / l
    t = jnp.concatenate([o[half_even * HD:(half_even + 1) * HD, 0:n],
                         o[half_odd * HD:(half_odd + 1) * HD, n:2 * n]], axis=0)
    return t.T


NA_QROWS = 2
NA_Q = NA_QROWS * GRID_W
NA_KROWS = NA_ROWS + NA_QROWS
NA_KEYS = NA_KROWS * GRID_W
NA_VARIANTS = ((0, 0, 0, -1), (0, -2, 0, -3), (0, -4, 1, -5), (0, -4, 0, -5), (0, -6, 0, -7))


def _na_window_row(j):
    return jnp.clip(NA_QROWS * j - NA_ROWS // 2, 0, GRID_ROWS - NA_ROWS)


NA_SUB = 2
NA_BLOCKS = GRID_ROWS // NA_QROWS


def _natten_kernel(q_ref, k_ref, vt_ref, bias0_ref, bias1_ref, sz_ref, o_ref, s_scr):
    step = pl.program_id(1)
    bias_refs = (bias0_ref, bias1_ref)
    k0 = [pl.multiple_of(_na_window_row(NA_SUB * step + s) * GRID_W, LANES) for s in range(NA_SUB)]
    n_pair = H_A // 2

    def scores(i):
        s, p = divmod(i, n_pair)
        cols = slice(p * LANES, (p + 1) * LANES)
        q2 = _pair_q(q_ref[0, s * NA_Q:(s + 1) * NA_Q, cols].astype(f32), 0, 1)
        s_loc = _dot_nt(k_ref[0, pl.ds(k0[s], NA_KEYS), cols], q2) + bias_refs[s][0, p]
        s_ctx = _dot_nt(k_ref[0, SEQ:S_ALL, cols], q2)
        s_scr[i % 2, 0:NA_KEYS, :] = s_loc
        s_scr[i % 2, NA_KEYS:, :] = s_ctx
        return jnp.maximum(_col_max(s_loc), _col_max(s_ctx))

    def values(i, m):
        s, p = divmod(i, n_pair)
        cols = slice(p * LANES, (p + 1) * LANES)
        rows = slice(s * NA_Q, (s + 1) * NA_Q)
        acc = (_dot(vt_ref[p, :, pl.ds(k0[s], NA_KEYS)], _prob(s_scr[i % 2, 0:NA_KEYS, :], m))
               + _dot(vt_ref[p, :, SEQ:S_ALL], _prob(s_scr[i % 2, NA_KEYS:, :], m)))
        o = _pair_out(acc, acc[LANES:LANES + 1], NA_Q, 0, 1)
        o_ref[0, rows, cols] = (o * sz_ref[0, rows, cols]).astype(bf16)

    _pipelined(NA_SUB * n_pair, scores, values)


def _natten(qk, vt, bias, sz):
    def variant(s):
        def index(b, step):
            j = NA_SUB * step + s
            v = jnp.where(j <= 1, j, jnp.where(j >= NA_BLOCKS - 2, j - (NA_BLOCKS - 5), 2))
            return (v, 0, 0, 0)
        return index

    tq = NA_SUB * NA_Q
    bias_block = (1, H_A // 2, NA_KEYS, 2 * NA_Q)
    return pl.pallas_call(
        _natten_kernel,
        grid=(BATCH, NA_BLOCKS // NA_SUB),
        in_specs=[
            pl.BlockSpec((1, tq, D_A), lambda b, j: (b, j, 0)),
            pl.BlockSpec((1, S_ALL, D_A), lambda b, j: (b, 0, 1)),
            pl.BlockSpec((H_A // 2, VT_ROWS, S_ALL), lambda b, j: (0, 0, b)),
            pl.BlockSpec(bias_block, variant(0)),
            pl.BlockSpec(bias_block, variant(1)),
            pl.BlockSpec((1, tq, D_A), lambda b, j: (b, j, 0)),
        ],
        out_specs=pl.BlockSpec((1, tq, D_A), lambda b, j: (b, j, 0)),
        out_shape=jax.ShapeDtypeStruct((BATCH, SEQ, D_A), bf16),
        scratch_shapes=[pltpu.VMEM((2, NA_KEYS + CTX_LEN, 2 * NA_Q), f32)],
        compiler_params=_cparams("arbitrary", "arbitrary"),
        name="natten",
    )(qk, qk, vt, bias, bias, sz)


def _natten_bias(rpb):
    n_dc = 2 * NA_COLS - 1
    c = np.arange(GRID_W)
    kc = np.arange(GRID_W)
    qstart = np.clip(c - NA_COLS // 2, 0, GRID_W - NA_COLS)
    col_ok = (kc[:, None] >= qstart[None, :]) & (kc[:, None] < qstart[None, :] + NA_COLS)
    dc = np.clip(kc[:, None] - c[None, :], -(NA_COLS - 1), NA_COLS - 1) + NA_COLS - 1
    onehot = jnp.asarray((dc[None] == np.arange(n_dc)[:, None, None]).astype(np.float32))
    exp = jnp.einsum("hrd,dkc->hrkc", rpb.astype(f32) * LOG2E, onehot, precision=lax.Precision.HIGHEST)
    pad = NA_KROWS + NA_ROWS - 1 - exp.shape[1]
    exp = jnp.pad(exp, ((0, 0), (0, pad), (0, 0), (0, 0)))
    kr = np.arange(NA_KROWS)
    tables = []
    for lo0, off0, lo1, off1 in NA_VARIANTS:
        per_row = []
        for lo, off in ((lo0, off0), (lo1, off1)):
            d0 = off + NA_ROWS - 1
            t = exp[:, d0:d0 + NA_KROWS]
            ok = ((kr >= lo) & (kr < lo + NA_ROWS))[:, None, None] & col_ok[None]
            per_row.append(jnp.where(jnp.asarray(ok)[None], t, NEG_INF))
        t = jnp.stack(per_row, axis=3)
        t = t.reshape(H_A // 2, 2, NA_KROWS, GRID_W, NA_QROWS, GRID_W)
        t = t.transpose(0, 2, 3, 1, 4, 5)
        tables.append(t.reshape(H_A // 2, NA_KEYS, 2 * NA_Q))
    return jnp.stack(tables, axis=0)


SW_Q = 128
SW_SPAN = SW_Q + 2 * SW_WINDOW
G_B = H_B // HKV_B
N_CHUNK_B = H_B // 2


def _gqa_halves(c):
    return ((2 * c) // G_B) % 2, ((2 * c + 1) // G_B) % 2


SW_SUB = 4
SW_BLOCKS = SEQ // SW_Q


def _swa_kernel(q_ref, k_ref, vt_ref, mask_ref, sink_ref, sz_ref, o_ref, s_scr):
    step = pl.program_id(1)
    blocks = [SW_SUB * step + s for s in range(SW_SUB)]
    start = [pl.multiple_of(jnp.clip((n - 1) * SW_Q, 0, SEQ - SW_SPAN), SW_Q) for n in blocks]
    variant = [jnp.where(n == 0, 0, jnp.where(n == SW_BLOCKS - 1, 2, 1)) for n in blocks]

    def scores(i):
        s, c = divmod(i, N_CHUNK_B)
        kp = c // G_B
        cols = slice(c * LANES, (c + 1) * LANES)
        kcols = slice(kp * LANES, (kp + 1) * LANES)
        q2 = _pair_q(q_ref[0, s * SW_Q:(s + 1) * SW_Q, cols].astype(f32), *_gqa_halves(c))
        s_loc = _dot_nt(k_ref[0, pl.ds(start[s], SW_SPAN), kcols], q2) + mask_ref[variant[s]]
        s_ctx = _dot_nt(k_ref[0, SEQ:S_ALL, kcols], q2)
        s_scr[i % 2, 0:SW_SPAN, :] = s_loc
        s_scr[i % 2, SW_SPAN:, :] = s_ctx
        return jnp.maximum(jnp.maximum(_col_max(s_loc), _col_max(s_ctx)), sink_ref[c])

    def values(i, m):
        s, c = divmod(i, N_CHUNK_B)
        kp = c // G_B
        cols = slice(c * LANES, (c + 1) * LANES)
        rows = slice(s * SW_Q, (s + 1) * SW_Q)
        acc = (_dot(vt_ref[kp, :, pl.ds(start[s], SW_SPAN)], _prob(s_scr[i % 2, 0:SW_SPAN, :], m))
               + _dot(vt_ref[kp, :, SEQ:S_ALL], _prob(s_scr[i % 2, SW_SPAN:, :], m)))
        l = acc[LANES:LANES + 1] + jnp.exp2(sink_ref[c] - m)
        o = _pair_out(acc, l, SW_Q, *_gqa_halves(c))
        o_ref[0, rows, cols] = (o * sz_ref[0, rows, cols]).astype(bf16)

    _pipelined(SW_SUB * N_CHUNK_B, scores, values)


def _swa_mask():
    i = np.arange(SW_Q)
    j = np.arange(SW_SPAN)
    out = []
    for d0 in (0, SW_Q, 2 * SW_Q):
        ok = np.abs(d0 + i[None, :] - j[:, None]) <= SW_WINDOW
        m = np.where(ok, 0.0, NEG_INF).astype(np.float32)
        out.append(np.concatenate([m, m], axis=1))
    return jnp.asarray(np.stack(out))


def _sink_cols(sink, n):
    return jnp.repeat(sink.astype(f32).reshape(N_CHUNK_B, 2) * LOG2E, n, axis=1)[:, None, :]


def _swa(qk, vt, sink, sz):
    tq = SW_SUB * SW_Q
    return pl.pallas_call(
        _swa_kernel,
        grid=(BATCH, SW_BLOCKS // SW_SUB),
        in_specs=[
            pl.BlockSpec((1, tq, D_B), lambda b, n: (b, n, 0)),
            pl.BlockSpec((1, S_ALL, KV_B), lambda b, n: (b, 0, D_B // KV_B)),
            pl.BlockSpec((HKV_B // 2, VT_ROWS, S_ALL), lambda b, n: (H_A // HKV_B, 0, b)),
            pl.BlockSpec((3, SW_SPAN, 2 * SW_Q), lambda b, n: (0, 0, 0)),
            pl.BlockSpec((N_CHUNK_B, 1, 2 * SW_Q), lambda b, n: (0, 0, 0)),
            pl.BlockSpec((1, tq, D_B), lambda b, n: (b, n, 0)),
        ],
        out_specs=pl.BlockSpec((1, tq, D_B), lambda b, n: (b, n, 0)),
        out_shape=jax.ShapeDtypeStruct((BATCH, SEQ, D_B), bf16),
        scratch_shapes=[pltpu.VMEM((2, SW_SPAN + CTX_LEN, 2 * SW_Q), f32)],
        compiler_params=_cparams("arbitrary", "arbitrary"),
        name="swa",
    )(qk, qk, vt, _swa_mask(), _sink_cols(sink, SW_Q), sz)


MLA_TQ = 1024
MLA_SUB = 256
MLA_KCHUNK = 256
MLA_SCALE = float(QK_C) ** -0.5


def _mla_kernel(q_ref, k_ref, vt_ref, sz_ref, o_ref, s_scr):
    chunks = [slice(c * MLA_KCHUNK, (c + 1) * MLA_KCHUNK) for c in range(S_ALL // MLA_KCHUNK)]

    def scores(j):
        q = q_ref[0, j * MLA_SUB:(j + 1) * MLA_SUB, :]
        m = None
        for ck in chunks:
            st = _dot_nt(k_ref[0, ck, :], q)
            s_scr[j % 2, ck, :] = st
            mc = _col_max(st)
            m = mc if m is None else jnp.maximum(m, mc)
        return m

    def values(j, m):
        acc = None
        for ck in chunks:
            d = _dot(vt_ref[0, :, ck], _prob(s_scr[j % 2, ck, :], m))
            acc = d if acc is None else acc + d
        o = acc[:V_C] / acc[V_C:V_C + 1]
        rows = slice(j * MLA_SUB, (j + 1) * MLA_SUB)
        o_ref[0, rows, :] = (o.T * sz_ref[0, rows, :]).astype(bf16)

    _pipelined(MLA_TQ // MLA_SUB, scores, values)


def _mla(qc, kc, vt, sz):
    return pl.pallas_call(
        _mla_kernel,
        grid=(BATCH, H_C, SEQ // MLA_TQ),
        in_specs=[
            pl.BlockSpec((1, MLA_TQ, QC_PAD), lambda b, h, i: (b, i, h)),
            pl.BlockSpec((1, S_ALL, QC_PAD), lambda b, h, i: (b, 0, h)),
            pl.BlockSpec((1, VT_ROWS, S_ALL), lambda b, h, i: (h, 0, b)),
            pl.BlockSpec((1, MLA_TQ, V_C), lambda b, h, i: (b, i, h)),
        ],
        out_specs=pl.BlockSpec((1, MLA_TQ, V_C), lambda b, h, i: (b, i, h)),
        out_shape=jax.ShapeDtypeStruct((BATCH, SEQ, D_C), bf16),
        scratch_shapes=[pltpu.VMEM((2, S_ALL, MLA_SUB), f32)],
        compiler_params=_cparams("arbitrary", "arbitrary", "arbitrary"),
        name="mla",
    )(qc, kc, vt, sz)


def _ctx_kernel(qa_ref, ka_ref, qb_ref, kb_ref, vtab_ref, qc_ref, kc_ref, vtc_ref,
                sink_ref, sza_ref, szb_ref, szc_ref, oa_ref, ob_ref, oc_ref):
    n = CTX_LEN
    for p in range(H_A // 2):
        cols = slice(p * LANES, (p + 1) * LANES)
        s = _dot_nt(ka_ref[0, :, cols], _pair_q(qa_ref[0, :, cols].astype(f32), 0, 1))
        acc = _dot(vtab_ref[p], _prob(s, _col_max(s)))
        o = _pair_out(acc, acc[LANES:LANES + 1], n, 0, 1)
        oa_ref[0, :, cols] = (o * sza_ref[0, :, cols]).astype(bf16)
    for c in range(N_CHUNK_B):
        kp = c // G_B
        cols = slice(c * LANES, (c + 1) * LANES)
        s = _dot_nt(kb_ref[0, :, kp * LANES:(kp + 1) * LANES],
                    _pair_q(qb_ref[0, :, cols].astype(f32), *_gqa_halves(c)))
        m = jnp.maximum(_col_max(s), sink_ref[c])
        acc = _dot(vtab_ref[H_A // 2 + kp], _prob(s, m))
        l = acc[LANES:LANES + 1] + jnp.exp2(sink_ref[c] - m)
        o = _pair_out(acc, l, n, *_gqa_halves(c))
        ob_ref[0, :, cols] = (o * szb_ref[0, :, cols]).astype(bf16)
    for h in range(H_C):
        qcols = slice(h * QC_PAD, (h + 1) * QC_PAD)
        vcols = slice(h * V_C, (h + 1) * V_C)
        s = _dot_nt(kc_ref[0, :, qcols], qc_ref[0, :, qcols])
        acc = _dot(vtc_ref[h], _prob(s, _col_max(s)))
        o = acc[:V_C] / acc[V_C:V_C + 1]
        oc_ref[0, :, vcols] = (o.T * szc_ref[0, :, vcols]).astype(bf16)


def _ctx_attention(qka, qkb, vt_ab, qc, kc, vt_c, sink, sza, szb, szc):
    cb = SEQ // CTX_LEN

    def spec(width, col):
        return pl.BlockSpec((1, CTX_LEN, width), lambda b: (b, cb, col))

    def vt_spec(tiles):
        return pl.BlockSpec((tiles, VT_ROWS, CTX_LEN), lambda b: (0, 0, b * (S_ALL // CTX_LEN) + cb))

    return pl.pallas_call(
        _ctx_kernel,
        grid=(BATCH,),
        in_specs=[
            spec(D_A, 0), spec(D_A, 1),
            spec(D_B, 0), spec(KV_B, D_B // KV_B),
            vt_spec((H_A + HKV_B) // 2),
            spec(H_C * QC_PAD, 0), spec(H_C * QC_PAD, 0),
            vt_spec(H_C),
            pl.BlockSpec((N_CHUNK_B, 1, 2 * CTX_LEN), lambda b: (0, 0, 0)),
            spec(D_A, 0), spec(D_B, 0), spec(D_C, 0),
        ],
        out_specs=[
            pl.BlockSpec((1, CTX_LEN, D_A), lambda b: (b, 0, 0)),
            pl.BlockSpec((1, CTX_LEN, D_B), lambda b: (b, 0, 0)),
            pl.BlockSpec((1, CTX_LEN, D_C), lambda b: (b, 0, 0)),
        ],
        out_shape=[
            jax.ShapeDtypeStruct((BATCH, CTX_LEN, D_A), bf16),
            jax.ShapeDtypeStruct((BATCH, CTX_LEN, D_B), bf16),
            jax.ShapeDtypeStruct((BATCH, CTX_LEN, D_C), bf16),
        ],
        compiler_params=_cparams("arbitrary"),
        name="ctx_attention",
    )(qka, qka, qkb, qkb, vt_ab, qc, kc, vt_c, _sink_cols(sink, CTX_LEN), sza, szb, szc)


OUT_TM = 512
OUT_TN = 2048


def _out_kernel(x_ref, ga_ref, gb_ref, gc_ref, w_ref, mod_ref, o_ref):
    g = jnp.concatenate([ga_ref[0], gb_ref[0], gc_ref[0]], axis=1)
    o_ref[0] = x_ref[0] + mod_ref[0] * _dot(g, w_ref[...])


def _out_proj(x, ga, gb, gc, w_out_bf, mod3, is_ctx):
    t = x.shape[1]
    tm = min(t, OUT_TM)
    tn = OUT_TN
    gate_col0 = 2 * D_MODEL // tn
    mod_row = (lambda b: BATCH) if is_ctx else (lambda b: b)
    return pl.pallas_call(
        _out_kernel,
        grid=(BATCH, t // tm, D_MODEL // tn),
        in_specs=[
            pl.BlockSpec((1, tm, tn), lambda b, i, j: (b, i, j)),
            pl.BlockSpec((1, tm, D_A), lambda b, i, j: (b, i, 0)),
            pl.BlockSpec((1, tm, D_B), lambda b, i, j: (b, i, 0)),
            pl.BlockSpec((1, tm, D_C), lambda b, i, j: (b, i, 0)),
            pl.BlockSpec((D_MIX, tn), lambda b, i, j: (0, j)),
            pl.BlockSpec((1, 1, tn), lambda b, i, j: (mod_row(b), 0, gate_col0 + j)),
        ],
        out_specs=pl.BlockSpec((1, tm, tn), lambda b, i, j: (b, i, j)),
        out_shape=jax.ShapeDtypeStruct(x.shape, f32),
        compiler_params=_cparams("arbitrary", "arbitrary", "arbitrary"),
        name="out_proj_ctx" if is_ctx else "out_proj",
    )(x, ga, gb, gc, w_out_bf, mod3)


def _rope_tables():
    t = jnp.arange(SEQ)
    row = (t // GRID_W).astype(f32)
    col = (t % GRID_W).astype(f32)
    n_freq = ROPE_DIM // 4
    inv = ROPE_BASE ** (-jnp.arange(n_freq, dtype=f32) / n_freq)
    ar = row[:, None] * inv
    ac = col[:, None] * inv
    ang = jnp.concatenate([ar, ar, ac, ac], axis=-1)
    cos = jnp.cos(ang).astype(f32)
    sin = jnp.sin(ang).astype(f32)
    sign = jnp.asarray(np.where((np.arange(ROPE_DIM) % 32) < 16, -1.0, 1.0), dtype=f32)
    sin = sin * sign
    cos = jnp.concatenate([cos, jnp.ones((CTX_LEN, ROPE_DIM), f32)], axis=0)
    sin = jnp.concatenate([sin, jnp.zeros((CTX_LEN, ROPE_DIM), f32)], axis=0)
    pair = (jnp.concatenate([cos, cos], axis=1), jnp.concatenate([sin, sin], axis=1))
    single = (jnp.concatenate([cos, jnp.ones_like(cos)], axis=1), jnp.concatenate([sin, jnp.zeros_like(sin)], axis=1))
    return pair, single


def _pad_heads(w, width, padded):
    lead = w.shape[:-1]
    w = w.reshape(lead + (H_C, width))
    w = jnp.pad(w, [(0, 0)] * len(lead) + [(0, 0), (0, padded - width)])
    return w.reshape(lead + (H_C * padded,))


def kernel(x, c, ctx, c_ctx, norm_w, w_ada, b_ada, w_in, qn_a, kn_a, rpb_a, qn_b, kn_b, sink_b,
           qa_norm, kva_norm, w_qb, w_kvb, qn_c, kn_c, w_out):
    (cos2, sin2), (cos1, sin1) = _rope_tables()
    cvec = jnp.concatenate([c, c_ctx[None, :], jnp.zeros((MOD_ROWS - BATCH - 1, D_MODEL), f32)], axis=0)
    mod = _modulation(cvec, w_ada, b_ada)
    q_scale_ab = HD ** -0.5 * LOG2E
    q_scale_c = MLA_SCALE * LOG2E
    w_out_bf = w_out.astype(bf16)

    o_qa, o_ka, o_va, o_qb, o_kb, o_vb, o_cq, o_ckv, o_kpe, o_z = np.cumsum(
        (0, D_A, D_A, D_A, D_B, KV_B, KV_B, Q_LORA, KV_LORA, ROPE_DIM)).tolist()

    xc = ctx
    for l in range(DEPTH):
        last = l == DEPTH - 1
        mod3 = mod[l].reshape(MOD_ROWS, 1, 3 * D_MODEL)
        h = _hnorm(x, xc, norm_w[l], mod3).reshape(R_ALL, D_MODEL)
        wl = w_in[l]

        def wseg(a, b):
            return wl[:, a:b].astype(bf16)

        gain_a = jnp.concatenate([jnp.tile(qn_a[l] * q_scale_ab, H_A), jnp.tile(kn_a[l], H_A)])
        gain_b = jnp.concatenate([jnp.tile(qn_b[l] * q_scale_ab, H_B), jnp.tile(kn_b[l], HKV_B)])
        qka = _proj(h, wseg(o_qa, o_va), "headnorm", 512, bf16, gain=gain_a, name="proj_qk_a")
        qkb = _proj(h, wseg(o_qb, o_vb), "headnorm_rope", 512, bf16, gain=gain_b, cos=cos2, sin=sin2,
                    name="proj_qk_b")
        wt_v = jnp.concatenate([wl[:, o_va:o_qb], wl[:, o_vb:o_cq]], axis=1).T.astype(bf16)
        vt_ab = _proj_vt(h, wt_v, "proj_vt_ab")
        cqn = _proj(h, wseg(o_cq, o_ckv), "rowrms", Q_LORA, bf16, gain=qa_norm[l], name="proj_cq")
        ckvn = _proj(h, wseg(o_ckv, o_kpe), "rowrms", KV_LORA, bf16, gain=kva_norm[l], name="proj_ckv")
        w_pe = jnp.pad(wl[:, o_kpe:o_z], ((0, 0), (0, LANES - ROPE_DIM))).astype(bf16)
        kpe = _proj(h, w_pe, "plain", LANES, f32, name="proj_kpe")
        sza = _proj(h, wseg(o_z, o_z + D_A), "silu", 512, f32, name="proj_z_a")
        szb = _proj(h, wseg(o_z + D_A, o_z + D_A + D_B), "silu", 768, f32, name="proj_z_b")
        szc = _proj(h, wseg(o_z + D_A + D_B, o_z + D_MIX), "silu", 768, f32, name="proj_z_c")

        wq = _pad_heads(w_qb[l], QK_C, QC_PAD).astype(bf16)
        wkv = w_kvb[l].reshape(KV_LORA, H_C, NOPE_C + V_C)
        wkn = wkv[:, :, :NOPE_C].reshape(KV_LORA, H_C * NOPE_C).astype(bf16)
        wvt = wkv[:, :, NOPE_C:].reshape(KV_LORA, H_C * V_C).T.astype(bf16)
        gq = _pad_heads(jnp.tile(qn_c[l] * q_scale_c, H_C), QK_C, QC_PAD).reshape(1, H_C * QC_PAD)
        gk = jnp.pad(kn_c[l], (0, QC_PAD - QK_C)).reshape(1, QC_PAD)
        qc, kc, vt_c = _mla_up(cqn, ckvn, kpe, wq, wkn, wvt, gq, gk, cos1, sin1)

        def b3(a):
            return a.reshape(BATCH, S_ALL, a.shape[-1])

        qka, qkb, qc, kc, sza, szb, szc = map(b3, (qka, qkb, qc, kc, sza, szb, szc))

        ga = _natten(qka, vt_ab, _natten_bias(rpb_a[l]), sza)
        gb = _swa(qkb, vt_ab, sink_b[l], szb)
        gc = _mla(qc, kc, vt_c, szc)
        if not last:
            ga_c, gb_c, gc_c = _ctx_attention(qka, qkb, vt_ab, qc, kc, vt_c, sink_b[l], sza, szb, szc)
            xc = _out_proj(xc, ga_c, gb_c, gc_c, w_out_bf[l], mod3, True)
        x = _out_proj(x, ga, gb, gc, w_out_bf[l], mod3, False)
    return x
```

```python
import functools
import math

import numpy as np
import jax
import jax.numpy as jnp
from jax import lax
from jax.experimental import pallas as pl
from jax.experimental.pallas import tpu as pltpu

D_MODEL = 2048
BATCH = 8
SEQ = 2048
DEPTH = 2
GRID_W = 64
CTX_LEN = 256
HD = 64
H_A = 8
D_A = H_A * HD
H_B = 12
HKV_B = 4
D_B = H_B * HD
KV_B = HKV_B * HD
H_C = 6
NOPE_C = 128
ROPE_DIM = 64
QK_C = NOPE_C + ROPE_DIM
V_C = 128
D_C = H_C * V_C
Q_LORA = 768
KV_LORA = 512
D_MIX = D_A + D_B + D_C
NA_ROWS = 8
NA_COLS = 16
SW_WINDOW = 128
ROPE_BASE = 10000.0
EPS = 1e-6
NEG_INF = -1e30

S_ALL = SEQ + CTX_LEN
R_ALL = BATCH * S_ALL
GRID_ROWS = SEQ // GRID_W
LANES = 128
MXU_DIM = 256
BF16_ROWS = 16
QC_PAD = 256
VT_ROWS = LANES + BF16_ROWS
MOD_ROWS = 16
VMEM_LIMIT = 48 * 1024 * 1024
LOG2E = math.log2(math.e)

bf16 = jnp.bfloat16
f32 = jnp.float32


def _cparams(*sem):
    return pltpu.CompilerParams(dimension_semantics=sem, vmem_limit_bytes=VMEM_LIMIT)


def _silu(v):
    return v / (1.0 + jnp.exp(-v))


def _lane_iota(shape):
    return lax.broadcasted_iota(jnp.int32, shape, len(shape) - 1)


def _rope_rotate(xn, cos, sin_signed):
    lane = _lane_iota(xn.shape)
    take_next = (lane % 32) < 16
    rot = jnp.where(take_next, pltpu.roll(xn, LANES - 16, 1), pltpu.roll(xn, 16, 1))
    return xn * cos + rot * sin_signed


def _dot_nt(a, b):
    return lax.dot_general(a, b, (((1,), (1,)), ((), ())), preferred_element_type=f32)


def _dot(a, b):
    return jnp.dot(a, b, preferred_element_type=f32)


def _mod_kernel(c_ref, w_ref, b_ref, o_ref):
    sc = _silu(c_ref[...]).astype(bf16)
    o_ref[0] = _dot(sc, w_ref[0].astype(bf16)) + b_ref[0]


def _modulation(cvec, w_ada, b_ada):
    tn = 1024
    n = 3 * D_MODEL
    return pl.pallas_call(
        _mod_kernel,
        grid=(DEPTH, n // tn),
        in_specs=[
            pl.BlockSpec((MOD_ROWS, D_MODEL), lambda l, j: (0, 0)),
            pl.BlockSpec((1, D_MODEL, tn), lambda l, j: (l, 0, j)),
            pl.BlockSpec((1, 1, tn), lambda l, j: (l, 0, j)),
        ],
        out_specs=pl.BlockSpec((1, MOD_ROWS, tn), lambda l, j: (l, 0, j)),
        out_shape=jax.ShapeDtypeStruct((DEPTH, MOD_ROWS, n), f32),
        compiler_params=_cparams("arbitrary", "arbitrary"),
        name="modulation",
    )(cvec, w_ada, b_ada.reshape(DEPTH, 1, n))


H_ROWS = 256
N_LAT_BLOCKS = SEQ // H_ROWS


def _hnorm_kernel(x_ref, ctx_ref, nw_ref, mod_ref, h_ref):
    s = pl.program_id(1)

    def body(src):
        x = src[0]
        ms = jnp.mean(x * x, axis=-1, keepdims=True)
        y = x * lax.rsqrt(ms + EPS) * nw_ref[...]
        shift = mod_ref[0, :, 0:D_MODEL]
        scale = mod_ref[0, :, D_MODEL:2 * D_MODEL]
        h_ref[0] = (y * (1.0 + scale) + shift).astype(bf16)

    @pl.when(s < N_LAT_BLOCKS)
    def _():
        body(x_ref)

    @pl.when(s >= N_LAT_BLOCKS)
    def _():
        body(ctx_ref)


def _hnorm(x, xc, norm_w, mod3):
    return pl.pallas_call(
        _hnorm_kernel,
        grid=(BATCH, S_ALL // H_ROWS),
        in_specs=[
            pl.BlockSpec((1, H_ROWS, D_MODEL), lambda b, s: (b, jnp.minimum(s, N_LAT_BLOCKS - 1), 0)),
            pl.BlockSpec((1, H_ROWS, D_MODEL), lambda b, s: (b, 0, 0)),
            pl.BlockSpec((1, D_MODEL), lambda b, s: (0, 0)),
            pl.BlockSpec((1, 1, 3 * D_MODEL), lambda b, s: (jnp.where(s < N_LAT_BLOCKS, b, BATCH), 0, 0)),
        ],
        out_specs=pl.BlockSpec((1, H_ROWS, D_MODEL), lambda b, s: (b, s, 0)),
        out_shape=jax.ShapeDtypeStruct((BATCH, S_ALL, D_MODEL), bf16),
        compiler_params=_cparams("arbitrary", "arbitrary"),
        name="hnorm",
    )(x, xc, norm_w.reshape(1, D_MODEL), mod3)


PROJ_TM = 768


def _group_mean_sq(u, g_ref):
    sq = u * u
    hi = sq.astype(bf16)
    lo = (sq - hi.astype(f32)).astype(bf16)
    g = g_ref[...]
    return _dot(hi, g) + _dot(lo, g)


def _group_matrix():
    idx = np.arange(MXU_DIM) // HD
    return jnp.asarray((idx[:, None] == idx[None, :]).astype(np.float32) / HD, dtype=bf16)


def _run_items(items):
    u = items[0][0]()
    for i, (_, epilogue) in enumerate(items):
        u_next = items[i + 1][0]() if i + 1 < len(items) else None
        epilogue(u)
        u = u_next


def _store_vt_tile(vt_ref, p, ut):
    vt_ref[p, 0:LANES, :] = ut.astype(bf16)
    vt_ref[p, LANES:VT_ROWS, :] = jnp.ones((VT_ROWS - LANES, ut.shape[1]), bf16)


PROJ_TN = 512
N_HN = 2 * D_A + D_B + KV_B
N_LORA = Q_LORA + KV_LORA + LANES
N_VT = D_A + KV_B


def _proj_main_kernel(h_ref, whn_ref, wl_ref, wvt_ref, ghn_ref, g_ref, cos_ref, sin_ref, gcq_ref, gckv_ref,
                      qka_ref, qkb_ref, cqn_ref, ckvn_ref, kpe_ref, vt_ref):
    def hn_item(k):
        cols = slice(k * PROJ_TN, (k + 1) * PROJ_TN)
        rope = k * PROJ_TN >= 2 * D_A
        o_ref = qkb_ref if rope else qka_ref
        o0 = k * PROJ_TN - (2 * D_A if rope else 0)

        def epilogue(u):
            for c0 in range(0, PROJ_TN, MXU_DIM):
                uc = u[:, c0:c0 + MXU_DIM]
                ms = _group_mean_sq(uc, g_ref)
                xn = uc * lax.rsqrt(ms + EPS) * ghn_ref[:, k * PROJ_TN + c0:k * PROJ_TN + c0 + MXU_DIM]
                if rope:
                    for c1 in range(0, MXU_DIM, LANES):
                        o_ref[:, o0 + c0 + c1:o0 + c0 + c1 + LANES] = _rope_rotate(
                            xn[:, c1:c1 + LANES], cos_ref[...], sin_ref[...]).astype(bf16)
                else:
                    o_ref[:, o0 + c0:o0 + c0 + MXU_DIM] = xn.astype(bf16)

        return lambda: _dot(h_ref[...], whn_ref[:, cols]), epilogue

    def rowrms_item(c0, width, gain_ref, o_ref):
        def epilogue(u):
            ms = jnp.mean(u * u, axis=-1, keepdims=True)
            o_ref[...] = (u * lax.rsqrt(ms + EPS) * gain_ref[...]).astype(bf16)

        return lambda: _dot(h_ref[...], wl_ref[:, c0:c0 + width]), epilogue

    def kpe_item():
        def epilogue(u):
            kpe_ref[...] = u

        return lambda: _dot(h_ref[...], wl_ref[:, Q_LORA + KV_LORA:N_LORA]), epilogue

    def vt_item(p):
        def epilogue(ut):
            _store_vt_tile(vt_ref, p, ut)

        return lambda: _dot_nt(wvt_ref[p * LANES:(p + 1) * LANES, :], h_ref[...]), epilogue

    _run_items([hn_item(k) for k in range(N_HN // PROJ_TN)]
               + [rowrms_item(0, Q_LORA, gcq_ref, cqn_ref), rowrms_item(Q_LORA, KV_LORA, gckv_ref, ckvn_ref),
                  kpe_item()]
               + [vt_item(p) for p in range(N_VT // LANES)])


def _resident(shape):
    return pl.BlockSpec(shape, lambda i: (0,) * len(shape), pipeline_mode=pl.Buffered(1))


def _proj_main(h2d, whn, wl, wvt, ghn, gcq, gckv, cos, sin):
    tm = PROJ_TM
    nt = S_ALL // tm
    row = lambda i: (i, 0)
    tiles = N_VT // LANES
    return pl.pallas_call(
        _proj_main_kernel,
        grid=(R_ALL // tm,),
        in_specs=[
            pl.BlockSpec((tm, D_MODEL), row),
            _resident((D_MODEL, N_HN)), _resident((D_MODEL, N_LORA)), _resident((N_VT, D_MODEL)),
            _resident((1, N_HN)), _resident((MXU_DIM, MXU_DIM)),
            pl.BlockSpec((tm, LANES), lambda i: (i % nt, 0)),
            pl.BlockSpec((tm, LANES), lambda i: (i % nt, 0)),
            _resident((1, Q_LORA)), _resident((1, KV_LORA)),
        ],
        out_specs=[
            pl.BlockSpec((tm, 2 * D_A), row),
            pl.BlockSpec((tm, D_B + KV_B), row),
            pl.BlockSpec((tm, Q_LORA), row),
            pl.BlockSpec((tm, KV_LORA), row),
            pl.BlockSpec((tm, LANES), row),
            pl.BlockSpec((tiles, VT_ROWS, tm), lambda i: (0, 0, i)),
        ],
        out_shape=[
            jax.ShapeDtypeStruct((R_ALL, 2 * D_A), bf16),
            jax.ShapeDtypeStruct((R_ALL, D_B + KV_B), bf16),
            jax.ShapeDtypeStruct((R_ALL, Q_LORA), bf16),
            jax.ShapeDtypeStruct((R_ALL, KV_LORA), bf16),
            jax.ShapeDtypeStruct((R_ALL, LANES), f32),
            jax.ShapeDtypeStruct((tiles, VT_ROWS, R_ALL), bf16),
        ],
        compiler_params=_cparams("arbitrary"),
        name="proj_main",
    )(h2d, whn, wl, wvt, ghn.reshape(1, N_HN), _group_matrix(), cos, sin,
      gcq.reshape(1, Q_LORA), gckv.reshape(1, KV_LORA))


def _proj_gate_kernel(h_ref, wz_ref, sza_ref, szb_ref, szc_ref):
    outs = ((sza_ref, 0, D_A), (szb_ref, D_A, D_B), (szc_ref, D_A + D_B, D_C))

    def item(k):
        c0 = k * PROJ_TN

        def epilogue(u):
            sz = _silu(u)
            for o_ref, start, width in outs:
                lo, hi = max(c0, start), min(c0 + PROJ_TN, start + width)
                if lo < hi:
                    o_ref[:, lo - start:hi - start] = sz[:, lo - c0:hi - c0]

        return lambda: _dot(h_ref[...], wz_ref[:, c0:c0 + PROJ_TN]), epilogue

    _run_items([item(k) for k in range(D_MIX // PROJ_TN)])


def _proj_gate(h2d, wz):
    tm = PROJ_TM
    row = lambda i: (i, 0)
    return pl.pallas_call(
        _proj_gate_kernel,
        grid=(R_ALL // tm,),
        in_specs=[pl.BlockSpec((tm, D_MODEL), row), _resident((D_MODEL, D_MIX))],
        out_specs=[pl.BlockSpec((tm, D_A), row), pl.BlockSpec((tm, D_B), row), pl.BlockSpec((tm, D_C), row)],
        out_shape=[jax.ShapeDtypeStruct((R_ALL, D_A), f32), jax.ShapeDtypeStruct((R_ALL, D_B), f32),
                   jax.ShapeDtypeStruct((R_ALL, D_C), f32)],
        compiler_params=_cparams("arbitrary"),
        name="proj_gate",
    )(h2d, wz)


def _mla_up_kernel(cq_ref, ckv_ref, kpe_ref, wq_ref, wkn_ref, wvt_ref, gq_ref, gk_ref,
                   cos_ref, sin_ref, qc_ref, kc_ref, vt_ref):
    cos = cos_ref[...]
    sin = sin_ref[...]
    kpe = kpe_ref[...]
    ss_pe = jnp.sum(kpe * kpe, axis=-1, keepdims=True)
    k_rot = _rope_rotate(kpe * gk_ref[:, LANES:], cos, sin)

    def q_head(h):
        cols = slice(h * QC_PAD, (h + 1) * QC_PAD)
        return lambda: _dot(cq_ref[...], wq_ref[:, cols]), functools.partial(q_epilogue, h)

    def q_epilogue(h, u):
        ms = jnp.sum(u * u, axis=-1, keepdims=True) * (1.0 / QK_C)
        xn = u * lax.rsqrt(ms + EPS) * gq_ref[:, h * QC_PAD:(h + 1) * QC_PAD]
        qc_ref[:, h * QC_PAD:h * QC_PAD + LANES] = xn[:, :LANES].astype(bf16)
        qc_ref[:, h * QC_PAD + LANES:(h + 1) * QC_PAD] = _rope_rotate(xn[:, LANES:], cos, sin).astype(bf16)

    def k_head(h):
        cols = slice(h * NOPE_C, (h + 1) * NOPE_C)
        return lambda: _dot(ckv_ref[...], wkn_ref[:, cols]), functools.partial(k_epilogue, h)

    def k_epilogue(h, u):
        ms = (jnp.sum(u * u, axis=-1, keepdims=True) + ss_pe) * (1.0 / QK_C)
        r = lax.rsqrt(ms + EPS)
        kc_ref[:, h * QC_PAD:h * QC_PAD + LANES] = (u * r * gk_ref[:, 0:LANES]).astype(bf16)
        kc_ref[:, h * QC_PAD + LANES:(h + 1) * QC_PAD] = (k_rot * r).astype(bf16)

    def v_tile(p):
        rows = slice(p * LANES, (p + 1) * LANES)

        def epilogue(ut):
            _store_vt_tile(vt_ref, p, ut)

        return lambda: _dot_nt(wvt_ref[rows, :], ckv_ref[...]), epilogue

    _run_items([q_head(h) for h in range(H_C)] + [k_head(h) for h in range(H_C)] + [v_tile(p) for p in range(H_C)])


def _mla_up(cqn, ckvn, kpe, wq, wkn, wvt, gq, gk, cos, sin):
    tm = PROJ_TM
    nt = S_ALL // tm
    row = lambda i: (i, 0)
    fixed = lambda i: (0, 0)
    return pl.pallas_call(
        _mla_up_kernel,
        grid=(R_ALL // tm,),
        in_specs=[
            pl.BlockSpec((tm, Q_LORA), row),
            pl.BlockSpec((tm, KV_LORA), row),
            pl.BlockSpec((tm, LANES), row),
            pl.BlockSpec((Q_LORA, H_C * QC_PAD), fixed),
            pl.BlockSpec((KV_LORA, H_C * NOPE_C), fixed),
            pl.BlockSpec((H_C * V_C, KV_LORA), fixed),
            pl.BlockSpec((1, H_C * QC_PAD), fixed),
            pl.BlockSpec((1, QC_PAD), fixed),
            pl.BlockSpec((tm, LANES), lambda i: (i % nt, 0)),
            pl.BlockSpec((tm, LANES), lambda i: (i % nt, 0)),
        ],
        out_specs=[
            pl.BlockSpec((tm, H_C * QC_PAD), row),
            pl.BlockSpec((tm, H_C * QC_PAD), row),
            pl.BlockSpec((H_C, VT_ROWS, tm), lambda i: (0, 0, i)),
        ],
        out_shape=[
            jax.ShapeDtypeStruct((R_ALL, H_C * QC_PAD), bf16),
            jax.ShapeDtypeStruct((R_ALL, H_C * QC_PAD), bf16),
            jax.ShapeDtypeStruct((H_C, VT_ROWS, R_ALL), bf16),
        ],
        compiler_params=_cparams("arbitrary"),
        name="mla_up",
    )(cqn, ckvn, kpe, wq, wkn, wvt, gq, gk, cos, sin)


def _pipelined(n, score_fn, value_fn):
    m = score_fn(0)
    for j in range(n):
        m_next = score_fn(j + 1) if j + 1 < n else None
        value_fn(j, m)
        m = m_next


def _col_max(s):
    return jnp.max(s, axis=0, keepdims=True)


def _prob(s, m):
    return jnp.exp2((s - m).astype(bf16))


def _pair_q(qv, half_even, half_odd):
    low = _lane_iota(qv.shape) < HD
    zero = jnp.zeros_like(qv)
    qa = jnp.where(low, qv, zero) if half_even == 0 else jnp.where(low, zero, pltpu.roll(qv, HD, 1))
    qb = jnp.where(low, zero, qv) if half_odd == 1 else jnp.where(low, pltpu.roll(qv, HD, 1), zero)
    return jnp.concatenate([qa, qb], axis=0).astype(bf16)


def _pair_out(acc, l, n, half_even, half_odd):
    o = acc[:LANES] / l
    t = jnp.concatenate([o[half_even * HD:(half_even + 1) * HD, 0:n],
                         o[half_odd * HD:(half_odd + 1) * HD, n:2 * n]], axis=0)
    return t.T


NA_QROWS = 2
NA_Q = NA_QROWS * GRID_W
NA_KROWS = NA_ROWS + NA_QROWS
NA_KEYS = NA_KROWS * GRID_W
NA_VARIANTS = ((0, 0, 0, -1), (0, -2, 0, -3), (0, -4, 1, -5), (0, -4, 0, -5), (0, -6, 0, -7))


def _na_window_row(j):
    return jnp.clip(NA_QROWS * j - NA_ROWS // 2, 0, GRID_ROWS - NA_ROWS)


NA_SUB = 2
NA_BLOCKS = GRID_ROWS // NA_QROWS


def _natten_kernel(q_ref, k_ref, vt_ref, bias0_ref, bias1_ref, sz_ref, o_ref, s_scr):
    step = pl.program_id(1)
    bias_refs = (bias0_ref, bias1_ref)
    k0 = [pl.multiple_of(_na_window_row(NA_SUB * step + s) * GRID_W, LANES) for s in range(NA_SUB)]
    n_pair = H_A // 2

    def scores(i):
        s, p = divmod(i, n_pair)
        cols = slice(p * LANES, (p + 1) * LANES)
        q2 = _pair_q(q_ref[0, s * NA_Q:(s + 1) * NA_Q, cols].astype(f32), 0, 1)
        s_loc = _dot_nt(k_ref[0, pl.ds(k0[s], NA_KEYS), cols], q2) + bias_refs[s][0, p]
        s_ctx = _dot_nt(k_ref[0, SEQ:S_ALL, cols], q2)
        s_scr[i % 2, 0:NA_KEYS, :] = s_loc
        s_scr[i % 2, NA_KEYS:, :] = s_ctx
        return jnp.maximum(_col_max(s_loc), _col_max(s_ctx))

    def values(i, m):
        s, p = divmod(i, n_pair)
        cols = slice(p * LANES, (p + 1) * LANES)
        rows = slice(s * NA_Q, (s + 1) * NA_Q)
        acc = (_dot(vt_ref[p, :, pl.ds(k0[s], NA_KEYS)], _prob(s_scr[i % 2, 0:NA_KEYS, :], m))
               + _dot(vt_ref[p, :, SEQ:S_ALL], _prob(s_scr[i % 2, NA_KEYS:, :], m)))
        o = _pair_out(acc, acc[LANES:LANES + 1], NA_Q, 0, 1)
        o_ref[0, rows, cols] = (o * sz_ref[0, rows, cols]).astype(bf16)

    _pipelined(NA_SUB * n_pair, scores, values)


def _natten(qk, vt, bias, sz):
    def variant(s):
        def index(b, step):
            j = NA_SUB * step + s
            v = jnp.where(j <= 1, j, jnp.where(j >= NA_BLOCKS - 2, j - (NA_BLOCKS - 5), 2))
            return (v, 0, 0, 0)
        return index

    tq = NA_SUB * NA_Q
    bias_block = (1, H_A // 2, NA_KEYS, 2 * NA_Q)
    return pl.pallas_call(
        _natten_kernel,
        grid=(BATCH, NA_BLOCKS // NA_SUB),
        in_specs=[
            pl.BlockSpec((1, tq, D_A), lambda b, j: (b, j, 0)),
            pl.BlockSpec((1, S_ALL, D_A), lambda b, j: (b, 0, 1)),
            pl.BlockSpec((H_A // 2, VT_ROWS, S_ALL), lambda b, j: (0, 0, b)),
            pl.BlockSpec(bias_block, variant(0)),
            pl.BlockSpec(bias_block, variant(1)),
            pl.BlockSpec((1, tq, D_A), lambda b, j: (b, j, 0)),
        ],
        out_specs=pl.BlockSpec((1, tq, D_A), lambda b, j: (b, j, 0)),
        out_shape=jax.ShapeDtypeStruct((BATCH, SEQ, D_A), bf16),
        scratch_shapes=[pltpu.VMEM((2, NA_KEYS + CTX_LEN, 2 * NA_Q), f32)],
        compiler_params=_cparams("arbitrary", "arbitrary"),
        name="natten",
    )(qk, qk, vt, bias, bias, sz)


def _natten_bias(rpb):
    n_dc = 2 * NA_COLS - 1
    c = np.arange(GRID_W)
    kc = np.arange(GRID_W)
    qstart = np.clip(c - NA_COLS // 2, 0, GRID_W - NA_COLS)
    col_ok = (kc[:, None] >= qstart[None, :]) & (kc[:, None] < qstart[None, :] + NA_COLS)
    dc = np.clip(kc[:, None] - c[None, :], -(NA_COLS - 1), NA_COLS - 1) + NA_COLS - 1
    onehot = jnp.asarray((dc[None] == np.arange(n_dc)[:, None, None]).astype(np.float32))
    exp = jnp.einsum("hrd,dkc->hrkc", rpb.astype(f32) * LOG2E, onehot, precision=lax.Precision.HIGHEST)
    pad = NA_KROWS + NA_ROWS - 1 - exp.shape[1]
    exp = jnp.pad(exp, ((0, 0), (0, pad), (0, 0), (0, 0)))
    kr = np.arange(NA_KROWS)
    tables = []
    for lo0, off0, lo1, off1 in NA_VARIANTS:
        per_row = []
        for lo, off in ((lo0, off0), (lo1, off1)):
            d0 = off + NA_ROWS - 1
            t = exp[:, d0:d0 + NA_KROWS]
            ok = ((kr >= lo) & (kr < lo + NA_ROWS))[:, None, None] & col_ok[None]
            per_row.append(jnp.where(jnp.asarray(ok)[None], t, NEG_INF))
        t = jnp.stack(per_row, axis=3)
        t = t.reshape(H_A // 2, 2, NA_KROWS, GRID_W, NA_QROWS, GRID_W)
        t = t.transpose(0, 2, 3, 1, 4, 5)
        tables.append(t.reshape(H_A // 2, NA_KEYS, 2 * NA_Q))
    return jnp.stack(tables, axis=0)


SW_Q = 128
SW_SPAN = SW_Q + 2 * SW_WINDOW
G_B = H_B // HKV_B
N_CHUNK_B = H_B // 2


def _gqa_halves(c):
    return ((2 * c) // G_B) % 2, ((2 * c + 1) // G_B) % 2


SW_SUB = 4
SW_BLOCKS = SEQ // SW_Q


def _swa_kernel(q_ref, k_ref, vt_ref, mask_ref, sink_ref, sz_ref, o_ref, s_scr):
    step = pl.program_id(1)
    blocks = [SW_SUB * step + s for s in range(SW_SUB)]
    start = [pl.multiple_of(jnp.clip((n - 1) * SW_Q, 0, SEQ - SW_SPAN), SW_Q) for n in blocks]
    variant = [jnp.where(n == 0, 0, jnp.where(n == SW_BLOCKS - 1, 2, 1)) for n in blocks]

    def scores(i):
        s, c = divmod(i, N_CHUNK_B)
        kp = c // G_B
        cols = slice(c * LANES, (c + 1) * LANES)
        kcols = slice(kp * LANES, (kp + 1) * LANES)
        q2 = _pair_q(q_ref[0, s * SW_Q:(s + 1) * SW_Q, cols].astype(f32), *_gqa_halves(c))
        s_loc = _dot_nt(k_ref[0, pl.ds(start[s], SW_SPAN), kcols], q2) + mask_ref[variant[s]]
        s_ctx = _dot_nt(k_ref[0, SEQ:S_ALL, kcols], q2)
        s_scr[i % 2, 0:SW_SPAN, :] = s_loc
        s_scr[i % 2, SW_SPAN:, :] = s_ctx
        return jnp.maximum(jnp.maximum(_col_max(s_loc), _col_max(s_ctx)), sink_ref[c])

    def values(i, m):
        s, c = divmod(i, N_CHUNK_B)
        kp = c // G_B
        cols = slice(c * LANES, (c + 1) * LANES)
        rows = slice(s * SW_Q, (s + 1) * SW_Q)
        acc = (_dot(vt_ref[kp, :, pl.ds(start[s], SW_SPAN)], _prob(s_scr[i % 2, 0:SW_SPAN, :], m))
               + _dot(vt_ref[kp, :, SEQ:S_ALL], _prob(s_scr[i % 2, SW_SPAN:, :], m)))
        l = acc[LANES:LANES + 1] + jnp.exp2(sink_ref[c] - m)
        o = _pair_out(acc, l, SW_Q, *_gqa_halves(c))
        o_ref[0, rows, cols] = (o * sz_ref[0, rows, cols]).astype(bf16)

    _pipelined(SW_SUB * N_CHUNK_B, scores, values)


def _swa_mask():
    i = np.arange(SW_Q)
    j = np.arange(SW_SPAN)
    out = []
    for d0 in (0, SW_Q, 2 * SW_Q):
        ok = np.abs(d0 + i[None, :] - j[:, None]) <= SW_WINDOW
        m = np.where(ok, 0.0, NEG_INF).astype(np.float32)
        out.append(np.concatenate([m, m], axis=1))
    return jnp.asarray(np.stack(out))


def _sink_cols(sink, n):
    return jnp.repeat(sink.astype(f32).reshape(N_CHUNK_B, 2) * LOG2E, n, axis=1)[:, None, :]


def _swa(qk, vt, sink, sz):
    tq = SW_SUB * SW_Q
    return pl.pallas_call(
        _swa_kernel,
        grid=(BATCH, SW_BLOCKS // SW_SUB),
        in_specs=[
            pl.BlockSpec((1, tq, D_B), lambda b, n: (b, n, 0)),
            pl.BlockSpec((1, S_ALL, KV_B), lambda b, n: (b, 0, D_B // KV_B)),
            pl.BlockSpec((HKV_B // 2, VT_ROWS, S_ALL), lambda b, n: (H_A // HKV_B, 0, b)),
            pl.BlockSpec((3, SW_SPAN, 2 * SW_Q), lambda b, n: (0, 0, 0)),
            pl.BlockSpec((N_CHUNK_B, 1, 2 * SW_Q), lambda b, n: (0, 0, 0)),
            pl.BlockSpec((1, tq, D_B), lambda b, n: (b, n, 0)),
        ],
        out_specs=pl.BlockSpec((1, tq, D_B), lambda b, n: (b, n, 0)),
        out_shape=jax.ShapeDtypeStruct((BATCH, SEQ, D_B), bf16),
        scratch_shapes=[pltpu.VMEM((2, SW_SPAN + CTX_LEN, 2 * SW_Q), f32)],
        compiler_params=_cparams("arbitrary", "arbitrary"),
        name="swa",
    )(qk, qk, vt, _swa_mask(), _sink_cols(sink, SW_Q), sz)


MLA_TQ = 1024
MLA_SUB = 256
MLA_KCHUNK = 256
MLA_SCALE = float(QK_C) ** -0.5


def _mla_kernel(q_ref, k_ref, vt_ref, sz_ref, o_ref, s_scr):
    chunks = [slice(c * MLA_KCHUNK, (c + 1) * MLA_KCHUNK) for c in range(S_ALL // MLA_KCHUNK)]

    def scores(j):
        q = q_ref[0, j * MLA_SUB:(j + 1) * MLA_SUB, :]
        m = None
        for ck in chunks:
            st = _dot_nt(k_ref[0, ck, :], q)
            s_scr[j % 2, ck, :] = st
            mc = _col_max(st)
            m = mc if m is None else jnp.maximum(m, mc)
        return m

    def values(j, m):
        acc = None
        for ck in chunks:
            d = _dot(vt_ref[0, :, ck], _prob(s_scr[j % 2, ck, :], m))
            acc = d if acc is None else acc + d
        o = acc[:V_C] / acc[V_C:V_C + 1]
        rows = slice(j * MLA_SUB, (j + 1) * MLA_SUB)
        o_ref[0, rows, :] = (o.T * sz_ref[0, rows, :]).astype(bf16)

    _pipelined(MLA_TQ // MLA_SUB, scores, values)


def _mla(qc, kc, vt, sz):
    return pl.pallas_call(
        _mla_kernel,
        grid=(BATCH, H_C, SEQ // MLA_TQ),
        in_specs=[
            pl.BlockSpec((1, MLA_TQ, QC_PAD), lambda b, h, i: (b, i, h)),
            pl.BlockSpec((1, S_ALL, QC_PAD), lambda b, h, i: (b, 0, h)),
            pl.BlockSpec((1, VT_ROWS, S_ALL), lambda b, h, i: (h, 0, b)),
            pl.BlockSpec((1, MLA_TQ, V_C), lambda b, h, i: (b, i, h)),
        ],
        out_specs=pl.BlockSpec((1, MLA_TQ, V_C), lambda b, h, i: (b, i, h)),
        out_shape=jax.ShapeDtypeStruct((BATCH, SEQ, D_C), bf16),
        scratch_shapes=[pltpu.VMEM((2, S_ALL, MLA_SUB), f32)],
        compiler_params=_cparams("arbitrary", "arbitrary", "arbitrary"),
        name="mla",
    )(qc, kc, vt, sz)


def _ctx_kernel(qa_ref, ka_ref, qb_ref, kb_ref, vtab_ref, qc_ref, kc_ref, vtc_ref,
                sink_ref, sza_ref, szb_ref, szc_ref, oa_ref, ob_ref, oc_ref):
    n = CTX_LEN
    for p in range(H_A // 2):
        cols = slice(p * LANES, (p + 1) * LANES)
        s = _dot_nt(ka_ref[0, :, cols], _pair_q(qa_ref[0, :, cols].astype(f32), 0, 1))
        acc = _dot(vtab_ref[p], _prob(s, _col_max(s)))
        o = _pair_out(acc, acc[LANES:LANES + 1], n, 0, 1)
        oa_ref[0, :, cols] = (o * sza_ref[0, :, cols]).astype(bf16)
    for c in range(N_CHUNK_B):
        kp = c // G_B
        cols = slice(c * LANES, (c + 1) * LANES)
        s = _dot_nt(kb_ref[0, :, kp * LANES:(kp + 1) * LANES],
                    _pair_q(qb_ref[0, :, cols].astype(f32), *_gqa_halves(c)))
        m = jnp.maximum(_col_max(s), sink_ref[c])
        acc = _dot(vtab_ref[H_A // 2 + kp], _prob(s, m))
        l = acc[LANES:LANES + 1] + jnp.exp2(sink_ref[c] - m)
        o = _pair_out(acc, l, n, *_gqa_halves(c))
        ob_ref[0, :, cols] = (o * szb_ref[0, :, cols]).astype(bf16)
    for h in range(H_C):
        qcols = slice(h * QC_PAD, (h + 1) * QC_PAD)
        vcols = slice(h * V_C, (h + 1) * V_C)
        s = _dot_nt(kc_ref[0, :, qcols], qc_ref[0, :, qcols])
        acc = _dot(vtc_ref[h], _prob(s, _col_max(s)))
        o = acc[:V_C] / acc[V_C:V_C + 1]
        oc_ref[0, :, vcols] = (o.T * szc_ref[0, :, vcols]).astype(bf16)


def _ctx_attention(qka, qkb, vt_ab, qc, kc, vt_c, sink, sza, szb, szc):
    cb = SEQ // CTX_LEN

    def spec(width, col):
        return pl.BlockSpec((1, CTX_LEN, width), lambda b: (b, cb, col))

    def vt_spec(tiles):
        return pl.BlockSpec((tiles, VT_ROWS, CTX_LEN), lambda b: (0, 0, b * (S_ALL // CTX_LEN) + cb))

    return pl.pallas_call(
        _ctx_kernel,
        grid=(BATCH,),
        in_specs=[
            spec(D_A, 0), spec(D_A, 1),
            spec(D_B, 0), spec(KV_B, D_B // KV_B),
            vt_spec((H_A + HKV_B) // 2),
            spec(H_C * QC_PAD, 0), spec(H_C * QC_PAD, 0),
            vt_spec(H_C),
            pl.BlockSpec((N_CHUNK_B, 1, 2 * CTX_LEN), lambda b: (0, 0, 0)),
            spec(D_A, 0), spec(D_B, 0), spec(D_C, 0),
        ],
        out_specs=[
            pl.BlockSpec((1, CTX_LEN, D_A), lambda b: (b, 0, 0)),
            pl.BlockSpec((1, CTX_LEN, D_B), lambda b: (b, 0, 0)),
            pl.BlockSpec((1, CTX_LEN, D_C), lambda b: (b, 0, 0)),
        ],
        out_shape=[
            jax.ShapeDtypeStruct((BATCH, CTX_LEN, D_A), bf16),
            jax.ShapeDtypeStruct((BATCH, CTX_LEN, D_B), bf16),
            jax.ShapeDtypeStruct((BATCH, CTX_LEN, D_C), bf16),
        ],
        compiler_params=_cparams("arbitrary"),
        name="ctx_attention",
    )(qka, qka, qkb, qkb, vt_ab, qc, kc, vt_c, _sink_cols(sink, CTX_LEN), sza, szb, szc)


OUT_TM = 512
OUT_TN = 2048


def _out_kernel(x_ref, ga_ref, gb_ref, gc_ref, w_ref, mod_ref, o_ref):
    g = jnp.concatenate([ga_ref[0], gb_ref[0], gc_ref[0]], axis=1)
    o_ref[0] = x_ref[0] + mod_ref[0] * _dot(g, w_ref[...])


def _out_proj(x, ga, gb, gc, w_out_bf, mod3, is_ctx):
    t = x.shape[1]
    tm = min(t, OUT_TM)
    tn = OUT_TN
    gate_col0 = 2 * D_MODEL // tn
    mod_row = (lambda b: BATCH) if is_ctx else (lambda b: b)
    return pl.pallas_call(
        _out_kernel,
        grid=(BATCH, t // tm, D_MODEL // tn),
        in_specs=[
            pl.BlockSpec((1, tm, tn), lambda b, i, j: (b, i, j)),
            pl.BlockSpec((1, tm, D_A), lambda b, i, j: (b, i, 0)),
            pl.BlockSpec((1, tm, D_B), lambda b, i, j: (b, i, 0)),
            pl.BlockSpec((1, tm, D_C), lambda b, i, j: (b, i, 0)),
            pl.BlockSpec((D_MIX, tn), lambda b, i, j: (0, j)),
            pl.BlockSpec((1, 1, tn), lambda b, i, j: (mod_row(b), 0, gate_col0 + j)),
        ],
        out_specs=pl.BlockSpec((1, tm, tn), lambda b, i, j: (b, i, j)),
        out_shape=jax.ShapeDtypeStruct(x.shape, f32),
        compiler_params=_cparams("arbitrary", "arbitrary", "arbitrary"),
        name="out_proj_ctx" if is_ctx else "out_proj",
    )(x, ga, gb, gc, w_out_bf, mod3)


def _rope_tables():
    t = jnp.arange(SEQ)
    row = (t // GRID_W).astype(f32)
    col = (t % GRID_W).astype(f32)
    n_freq = ROPE_DIM // 4
    inv = ROPE_BASE ** (-jnp.arange(n_freq, dtype=f32) / n_freq)
    ar = row[:, None] * inv
    ac = col[:, None] * inv
    ang = jnp.concatenate([ar, ar, ac, ac], axis=-1)
    cos = jnp.cos(ang).astype(f32)
    sin = jnp.sin(ang).astype(f32)
    sign = jnp.asarray(np.where((np.arange(ROPE_DIM) % 32) < 16, -1.0, 1.0), dtype=f32)
    sin = sin * sign
    cos = jnp.concatenate([cos, jnp.ones((CTX_LEN, ROPE_DIM), f32)], axis=0)
    sin = jnp.concatenate([sin, jnp.zeros((CTX_LEN, ROPE_DIM), f32)], axis=0)
    pair = (jnp.concatenate([cos, cos], axis=1), jnp.concatenate([sin, sin], axis=1))
    single = (jnp.concatenate([cos, jnp.ones_like(cos)], axis=1), jnp.concatenate([sin, jnp.zeros_like(sin)], axis=1))
    return pair, single


def _pad_heads(w, width, padded):
    lead = w.shape[:-1]
    w = w.reshape(lead + (H_C, width))
    w = jnp.pad(w, [(0, 0)] * len(lead) + [(0, 0), (0, padded - width)])
    return w.reshape(lead + (H_C * padded,))


def kernel(x, c, ctx, c_ctx, norm_w, w_ada, b_ada, w_in, qn_a, kn_a, rpb_a, qn_b, kn_b, sink_b,
           qa_norm, kva_norm, w_qb, w_kvb, qn_c, kn_c, w_out):
    (cos2, sin2), (cos1, sin1) = _rope_tables()
    cvec = jnp.concatenate([c, c_ctx[None, :], jnp.zeros((MOD_ROWS - BATCH - 1, D_MODEL), f32)], axis=0)
    mod = _modulation(cvec, w_ada, b_ada)
    q_scale_ab = HD ** -0.5 * LOG2E
    q_scale_c = MLA_SCALE * LOG2E
    w_out_bf = w_out.astype(bf16)

    o_qa, o_ka, o_va, o_qb, o_kb, o_vb, o_cq, o_ckv, o_kpe, o_z = np.cumsum(
        (0, D_A, D_A, D_A, D_B, KV_B, KV_B, Q_LORA, KV_LORA, ROPE_DIM)).tolist()

    xc = ctx
    for l in range(DEPTH):
        last = l == DEPTH - 1
        mod3 = mod[l].reshape(MOD_ROWS, 1, 3 * D_MODEL)
        h = _hnorm(x, xc, norm_w[l], mod3).reshape(R_ALL, D_MODEL)
        wl = w_in[l]

        gain_hn = jnp.concatenate([jnp.tile(qn_a[l] * q_scale_ab, H_A), jnp.tile(kn_a[l], H_A),
                                   jnp.tile(qn_b[l] * q_scale_ab, H_B), jnp.tile(kn_b[l], HKV_B)])
        w_hn = jnp.concatenate([wl[:, o_qa:o_va], wl[:, o_qb:o_vb]], axis=1).astype(bf16)
        w_lora = jnp.pad(wl[:, o_cq:o_z], ((0, 0), (0, LANES - ROPE_DIM))).astype(bf16)
        wt_v = jnp.concatenate([wl[:, o_va:o_qb], wl[:, o_vb:o_cq]], axis=1).T.astype(bf16)
        qka, qkb, cqn, ckvn, kpe, vt_ab = _proj_main(h, w_hn, w_lora, wt_v, gain_hn, qa_norm[l], kva_norm[l],
                                                      cos2, sin2)
        sza, szb, szc = _proj_gate(h, wl[:, o_z:].astype(bf16))

        wq = _pad_heads(w_qb[l], QK_C, QC_PAD).astype(bf16)
        wkv = w_kvb[l].reshape(KV_LORA, H_C, NOPE_C + V_C)
        wkn = wkv[:, :, :NOPE_C].reshape(KV_LORA, H_C * NOPE_C).astype(bf16)
        wvt = wkv[:, :, NOPE_C:].reshape(KV_LORA, H_C * V_C).T.astype(bf16)
        gq = _pad_heads(jnp.tile(qn_c[l] * q_scale_c, H_C), QK_C, QC_PAD).reshape(1, H_C * QC_PAD)
        gk = jnp.pad(kn_c[l], (0, QC_PAD - QK_C)).reshape(1, QC_PAD)
        qc, kc, vt_c = _mla_up(cqn, ckvn, kpe, wq, wkn, wvt, gq, gk, cos1, sin1)

        def b3(a):
            return a.reshape(BATCH, S_ALL, a.shape[-1])

        qka, qkb, qc, kc, sza, szb, szc = map(b3, (qka, qkb, qc, kc, sza, szb, szc))

        ga = _natten(qka, vt_ab, _natten_bias(rpb_a[l]), sza)
        gb = _swa(qkb, vt_ab, sink_b[l], szb)
        gc = _mla(qc, kc, vt_c, szc)
        if not last:
            ga_c, gb_c, gc_c = _ctx_attention(qka, qkb, vt_ab, qc, kc, vt_c, sink_b[l], sza, szb, szc)
            xc = _out_proj(xc, ga_c, gb_c, gc_c, w_out_bf[l], mod3, True)
        x = _out_proj(x, ga, gb, gc, w_out_bf[l], mod3, False)
    return x
```

```python
import functools
import math

import numpy as np
import jax
import jax.numpy as jnp
from jax import lax
from jax.experimental import pallas as pl
from jax.experimental.pallas import tpu as pltpu

D_MODEL = 2048
BATCH = 8
SEQ = 2048
DEPTH = 2
GRID_W = 64
CTX_LEN = 256
HD = 64
H_A = 8
D_A = H_A * HD
H_B = 12
HKV_B = 4
D_B = H_B * HD
KV_B = HKV_B * HD
H_C = 6
NOPE_C = 128
ROPE_DIM = 64
QK_C = NOPE_C + ROPE_DIM
V_C = 128
D_C = H_C * V_C
Q_LORA = 768
KV_LORA = 512
D_MIX = D_A + D_B + D_C
NA_ROWS = 8
NA_COLS = 16
SW_WINDOW = 128
ROPE_BASE = 10000.0
EPS = 1e-6
NEG_INF = -1e30

S_ALL = SEQ + CTX_LEN
R_ALL = BATCH * S_ALL
GRID_ROWS = SEQ // GRID_W
LANES = 128
MXU_DIM = 256
BF16_ROWS = 16
QC_PAD = 256
VT_ROWS = LANES + BF16_ROWS
MOD_ROWS = 16
VMEM_LIMIT = 48 * 1024 * 1024
LOG2E = math.log2(math.e)

bf16 = jnp.bfloat16
f32 = jnp.float32


def _cparams(*sem):
    return pltpu.CompilerParams(dimension_semantics=sem, vmem_limit_bytes=VMEM_LIMIT)


def _silu(v):
    return v / (1.0 + jnp.exp(-v))


def _lane_iota(shape):
    return lax.broadcasted_iota(jnp.int32, shape, len(shape) - 1)


def _rope_rotate(xn, cos, sin_signed):
    lane = _lane_iota(xn.shape)
    take_next = (lane % 32) < 16
    rot = jnp.where(take_next, pltpu.roll(xn, LANES - 16, 1), pltpu.roll(xn, 16, 1))
    return xn * cos + rot * sin_signed


def _dot_nt(a, b):
    return lax.dot_general(a, b, (((1,), (1,)), ((), ())), preferred_element_type=f32)


def _dot(a, b):
    return jnp.dot(a, b, preferred_element_type=f32)


def _mod_kernel(c_ref, w_ref, b_ref, o_ref):
    sc = _silu(c_ref[...]).astype(bf16)
    o_ref[0] = _dot(sc, w_ref[0].astype(bf16)) + b_ref[0]


def _modulation(cvec, w_ada, b_ada):
    tn = 1024
    n = 3 * D_MODEL
    return pl.pallas_call(
        _mod_kernel,
        grid=(DEPTH, n // tn),
        in_specs=[
            pl.BlockSpec((MOD_ROWS, D_MODEL), lambda l, j: (0, 0)),
            pl.BlockSpec((1, D_MODEL, tn), lambda l, j: (l, 0, j)),
            pl.BlockSpec((1, 1, tn), lambda l, j: (l, 0, j)),
        ],
        out_specs=pl.BlockSpec((1, MOD_ROWS, tn), lambda l, j: (l, 0, j)),
        out_shape=jax.ShapeDtypeStruct((DEPTH, MOD_ROWS, n), f32),
        compiler_params=_cparams("arbitrary", "arbitrary"),
        name="modulation",
    )(cvec, w_ada, b_ada.reshape(DEPTH, 1, n))


H_ROWS = 256
N_LAT_BLOCKS = SEQ // H_ROWS


def _hnorm_kernel(x_ref, ctx_ref, nw_ref, mod_ref, h_ref):
    s = pl.program_id(1)

    def body(src):
        x = src[0]
        ms = jnp.mean(x * x, axis=-1, keepdims=True)
        y = x * lax.rsqrt(ms + EPS) * nw_ref[...]
        shift = mod_ref[0, :, 0:D_MODEL]
        scale = mod_ref[0, :, D_MODEL:2 * D_MODEL]
        h_ref[0] = (y * (1.0 + scale) + shift).astype(bf16)

    @pl.when(s < N_LAT_BLOCKS)
    def _():
        body(x_ref)

    @pl.when(s >= N_LAT_BLOCKS)
    def _():
        body(ctx_ref)


def _hnorm(x, xc, norm_w, mod3):
    return pl.pallas_call(
        _hnorm_kernel,
        grid=(BATCH, S_ALL // H_ROWS),
        in_specs=[
            pl.BlockSpec((1, H_ROWS, D_MODEL), lambda b, s: (b, jnp.minimum(s, N_LAT_BLOCKS - 1), 0)),
            pl.BlockSpec((1, H_ROWS, D_MODEL), lambda b, s: (b, 0, 0)),
            pl.BlockSpec((1, D_MODEL), lambda b, s: (0, 0)),
            pl.BlockSpec((1, 1, 3 * D_MODEL), lambda b, s: (jnp.where(s < N_LAT_BLOCKS, b, BATCH), 0, 0)),
        ],
        out_specs=pl.BlockSpec((1, H_ROWS, D_MODEL), lambda b, s: (b, s, 0)),
        out_shape=jax.ShapeDtypeStruct((BATCH, S_ALL, D_MODEL), bf16),
        compiler_params=_cparams("arbitrary", "arbitrary"),
        name="hnorm",
    )(x, xc, norm_w.reshape(1, D_MODEL), mod3)


PROJ_TM = 768


def _group_mean_sq(u, g_ref):
    sq = u * u
    hi = sq.astype(bf16)
    lo = (sq - hi.astype(f32)).astype(bf16)
    g = g_ref[...]
    return _dot(hi, g) + _dot(lo, g)


def _group_matrix():
    idx = np.arange(MXU_DIM) // HD
    return jnp.asarray((idx[:, None] == idx[None, :]).astype(np.float32) / HD, dtype=bf16)


def _run_items(items):
    u = items[0][0]()
    for i, (_, epilogue) in enumerate(items):
        u_next = items[i + 1][0]() if i + 1 < len(items) else None
        epilogue(u)
        u = u_next


def _store_vt_tile(vt_ref, p, ut):
    vt_ref[p, 0:LANES, :] = ut.astype(bf16)
    vt_ref[p, LANES:VT_ROWS, :] = jnp.ones((VT_ROWS - LANES, ut.shape[1]), bf16)


PROJ_TN = 512
N_HN = 2 * D_A + D_B + KV_B
N_LORA = Q_LORA + KV_LORA + LANES
N_VT = D_A + KV_B


def _proj_main_kernel(h_ref, whn_ref, wl_ref, wvt_ref, ghn_ref, g_ref, cos_ref, sin_ref, gcq_ref, gckv_ref,
                      qka_ref, qkb_ref, cqn_ref, ckvn_ref, kpe_ref, vt_ref):
    def hn_item(k):
        cols = slice(k * PROJ_TN, (k + 1) * PROJ_TN)
        rope = k * PROJ_TN >= 2 * D_A
        o_ref = qkb_ref if rope else qka_ref
        o0 = k * PROJ_TN - (2 * D_A if rope else 0)

        def epilogue(u):
            for c0 in range(0, PROJ_TN, MXU_DIM):
                uc = u[:, c0:c0 + MXU_DIM]
                ms = _group_mean_sq(uc, g_ref)
                xn = uc * lax.rsqrt(ms + EPS) * ghn_ref[:, k * PROJ_TN + c0:k * PROJ_TN + c0 + MXU_DIM]
                if rope:
                    for c1 in range(0, MXU_DIM, LANES):
                        o_ref[:, o0 + c0 + c1:o0 + c0 + c1 + LANES] = _rope_rotate(
                            xn[:, c1:c1 + LANES], cos_ref[...], sin_ref[...]).astype(bf16)
                else:
                    o_ref[:, o0 + c0:o0 + c0 + MXU_DIM] = xn.astype(bf16)

        return lambda: _dot(h_ref[...], whn_ref[:, cols]), epilogue

    def rowrms_item(c0, width, gain_ref, o_ref):
        def epilogue(u):
            ms = jnp.mean(u * u, axis=-1, keepdims=True)
            o_ref[...] = (u * lax.rsqrt(ms + EPS) * gain_ref[...]).astype(bf16)

        return lambda: _dot(h_ref[...], wl_ref[:, c0:c0 + width]), epilogue

    def kpe_item():
        def epilogue(u):
            kpe_ref[...] = u

        return lambda: _dot(h_ref[...], wl_ref[:, Q_LORA + KV_LORA:N_LORA]), epilogue

    def vt_item(p):
        def epilogue(ut):
            _store_vt_tile(vt_ref, p, ut)

        return lambda: _dot_nt(wvt_ref[p * LANES:(p + 1) * LANES, :], h_ref[...]), epilogue

    _run_items([hn_item(k) for k in range(N_HN // PROJ_TN)]
               + [rowrms_item(0, Q_LORA, gcq_ref, cqn_ref), rowrms_item(Q_LORA, KV_LORA, gckv_ref, ckvn_ref),
                  kpe_item()]
               + [vt_item(p) for p in range(N_VT // LANES)])


def _resident(shape):
    return pl.BlockSpec(shape, lambda i: (0,) * len(shape), pipeline_mode=pl.Buffered(1))


def _proj_main(h2d, whn, wl, wvt, ghn, gcq, gckv, cos, sin):
    tm = PROJ_TM
    nt = S_ALL // tm
    row = lambda i: (i, 0)
    tiles = N_VT // LANES
    return pl.pallas_call(
        _proj_main_kernel,
        grid=(R_ALL // tm,),
        in_specs=[
            pl.BlockSpec((tm, D_MODEL), row),
            _resident((D_MODEL, N_HN)), _resident((D_MODEL, N_LORA)), _resident((N_VT, D_MODEL)),
            _resident((1, N_HN)), _resident((MXU_DIM, MXU_DIM)),
            pl.BlockSpec((tm, LANES), lambda i: (i % nt, 0)),
            pl.BlockSpec((tm, LANES), lambda i: (i % nt, 0)),
            _resident((1, Q_LORA)), _resident((1, KV_LORA)),
        ],
        out_specs=[
            pl.BlockSpec((tm, 2 * D_A), row),
            pl.BlockSpec((tm, D_B + KV_B), row),
            pl.BlockSpec((tm, Q_LORA), row),
            pl.BlockSpec((tm, KV_LORA), row),
            pl.BlockSpec((tm, LANES), row),
            pl.BlockSpec((tiles, VT_ROWS, tm), lambda i: (0, 0, i)),
        ],
        out_shape=[
            jax.ShapeDtypeStruct((R_ALL, 2 * D_A), bf16),
            jax.ShapeDtypeStruct((R_ALL, D_B + KV_B), bf16),
            jax.ShapeDtypeStruct((R_ALL, Q_LORA), bf16),
            jax.ShapeDtypeStruct((R_ALL, KV_LORA), bf16),
            jax.ShapeDtypeStruct((R_ALL, LANES), f32),
            jax.ShapeDtypeStruct((tiles, VT_ROWS, R_ALL), bf16),
        ],
        compiler_params=_cparams("arbitrary"),
        name="proj_main",
    )(h2d, whn, wl, wvt, ghn.reshape(1, N_HN), _group_matrix(), cos, sin,
      gcq.reshape(1, Q_LORA), gckv.reshape(1, KV_LORA))


def _proj_gate_kernel(h_ref, wz_ref, sza_ref, szb_ref, szc_ref):
    outs = ((sza_ref, 0, D_A), (szb_ref, D_A, D_B), (szc_ref, D_A + D_B, D_C))

    def item(k):
        c0 = k * PROJ_TN

        def epilogue(u):
            sz = _silu(u)
            for o_ref, start, width in outs:
                lo, hi = max(c0, start), min(c0 + PROJ_TN, start + width)
                if lo < hi:
                    o_ref[:, lo - start:hi - start] = sz[:, lo - c0:hi - c0]

        return lambda: _dot(h_ref[...], wz_ref[:, c0:c0 + PROJ_TN]), epilogue

    _run_items([item(k) for k in range(D_MIX // PROJ_TN)])


def _proj_gate(h2d, wz):
    tm = PROJ_TM
    row = lambda i: (i, 0)
    return pl.pallas_call(
        _proj_gate_kernel,
        grid=(R_ALL // tm,),
        in_specs=[pl.BlockSpec((tm, D_MODEL), row), _resident((D_MODEL, D_MIX))],
        out_specs=[pl.BlockSpec((tm, D_A), row), pl.BlockSpec((tm, D_B), row), pl.BlockSpec((tm, D_C), row)],
        out_shape=[jax.ShapeDtypeStruct((R_ALL, D_A), f32), jax.ShapeDtypeStruct((R_ALL, D_B), f32),
                   jax.ShapeDtypeStruct((R_ALL, D_C), f32)],
        compiler_params=_cparams("arbitrary"),
        name="proj_gate",
    )(h2d, wz)


def _mla_up_kernel(cq_ref, ckv_ref, kpe_ref, wq_ref, wkn_ref, wvt_ref, gq_ref, gk_ref,
                   cos_ref, sin_ref, qc_ref, kc_ref, vt_ref):
    cos = cos_ref[...]
    sin = sin_ref[...]
    kpe = kpe_ref[...]
    ss_pe = jnp.sum(kpe * kpe, axis=-1, keepdims=True)
    k_rot = _rope_rotate(kpe * gk_ref[:, LANES:], cos, sin)

    def q_head(h):
        cols = slice(h * QC_PAD, (h + 1) * QC_PAD)
        return lambda: _dot(cq_ref[...], wq_ref[:, cols]), functools.partial(q_epilogue, h)

    def q_epilogue(h, u):
        ms = jnp.sum(u * u, axis=-1, keepdims=True) * (1.0 / QK_C)
        xn = u * lax.rsqrt(ms + EPS) * gq_ref[:, h * QC_PAD:(h + 1) * QC_PAD]
        qc_ref[:, h * QC_PAD:h * QC_PAD + LANES] = xn[:, :LANES].astype(bf16)
        qc_ref[:, h * QC_PAD + LANES:(h + 1) * QC_PAD] = _rope_rotate(xn[:, LANES:], cos, sin).astype(bf16)

    def k_head(h):
        cols = slice(h * NOPE_C, (h + 1) * NOPE_C)
        return lambda: _dot(ckv_ref[...], wkn_ref[:, cols]), functools.partial(k_epilogue, h)

    def k_epilogue(h, u):
        ms = (jnp.sum(u * u, axis=-1, keepdims=True) + ss_pe) * (1.0 / QK_C)
        r = lax.rsqrt(ms + EPS)
        kc_ref[:, h * QC_PAD:h * QC_PAD + LANES] = (u * r * gk_ref[:, 0:LANES]).astype(bf16)
        kc_ref[:, h * QC_PAD + LANES:(h + 1) * QC_PAD] = (k_rot * r).astype(bf16)

    def v_tile(p):
        rows = slice(p * LANES, (p + 1) * LANES)

        def epilogue(ut):
            _store_vt_tile(vt_ref, p, ut)

        return lambda: _dot_nt(wvt_ref[rows, :], ckv_ref[...]), epilogue

    _run_items([q_head(h) for h in range(H_C)] + [k_head(h) for h in range(H_C)] + [v_tile(p) for p in range(H_C)])


def _mla_up(cqn, ckvn, kpe, wq, wkn, wvt, gq, gk, cos, sin):
    tm = PROJ_TM
    nt = S_ALL // tm
    row = lambda i: (i, 0)
    fixed = lambda i: (0, 0)
    return pl.pallas_call(
        _mla_up_kernel,
        grid=(R_ALL // tm,),
        in_specs=[
            pl.BlockSpec((tm, Q_LORA), row),
            pl.BlockSpec((tm, KV_LORA), row),
            pl.BlockSpec((tm, LANES), row),
            pl.BlockSpec((Q_LORA, H_C * QC_PAD), fixed),
            pl.BlockSpec((KV_LORA, H_C * NOPE_C), fixed),
            pl.BlockSpec((H_C * V_C, KV_LORA), fixed),
            pl.BlockSpec((1, H_C * QC_PAD), fixed),
            pl.BlockSpec((1, QC_PAD), fixed),
            pl.BlockSpec((tm, LANES), lambda i: (i % nt, 0)),
            pl.BlockSpec((tm, LANES), lambda i: (i % nt, 0)),
        ],
        out_specs=[
            pl.BlockSpec((tm, H_C * QC_PAD), row),
            pl.BlockSpec((tm, H_C * QC_PAD), row),
            pl.BlockSpec((H_C, VT_ROWS, tm), lambda i: (0, 0, i)),
        ],
        out_shape=[
            jax.ShapeDtypeStruct((R_ALL, H_C * QC_PAD), bf16),
            jax.ShapeDtypeStruct((R_ALL, H_C * QC_PAD), bf16),
            jax.ShapeDtypeStruct((H_C, VT_ROWS, R_ALL), bf16),
        ],
        compiler_params=_cparams("arbitrary"),
        name="mla_up",
    )(cqn, ckvn, kpe, wq, wkn, wvt, gq, gk, cos, sin)


def _pipelined(n, score_fn, value_fn):
    m = score_fn(0)
    for j in range(n):
        m_next = score_fn(j + 1) if j + 1 < n else None
        value_fn(j, m)
        m = m_next


def _col_max(s):
    return jnp.max(s, axis=0, keepdims=True)


def _prob(s, m):
    return jnp.exp2((s - m).astype(bf16))


def _pair_q(qv, half_even, half_odd):
    low = _lane_iota(qv.shape) < HD
    zero = jnp.zeros_like(qv)
    qa = jnp.where(low, qv, zero) if half_even == 0 else jnp.where(low, zero, pltpu.roll(qv, HD, 1))
    qb = jnp.where(low, zero, qv) if half_odd == 1 else jnp.where(low, pltpu.roll(qv, HD, 1), zero)
    return jnp.concatenate([qa, qb], axis=0).astype(bf16)


def _pair_out(acc, l, n, half_even, half_odd):
    o = acc[:LANES] / l
    t = jnp.concatenate([o[half_even * HD:(half_even + 1) * HD, 0:n],
                         o[half_odd * HD:(half_odd + 1) * HD, n:2 * n]], axis=0)
    return t.T


NA_QROWS = 2
NA_Q = NA_QROWS * GRID_W
NA_KROWS = NA_ROWS + NA_QROWS
NA_KEYS = NA_KROWS * GRID_W
NA_VARIANTS = ((0, 0, 0, -1), (0, -2, 0, -3), (0, -4, 1, -5), (0, -4, 0, -5), (0, -6, 0, -7))


def _na_window_row(j):
    return jnp.clip(NA_QROWS * j - NA_ROWS // 2, 0, GRID_ROWS - NA_ROWS)


NA_SUB = 2
NA_BLOCKS = GRID_ROWS // NA_QROWS


def _natten_kernel(q_ref, k_ref, vt_ref, bias0_ref, bias1_ref, sz_ref, o_ref, s_scr):
    step = pl.program_id(1)
    bias_refs = (bias0_ref, bias1_ref)
    k0 = [pl.multiple_of(_na_window_row(NA_SUB * step + s) * GRID_W, LANES) for s in range(NA_SUB)]
    n_pair = H_A // 2

    def scores(i):
        s, p = divmod(i, n_pair)
        cols = slice(p * LANES, (p + 1) * LANES)
        q2 = _pair_q(q_ref[0, s * NA_Q:(s + 1) * NA_Q, cols].astype(f32), 0, 1)
        s_loc = _dot_nt(k_ref[0, pl.ds(k0[s], NA_KEYS), cols], q2) + bias_refs[s][0, 0, p]
        s_ctx = _dot_nt(k_ref[0, SEQ:S_ALL, cols], q2)
        s_scr[i % 2, 0:NA_KEYS, :] = s_loc
        s_scr[i % 2, NA_KEYS:, :] = s_ctx
        return jnp.maximum(_col_max(s_loc), _col_max(s_ctx))

    def values(i, m):
        s, p = divmod(i, n_pair)
        cols = slice(p * LANES, (p + 1) * LANES)
        rows = slice(s * NA_Q, (s + 1) * NA_Q)
        acc = (_dot(vt_ref[p, :, pl.ds(k0[s], NA_KEYS)], _prob(s_scr[i % 2, 0:NA_KEYS, :], m))
               + _dot(vt_ref[p, :, SEQ:S_ALL], _prob(s_scr[i % 2, NA_KEYS:, :], m)))
        o = _pair_out(acc, acc[LANES:LANES + 1], NA_Q, 0, 1)
        o_ref[0, rows, cols] = (o * sz_ref[0, rows, cols]).astype(bf16)

    _pipelined(NA_SUB * n_pair, scores, values)


def _natten(qk, vt, bias, layer, sz):
    def variant(s):
        def index(b, step):
            j = NA_SUB * step + s
            v = jnp.where(j <= 1, j, jnp.where(j >= NA_BLOCKS - 2, j - (NA_BLOCKS - 5), 2))
            return (layer, v, 0, 0, 0)
        return index

    tq = NA_SUB * NA_Q
    bias_block = (1, 1, H_A // 2, NA_KEYS, 2 * NA_Q)
    return pl.pallas_call(
        _natten_kernel,
        grid=(BATCH, NA_BLOCKS // NA_SUB),
        in_specs=[
            pl.BlockSpec((1, tq, D_A), lambda b, j: (b, j, 0)),
            pl.BlockSpec((1, S_ALL, D_A), lambda b, j: (b, 0, 1)),
            pl.BlockSpec((H_A // 2, VT_ROWS, S_ALL), lambda b, j: (0, 0, b)),
            pl.BlockSpec(bias_block, variant(0)),
            pl.BlockSpec(bias_block, variant(1)),
            pl.BlockSpec((1, tq, D_A), lambda b, j: (b, j, 0)),
        ],
        out_specs=pl.BlockSpec((1, tq, D_A), lambda b, j: (b, j, 0)),
        out_shape=jax.ShapeDtypeStruct((BATCH, SEQ, D_A), bf16),
        scratch_shapes=[pltpu.VMEM((2, NA_KEYS + CTX_LEN, 2 * NA_Q), f32)],
        compiler_params=_cparams("arbitrary", "arbitrary"),
        name="natten",
    )(qk, qk, vt, bias, bias, sz)


def _natten_bias(rpb):
    n_dc = 2 * NA_COLS - 1
    n_dr = NA_KROWS + NA_ROWS - 1
    c = np.arange(LANES) % GRID_W
    kc = np.arange(GRID_W)
    qstart = np.clip(c - NA_COLS // 2, 0, GRID_W - NA_COLS)
    col_ok = (kc[:, None] >= qstart[None, :]) & (kc[:, None] < qstart[None, :] + NA_COLS)
    dc = np.clip(kc[:, None] - c[None, :], -(NA_COLS - 1), NA_COLS - 1) + NA_COLS - 1
    onehot = jnp.asarray((dc[None] == np.arange(n_dc)[:, None, None]).astype(np.float32))
    exp = jnp.einsum("lhrd,dkc->lhrkc", rpb.astype(f32) * LOG2E, onehot, precision=lax.Precision.HIGHEST)
    exp = jnp.pad(exp, ((0, 0), (0, 0), (0, n_dr - exp.shape[2]), (0, 0), (0, 0)))
    col_mask = jnp.asarray(np.where(col_ok, 0.0, NEG_INF).astype(np.float32))
    n_var = len(NA_VARIANTS)
    return pl.pallas_call(
        _natten_bias_kernel,
        grid=(DEPTH, n_var, H_A // 2),
        in_specs=[
            pl.BlockSpec((1, 2, n_dr, GRID_W, LANES), lambda l, v, p: (l, p, 0, 0, 0)),
            pl.BlockSpec((GRID_W, LANES), lambda l, v, p: (0, 0)),
        ],
        out_specs=pl.BlockSpec((1, 1, 1, NA_KEYS, 2 * NA_Q), lambda l, v, p: (l, v, p, 0, 0)),
        out_shape=jax.ShapeDtypeStruct((DEPTH, n_var, H_A // 2, NA_KEYS, 2 * NA_Q), f32),
        compiler_params=_cparams("arbitrary", "arbitrary", "arbitrary"),
        name="natten_bias",
    )(exp, col_mask)


def _natten_bias_kernel(exp_ref, mask_ref, o_ref):
    v = pl.program_id(1)

    def pick(column):
        out = jnp.int32(0)
        for i, var in enumerate(NA_VARIANTS):
            out = jnp.where(v == i, var[column], out)
        return out

    lo = (pick(0), pick(2))
    off = (pick(1), pick(3))
    low = _lane_iota((GRID_W, LANES)) < GRID_W
    mask = mask_ref[...]
    for kr in range(NA_KROWS):
        for e in range(2):
            halves = []
            for qr in range(NA_QROWS):
                blk = exp_ref[0, e, pl.ds(kr + off[qr] + NA_ROWS - 1, 1)][0] + mask
                valid = (kr >= lo[qr]) & (kr < lo[qr] + NA_ROWS)
                halves.append(jnp.where(valid, blk, NEG_INF))
            o_ref[0, 0, 0, kr * GRID_W:(kr + 1) * GRID_W, e * NA_Q:(e + 1) * NA_Q] = jnp.where(
                low, halves[0], halves[1])


SW_Q = 128
SW_SPAN = SW_Q + 2 * SW_WINDOW
G_B = H_B // HKV_B
N_CHUNK_B = H_B // 2


def _gqa_halves(c):
    return ((2 * c) // G_B) % 2, ((2 * c + 1) // G_B) % 2


SW_SUB = 4
SW_BLOCKS = SEQ // SW_Q


def _swa_kernel(q_ref, k_ref, vt_ref, mask_ref, sink_ref, sz_ref, o_ref, s_scr):
    step = pl.program_id(1)
    blocks = [SW_SUB * step + s for s in range(SW_SUB)]
    start = [pl.multiple_of(jnp.clip((n - 1) * SW_Q, 0, SEQ - SW_SPAN), SW_Q) for n in blocks]
    variant = [jnp.where(n == 0, 0, jnp.where(n == SW_BLOCKS - 1, 2, 1)) for n in blocks]

    def scores(i):
        s, c = divmod(i, N_CHUNK_B)
        kp = c // G_B
        cols = slice(c * LANES, (c + 1) * LANES)
        kcols = slice(kp * LANES, (kp + 1) * LANES)
        q2 = _pair_q(q_ref[0, s * SW_Q:(s + 1) * SW_Q, cols].astype(f32), *_gqa_halves(c))
        s_loc = _dot_nt(k_ref[0, pl.ds(start[s], SW_SPAN), kcols], q2) + mask_ref[variant[s]]
        s_ctx = _dot_nt(k_ref[0, SEQ:S_ALL, kcols], q2)
        s_scr[i % 2, 0:SW_SPAN, :] = s_loc
        s_scr[i % 2, SW_SPAN:, :] = s_ctx
        return jnp.maximum(jnp.maximum(_col_max(s_loc), _col_max(s_ctx)), sink_ref[c])

    def values(i, m):
        s, c = divmod(i, N_CHUNK_B)
        kp = c // G_B
        cols = slice(c * LANES, (c + 1) * LANES)
        rows = slice(s * SW_Q, (s + 1) * SW_Q)
        acc = (_dot(vt_ref[kp, :, pl.ds(start[s], SW_SPAN)], _prob(s_scr[i % 2, 0:SW_SPAN, :], m))
               + _dot(vt_ref[kp, :, SEQ:S_ALL], _prob(s_scr[i % 2, SW_SPAN:, :], m)))
        l = acc[LANES:LANES + 1] + jnp.exp2(sink_ref[c] - m)
        o = _pair_out(acc, l, SW_Q, *_gqa_halves(c))
        o_ref[0, rows, cols] = (o * sz_ref[0, rows, cols]).astype(bf16)

    _pipelined(SW_SUB * N_CHUNK_B, scores, values)


def _swa_mask():
    i = np.arange(SW_Q)
    j = np.arange(SW_SPAN)
    out = []
    for d0 in (0, SW_Q, 2 * SW_Q):
        ok = np.abs(d0 + i[None, :] - j[:, None]) <= SW_WINDOW
        m = np.where(ok, 0.0, NEG_INF).astype(np.float32)
        out.append(np.concatenate([m, m], axis=1))
    return jnp.asarray(np.stack(out))


def _sink_cols(sink, n):
    return jnp.repeat(sink.astype(f32).reshape(N_CHUNK_B, 2) * LOG2E, n, axis=1)[:, None, :]


def _swa(qk, vt, sink, sz):
    tq = SW_SUB * SW_Q
    return pl.pallas_call(
        _swa_kernel,
        grid=(BATCH, SW_BLOCKS // SW_SUB),
        in_specs=[
            pl.BlockSpec((1, tq, D_B), lambda b, n: (b, n, 0)),
            pl.BlockSpec((1, S_ALL, KV_B), lambda b, n: (b, 0, D_B // KV_B)),
            pl.BlockSpec((HKV_B // 2, VT_ROWS, S_ALL), lambda b, n: (H_A // HKV_B, 0, b)),
            pl.BlockSpec((3, SW_SPAN, 2 * SW_Q), lambda b, n: (0, 0, 0)),
            pl.BlockSpec((N_CHUNK_B, 1, 2 * SW_Q), lambda b, n: (0, 0, 0)),
            pl.BlockSpec((1, tq, D_B), lambda b, n: (b, n, 0)),
        ],
        out_specs=pl.BlockSpec((1, tq, D_B), lambda b, n: (b, n, 0)),
        out_shape=jax.ShapeDtypeStruct((BATCH, SEQ, D_B), bf16),
        scratch_shapes=[pltpu.VMEM((2, SW_SPAN + CTX_LEN, 2 * SW_Q), f32)],
        compiler_params=_cparams("arbitrary", "arbitrary"),
        name="swa",
    )(qk, qk, vt, _swa_mask(), _sink_cols(sink, SW_Q), sz)


MLA_TQ = 2048
MLA_SUB = 256
MLA_KCHUNK = 256
MLA_SCALE = float(QK_C) ** -0.5


def _mla_kernel(q_ref, k_ref, vt_ref, sz_ref, o_ref, s_scr):
    chunks = [slice(c * MLA_KCHUNK, (c + 1) * MLA_KCHUNK) for c in range(S_ALL // MLA_KCHUNK)]

    def scores(j):
        q = q_ref[0, j * MLA_SUB:(j + 1) * MLA_SUB, :]
        m = None
        for ck in chunks:
            st = _dot_nt(k_ref[0, ck, :], q)
            s_scr[j % 2, ck, :] = st
            mc = _col_max(st)
            m = mc if m is None else jnp.maximum(m, mc)
        return m

    def values(j, m):
        acc = None
        for ck in chunks:
            d = _dot(vt_ref[0, :, ck], _prob(s_scr[j % 2, ck, :], m))
            acc = d if acc is None else acc + d
        o = acc[:V_C] / acc[V_C:V_C + 1]
        rows = slice(j * MLA_SUB, (j + 1) * MLA_SUB)
        o_ref[0, rows, :] = (o.T * sz_ref[0, rows, :]).astype(bf16)

    _pipelined(MLA_TQ // MLA_SUB, scores, values)


def _mla(qc, kc, vt, sz):
    return pl.pallas_call(
        _mla_kernel,
        grid=(BATCH, H_C, SEQ // MLA_TQ),
        in_specs=[
            pl.BlockSpec((1, MLA_TQ, QC_PAD), lambda b, h, i: (b, i, h)),
            pl.BlockSpec((1, S_ALL, QC_PAD), lambda b, h, i: (b, 0, h)),
            pl.BlockSpec((1, VT_ROWS, S_ALL), lambda b, h, i: (h, 0, b)),
            pl.BlockSpec((1, MLA_TQ, V_C), lambda b, h, i: (b, i, h)),
        ],
        out_specs=pl.BlockSpec((1, MLA_TQ, V_C), lambda b, h, i: (b, i, h)),
        out_shape=jax.ShapeDtypeStruct((BATCH, SEQ, D_C), bf16),
        scratch_shapes=[pltpu.VMEM((2, S_ALL, MLA_SUB), f32)],
        compiler_params=_cparams("arbitrary", "arbitrary", "arbitrary"),
        name="mla",
    )(qc, kc, vt, sz)


def _ctx_kernel(qa_ref, ka_ref, qb_ref, kb_ref, vtab_ref, qc_ref, kc_ref, vtc_ref,
                sink_ref, sza_ref, szb_ref, szc_ref, oa_ref, ob_ref, oc_ref):
    n = CTX_LEN
    for p in range(H_A // 2):
        cols = slice(p * LANES, (p + 1) * LANES)
        s = _dot_nt(ka_ref[0, :, cols], _pair_q(qa_ref[0, :, cols].astype(f32), 0, 1))
        acc = _dot(vtab_ref[p], _prob(s, _col_max(s)))
        o = _pair_out(acc, acc[LANES:LANES + 1], n, 0, 1)
        oa_ref[0, :, cols] = (o * sza_ref[0, :, cols]).astype(bf16)
    for c in range(N_CHUNK_B):
        kp = c // G_B
        cols = slice(c * LANES, (c + 1) * LANES)
        s = _dot_nt(kb_ref[0, :, kp * LANES:(kp + 1) * LANES],
                    _pair_q(qb_ref[0, :, cols].astype(f32), *_gqa_halves(c)))
        m = jnp.maximum(_col_max(s), sink_ref[c])
        acc = _dot(vtab_ref[H_A // 2 + kp], _prob(s, m))
        l = acc[LANES:LANES + 1] + jnp.exp2(sink_ref[c] - m)
        o = _pair_out(acc, l, n, *_gqa_halves(c))
        ob_ref[0, :, cols] = (o * szb_ref[0, :, cols]).astype(bf16)
    for h in range(H_C):
        qcols = slice(h * QC_PAD, (h + 1) * QC_PAD)
        vcols = slice(h * V_C, (h + 1) * V_C)
        s = _dot_nt(kc_ref[0, :, qcols], qc_ref[0, :, qcols])
        acc = _dot(vtc_ref[h], _prob(s, _col_max(s)))
        o = acc[:V_C] / acc[V_C:V_C + 1]
        oc_ref[0, :, vcols] = (o.T * szc_ref[0, :, vcols]).astype(bf16)


def _ctx_attention(qka, qkb, vt_ab, qc, kc, vt_c, sink, sza, szb, szc):
    cb = SEQ // CTX_LEN

    def spec(width, col):
        return pl.BlockSpec((1, CTX_LEN, width), lambda b: (b, cb, col))

    def vt_spec(tiles):
        return pl.BlockSpec((tiles, VT_ROWS, CTX_LEN), lambda b: (0, 0, b * (S_ALL // CTX_LEN) + cb))

    return pl.pallas_call(
        _ctx_kernel,
        grid=(BATCH,),
        in_specs=[
            spec(D_A, 0), spec(D_A, 1),
            spec(D_B, 0), spec(KV_B, D_B // KV_B),
            vt_spec((H_A + HKV_B) // 2),
            spec(H_C * QC_PAD, 0), spec(H_C * QC_PAD, 0),
            vt_spec(H_C),
            pl.BlockSpec((N_CHUNK_B, 1, 2 * CTX_LEN), lambda b: (0, 0, 0)),
            spec(D_A, 0), spec(D_B, 0), spec(D_C, 0),
        ],
        out_specs=[
            pl.BlockSpec((1, CTX_LEN, D_A), lambda b: (b, 0, 0)),
            pl.BlockSpec((1, CTX_LEN, D_B), lambda b: (b, 0, 0)),
            pl.BlockSpec((1, CTX_LEN, D_C), lambda b: (b, 0, 0)),
        ],
        out_shape=[
            jax.ShapeDtypeStruct((BATCH, CTX_LEN, D_A), bf16),
            jax.ShapeDtypeStruct((BATCH, CTX_LEN, D_B), bf16),
            jax.ShapeDtypeStruct((BATCH, CTX_LEN, D_C), bf16),
        ],
        compiler_params=_cparams("arbitrary"),
        name="ctx_attention",
    )(qka, qka, qkb, qkb, vt_ab, qc, kc, vt_c, _sink_cols(sink, CTX_LEN), sza, szb, szc)


OUT_TM = 512
OUT_TN = 2048


def _out_kernel(x_ref, ga_ref, gb_ref, gc_ref, w_ref, mod_ref, o_ref):
    g = jnp.concatenate([ga_ref[0], gb_ref[0], gc_ref[0]], axis=1)
    o_ref[0] = x_ref[0] + mod_ref[0] * _dot(g, w_ref[...])


def _out_proj(x, ga, gb, gc, w_out_bf, mod3, is_ctx):
    t = x.shape[1]
    tm = min(t, OUT_TM)
    tn = OUT_TN
    gate_col0 = 2 * D_MODEL // tn
    mod_row = (lambda b: BATCH) if is_ctx else (lambda b: b)
    return pl.pallas_call(
        _out_kernel,
        grid=(BATCH, t // tm, D_MODEL // tn),
        in_specs=[
            pl.BlockSpec((1, tm, tn), lambda b, i, j: (b, i, j)),
            pl.BlockSpec((1, tm, D_A), lambda b, i, j: (b, i, 0)),
            pl.BlockSpec((1, tm, D_B), lambda b, i, j: (b, i, 0)),
            pl.BlockSpec((1, tm, D_C), lambda b, i, j: (b, i, 0)),
            pl.BlockSpec((D_MIX, tn), lambda b, i, j: (0, j)),
            pl.BlockSpec((1, 1, tn), lambda b, i, j: (mod_row(b), 0, gate_col0 + j)),
        ],
        out_specs=pl.BlockSpec((1, tm, tn), lambda b, i, j: (b, i, j)),
        out_shape=jax.ShapeDtypeStruct(x.shape, f32),
        compiler_params=_cparams("arbitrary", "arbitrary", "arbitrary"),
        name="out_proj_ctx" if is_ctx else "out_proj",
    )(x, ga, gb, gc, w_out_bf, mod3)


def _rope_tables():
    t = jnp.arange(SEQ)
    row = (t // GRID_W).astype(f32)
    col = (t % GRID_W).astype(f32)
    n_freq = ROPE_DIM // 4
    inv = ROPE_BASE ** (-jnp.arange(n_freq, dtype=f32) / n_freq)
    ar = row[:, None] * inv
    ac = col[:, None] * inv
    ang = jnp.concatenate([ar, ar, ac, ac], axis=-1)
    cos = jnp.cos(ang).astype(f32)
    sin = jnp.sin(ang).astype(f32)
    sign = jnp.asarray(np.where((np.arange(ROPE_DIM) % 32) < 16, -1.0, 1.0), dtype=f32)
    sin = sin * sign
    cos = jnp.concatenate([cos, jnp.ones((CTX_LEN, ROPE_DIM), f32)], axis=0)
    sin = jnp.concatenate([sin, jnp.zeros((CTX_LEN, ROPE_DIM), f32)], axis=0)
    pair = (jnp.concatenate([cos, cos], axis=1), jnp.concatenate([sin, sin], axis=1))
    single = (jnp.concatenate([cos, jnp.ones_like(cos)], axis=1), jnp.concatenate([sin, jnp.zeros_like(sin)], axis=1))
    return pair, single


def _pad_heads(w, width, padded):
    lead = w.shape[:-1]
    w = w.reshape(lead + (H_C, width))
    w = jnp.pad(w, [(0, 0)] * len(lead) + [(0, 0), (0, padded - width)])
    return w.reshape(lead + (H_C * padded,))


def kernel(x, c, ctx, c_ctx, norm_w, w_ada, b_ada, w_in, qn_a, kn_a, rpb_a, qn_b, kn_b, sink_b,
           qa_norm, kva_norm, w_qb, w_kvb, qn_c, kn_c, w_out):
    (cos2, sin2), (cos1, sin1) = _rope_tables()
    cvec = jnp.concatenate([c, c_ctx[None, :], jnp.zeros((MOD_ROWS - BATCH - 1, D_MODEL), f32)], axis=0)
    mod = _modulation(cvec, w_ada, b_ada)
    q_scale_ab = HD ** -0.5 * LOG2E
    q_scale_c = MLA_SCALE * LOG2E
    w_out_bf = w_out.astype(bf16)
    na_bias = _natten_bias(rpb_a)

    o_qa, o_ka, o_va, o_qb, o_kb, o_vb, o_cq, o_ckv, o_kpe, o_z = np.cumsum(
        (0, D_A, D_A, D_A, D_B, KV_B, KV_B, Q_LORA, KV_LORA, ROPE_DIM)).tolist()

    xc = ctx
    for l in range(DEPTH):
        last = l == DEPTH - 1
        mod3 = mod[l].reshape(MOD_ROWS, 1, 3 * D_MODEL)
        h = _hnorm(x, xc, norm_w[l], mod3).reshape(R_ALL, D_MODEL)
        wl = w_in[l]

        gain_hn = jnp.concatenate([jnp.tile(qn_a[l] * q_scale_ab, H_A), jnp.tile(kn_a[l], H_A),
                                   jnp.tile(qn_b[l] * q_scale_ab, H_B), jnp.tile(kn_b[l], HKV_B)])
        w_hn = jnp.concatenate([wl[:, o_qa:o_va], wl[:, o_qb:o_vb]], axis=1).astype(bf16)
        w_lora = jnp.pad(wl[:, o_cq:o_z], ((0, 0), (0, LANES - ROPE_DIM))).astype(bf16)
        wt_v = jnp.concatenate([wl[:, o_va:o_qb], wl[:, o_vb:o_cq]], axis=1).T.astype(bf16)
        qka, qkb, cqn, ckvn, kpe, vt_ab = _proj_main(h, w_hn, w_lora, wt_v, gain_hn, qa_norm[l], kva_norm[l],
                                                      cos2, sin2)
        sza, szb, szc = _proj_gate(h, wl[:, o_z:].astype(bf16))

        wq = _pad_heads(w_qb[l], QK_C, QC_PAD).astype(bf16)
        wkv = w_kvb[l].reshape(KV_LORA, H_C, NOPE_C + V_C)
        wkn = wkv[:, :, :NOPE_C].reshape(KV_LORA, H_C * NOPE_C).astype(bf16)
        wvt = wkv[:, :, NOPE_C:].reshape(KV_LORA, H_C * V_C).T.astype(bf16)
        gq = _pad_heads(jnp.tile(qn_c[l] * q_scale_c, H_C), QK_C, QC_PAD).reshape(1, H_C * QC_PAD)
        gk = jnp.pad(kn_c[l], (0, QC_PAD - QK_C)).reshape(1, QC_PAD)
        qc, kc, vt_c = _mla_up(cqn, ckvn, kpe, wq, wkn, wvt, gq, gk, cos1, sin1)

        def b3(a):
            return a.reshape(BATCH, S_ALL, a.shape[-1])

        qka, qkb, qc, kc, sza, szb, szc = map(b3, (qka, qkb, qc, kc, sza, szb, szc))

        ga = _natten(qka, vt_ab, na_bias, l, sza)
        gb = _swa(qkb, vt_ab, sink_b[l], szb)
        gc = _mla(qc, kc, vt_c, szc)
        if not last:
            ga_c, gb_c, gc_c = _ctx_attention(qka, qkb, vt_ab, qc, kc, vt_c, sink_b[l], sza, szb, szc)
            xc = _out_proj(xc, ga_c, gb_c, gc_c, w_out_bf[l], mod3, True)
        x = _out_proj(x, ga, gb, gc, w_out_bf[l], mod3, False)
    return x
```

```python
import functools
import math

import numpy as np
import jax
import jax.numpy as jnp
from jax import lax
from jax.experimental import pallas as pl
from jax.experimental.pallas import tpu as pltpu

D_MODEL = 2048
BATCH = 8
SEQ = 2048
DEPTH = 2
GRID_W = 64
CTX_LEN = 256
HD = 64
H_A = 8
D_A = H_A * HD
H_B = 12
HKV_B = 4
D_B = H_B * HD
KV_B = HKV_B * HD
H_C = 6
NOPE_C = 128
ROPE_DIM = 64
QK_C = NOPE_C + ROPE_DIM
V_C = 128
D_C = H_C * V_C
Q_LORA = 768
KV_LORA = 512
D_MIX = D_A + D_B + D_C
NA_ROWS = 8
NA_COLS = 16
SW_WINDOW = 128
ROPE_BASE = 10000.0
EPS = 1e-6
NEG_INF = -1e30

S_ALL = SEQ + CTX_LEN
R_ALL = BATCH * S_ALL
GRID_ROWS = SEQ // GRID_W
LANES = 128
MXU_DIM = 256
BF16_ROWS = 16
QC_PAD = 256
VT_ROWS = LANES + BF16_ROWS
MOD_ROWS = 16
VMEM_LIMIT = 48 * 1024 * 1024
LOG2E = math.log2(math.e)

bf16 = jnp.bfloat16
f32 = jnp.float32


def _cparams(*sem):
    return pltpu.CompilerParams(dimension_semantics=sem, vmem_limit_bytes=VMEM_LIMIT)


def _silu(v):
    return v / (1.0 + jnp.exp(-v))


def _lane_iota(shape):
    return lax.broadcasted_iota(jnp.int32, shape, len(shape) - 1)


def _rope_rotate(xn, cos, sin_signed):
    lane = _lane_iota(xn.shape)
    take_next = (lane % 32) < 16
    rot = jnp.where(take_next, pltpu.roll(xn, LANES - 16, 1), pltpu.roll(xn, 16, 1))
    return xn * cos + rot * sin_signed


def _dot_nt(a, b):
    return lax.dot_general(a, b, (((1,), (1,)), ((), ())), preferred_element_type=f32)


def _dot(a, b):
    return jnp.dot(a, b, preferred_element_type=f32)


def _mod_kernel(c_ref, w_ref, b_ref, o_ref):
    sc = _silu(c_ref[...]).astype(bf16)
    o_ref[0] = _dot(sc, w_ref[0].astype(bf16)) + b_ref[0]


def _modulation(cvec, w_ada, b_ada):
    tn = 1024
    n = 3 * D_MODEL
    return pl.pallas_call(
        _mod_kernel,
        grid=(DEPTH, n // tn),
        in_specs=[
            pl.BlockSpec((MOD_ROWS, D_MODEL), lambda l, j: (0, 0)),
            pl.BlockSpec((1, D_MODEL, tn), lambda l, j: (l, 0, j)),
            pl.BlockSpec((1, 1, tn), lambda l, j: (l, 0, j)),
        ],
        out_specs=pl.BlockSpec((1, MOD_ROWS, tn), lambda l, j: (l, 0, j)),
        out_shape=jax.ShapeDtypeStruct((DEPTH, MOD_ROWS, n), f32),
        compiler_params=_cparams("arbitrary", "arbitrary"),
        name="modulation",
    )(cvec, w_ada, b_ada.reshape(DEPTH, 1, n))


H_ROWS = 256
N_LAT_BLOCKS = SEQ // H_ROWS


def _modulated_norm(x, nw, mod):
    ms = jnp.mean(x * x, axis=-1, keepdims=True)
    y = x * lax.rsqrt(ms + EPS) * nw
    return y * (1.0 + mod[:, D_MODEL:2 * D_MODEL]) + mod[:, 0:D_MODEL]


def _hnorm_kernel(x_ref, ctx_ref, nw_ref, mod_ref, h_ref):
    s = pl.program_id(1)

    def body(src):
        h_ref[0] = _modulated_norm(src[0], nw_ref[...], mod_ref[0]).astype(bf16)

    @pl.when(s < N_LAT_BLOCKS)
    def _():
        body(x_ref)

    @pl.when(s >= N_LAT_BLOCKS)
    def _():
        body(ctx_ref)


def _hnorm(x, xc, norm_w, mod3):
    return pl.pallas_call(
        _hnorm_kernel,
        grid=(BATCH, S_ALL // H_ROWS),
        in_specs=[
            pl.BlockSpec((1, H_ROWS, D_MODEL), lambda b, s: (b, jnp.minimum(s, N_LAT_BLOCKS - 1), 0)),
            pl.BlockSpec((1, H_ROWS, D_MODEL), lambda b, s: (b, 0, 0)),
            pl.BlockSpec((1, D_MODEL), lambda b, s: (0, 0)),
            pl.BlockSpec((1, 1, 3 * D_MODEL), lambda b, s: (jnp.where(s < N_LAT_BLOCKS, b, BATCH), 0, 0)),
        ],
        out_specs=pl.BlockSpec((1, H_ROWS, D_MODEL), lambda b, s: (b, s, 0)),
        out_shape=jax.ShapeDtypeStruct((BATCH, S_ALL, D_MODEL), bf16),
        compiler_params=_cparams("arbitrary", "arbitrary"),
        name="hnorm",
    )(x, xc, norm_w.reshape(1, D_MODEL), mod3)


PROJ_TM = 768


def _group_mean_sq(u, g_ref):
    sq = u * u
    hi = sq.astype(bf16)
    lo = (sq - hi.astype(f32)).astype(bf16)
    g = g_ref[...]
    return _dot(hi, g) + _dot(lo, g)


def _group_matrix():
    idx = np.arange(MXU_DIM) // HD
    return jnp.asarray((idx[:, None] == idx[None, :]).astype(np.float32) / HD, dtype=bf16)


def _run_items(items):
    u = items[0][0]()
    for i, (_, epilogue) in enumerate(items):
        u_next = items[i + 1][0]() if i + 1 < len(items) else None
        epilogue(u)
        u = u_next


def _store_vt_tile(vt_ref, p, ut):
    vt_ref[p, 0:LANES, :] = ut.astype(bf16)
    vt_ref[p, LANES:VT_ROWS, :] = jnp.ones((VT_ROWS - LANES, ut.shape[1]), bf16)


PROJ_TN = 512
N_HN = 2 * D_A + D_B + KV_B
N_LORA = Q_LORA + KV_LORA + LANES
N_VT = D_A + KV_B


def _proj_main_kernel(h_ref, whn_ref, wl_ref, wvt_ref, ghn_ref, g_ref, cos_ref, sin_ref, gcq_ref, gckv_ref,
                      qka_ref, qkb_ref, cqn_ref, ckvn_ref, kpe_ref, vt_ref):
    def hn_item(k):
        cols = slice(k * PROJ_TN, (k + 1) * PROJ_TN)
        rope = k * PROJ_TN >= 2 * D_A
        o_ref = qkb_ref if rope else qka_ref
        o0 = k * PROJ_TN - (2 * D_A if rope else 0)

        def epilogue(u):
            for c0 in range(0, PROJ_TN, MXU_DIM):
                uc = u[:, c0:c0 + MXU_DIM]
                ms = _group_mean_sq(uc, g_ref)
                xn = uc * lax.rsqrt(ms + EPS) * ghn_ref[:, k * PROJ_TN + c0:k * PROJ_TN + c0 + MXU_DIM]
                if rope:
                    for c1 in range(0, MXU_DIM, LANES):
                        o_ref[:, o0 + c0 + c1:o0 + c0 + c1 + LANES] = _rope_rotate(
                            xn[:, c1:c1 + LANES], cos_ref[...], sin_ref[...]).astype(bf16)
                else:
                    o_ref[:, o0 + c0:o0 + c0 + MXU_DIM] = xn.astype(bf16)

        return lambda: _dot(h_ref[...], whn_ref[:, cols]), epilogue

    def rowrms_item(c0, width, gain_ref, o_ref):
        def epilogue(u):
            ms = jnp.mean(u * u, axis=-1, keepdims=True)
            o_ref[...] = (u * lax.rsqrt(ms + EPS) * gain_ref[...]).astype(bf16)

        return lambda: _dot(h_ref[...], wl_ref[:, c0:c0 + width]), epilogue

    def kpe_item():
        def epilogue(u):
            kpe_ref[...] = u

        return lambda: _dot(h_ref[...], wl_ref[:, Q_LORA + KV_LORA:N_LORA]), epilogue

    def vt_item(p):
        def epilogue(ut):
            _store_vt_tile(vt_ref, p, ut)

        return lambda: _dot_nt(wvt_ref[p * LANES:(p + 1) * LANES, :], h_ref[...]), epilogue

    _run_items([hn_item(k) for k in range(N_HN // PROJ_TN)]
               + [rowrms_item(0, Q_LORA, gcq_ref, cqn_ref), rowrms_item(Q_LORA, KV_LORA, gckv_ref, ckvn_ref),
                  kpe_item()]
               + [vt_item(p) for p in range(N_VT // LANES)])


def _resident(shape):
    return pl.BlockSpec(shape, lambda i: (0,) * len(shape), pipeline_mode=pl.Buffered(1))


def _proj_main(h2d, whn, wl, wvt, ghn, gcq, gckv, cos, sin):
    tm = PROJ_TM
    nt = S_ALL // tm
    row = lambda i: (i, 0)
    tiles = N_VT // LANES
    return pl.pallas_call(
        _proj_main_kernel,
        grid=(R_ALL // tm,),
        in_specs=[
            pl.BlockSpec((tm, D_MODEL), row),
            _resident((D_MODEL, N_HN)), _resident((D_MODEL, N_LORA)), _resident((N_VT, D_MODEL)),
            _resident((1, N_HN)), _resident((MXU_DIM, MXU_DIM)),
            pl.BlockSpec((tm, LANES), lambda i: (i % nt, 0)),
            pl.BlockSpec((tm, LANES), lambda i: (i % nt, 0)),
            _resident((1, Q_LORA)), _resident((1, KV_LORA)),
        ],
        out_specs=[
            pl.BlockSpec((tm, 2 * D_A), row),
            pl.BlockSpec((tm, D_B + KV_B), row),
            pl.BlockSpec((tm, Q_LORA), row),
            pl.BlockSpec((tm, KV_LORA), row),
            pl.BlockSpec((tm, LANES), row),
            pl.BlockSpec((tiles, VT_ROWS, tm), lambda i: (0, 0, i)),
        ],
        out_shape=[
            jax.ShapeDtypeStruct((R_ALL, 2 * D_A), bf16),
            jax.ShapeDtypeStruct((R_ALL, D_B + KV_B), bf16),
            jax.ShapeDtypeStruct((R_ALL, Q_LORA), bf16),
            jax.ShapeDtypeStruct((R_ALL, KV_LORA), bf16),
            jax.ShapeDtypeStruct((R_ALL, LANES), f32),
            jax.ShapeDtypeStruct((tiles, VT_ROWS, R_ALL), bf16),
        ],
        compiler_params=_cparams("arbitrary"),
        name="proj_main",
    )(h2d, whn, wl, wvt, ghn.reshape(1, N_HN), _group_matrix(), cos, sin,
      gcq.reshape(1, Q_LORA), gckv.reshape(1, KV_LORA))


def _proj_gate_kernel(h_ref, wz_ref, sza_ref, szb_ref, szc_ref):
    outs = ((sza_ref, 0, D_A), (szb_ref, D_A, D_B), (szc_ref, D_A + D_B, D_C))

    def item(k):
        c0 = k * PROJ_TN

        def epilogue(u):
            sz = _silu(u)
            for o_ref, start, width in outs:
                lo, hi = max(c0, start), min(c0 + PROJ_TN, start + width)
                if lo < hi:
                    o_ref[:, lo - start:hi - start] = sz[:, lo - c0:hi - c0]

        return lambda: _dot(h_ref[...], wz_ref[:, c0:c0 + PROJ_TN]), epilogue

    _run_items([item(k) for k in range(D_MIX // PROJ_TN)])


def _proj_gate(h2d, wz):
    tm = PROJ_TM
    row = lambda i: (i, 0)
    return pl.pallas_call(
        _proj_gate_kernel,
        grid=(R_ALL // tm,),
        in_specs=[pl.BlockSpec((tm, D_MODEL), row), _resident((D_MODEL, D_MIX))],
        out_specs=[pl.BlockSpec((tm, D_A), row), pl.BlockSpec((tm, D_B), row), pl.BlockSpec((tm, D_C), row)],
        out_shape=[jax.ShapeDtypeStruct((R_ALL, D_A), f32), jax.ShapeDtypeStruct((R_ALL, D_B), f32),
                   jax.ShapeDtypeStruct((R_ALL, D_C), f32)],
        compiler_params=_cparams("arbitrary"),
        name="proj_gate",
    )(h2d, wz)


def _mla_up_kernel(cq_ref, ckv_ref, kpe_ref, wq_ref, wkn_ref, wvt_ref, gq_ref, gk_ref,
                   cos_ref, sin_ref, qc_ref, kc_ref, vt_ref):
    cos = cos_ref[...]
    sin = sin_ref[...]
    kpe = kpe_ref[...]
    ss_pe = jnp.sum(kpe * kpe, axis=-1, keepdims=True)
    k_rot = _rope_rotate(kpe * gk_ref[:, LANES:], cos, sin)

    def q_head(h):
        cols = slice(h * QC_PAD, (h + 1) * QC_PAD)
        return lambda: _dot(cq_ref[...], wq_ref[:, cols]), functools.partial(q_epilogue, h)

    def q_epilogue(h, u):
        ms = jnp.sum(u * u, axis=-1, keepdims=True) * (1.0 / QK_C)
        xn = u * lax.rsqrt(ms + EPS) * gq_ref[:, h * QC_PAD:(h + 1) * QC_PAD]
        qc_ref[:, h * QC_PAD:h * QC_PAD + LANES] = xn[:, :LANES].astype(bf16)
        qc_ref[:, h * QC_PAD + LANES:(h + 1) * QC_PAD] = _rope_rotate(xn[:, LANES:], cos, sin).astype(bf16)

    def k_head(h):
        cols = slice(h * NOPE_C, (h + 1) * NOPE_C)
        return lambda: _dot(ckv_ref[...], wkn_ref[:, cols]), functools.partial(k_epilogue, h)

    def k_epilogue(h, u):
        ms = (jnp.sum(u * u, axis=-1, keepdims=True) + ss_pe) * (1.0 / QK_C)
        r = lax.rsqrt(ms + EPS)
        kc_ref[:, h * QC_PAD:h * QC_PAD + LANES] = (u * r * gk_ref[:, 0:LANES]).astype(bf16)
        kc_ref[:, h * QC_PAD + LANES:(h + 1) * QC_PAD] = (k_rot * r).astype(bf16)

    def v_tile(p):
        rows = slice(p * LANES, (p + 1) * LANES)

        def epilogue(ut):
            _store_vt_tile(vt_ref, p, ut)

        return lambda: _dot_nt(wvt_ref[rows, :], ckv_ref[...]), epilogue

    _run_items([q_head(h) for h in range(H_C)] + [k_head(h) for h in range(H_C)] + [v_tile(p) for p in range(H_C)])


def _mla_up(cqn, ckvn, kpe, wq, wkn, wvt, gq, gk, cos, sin):
    tm = PROJ_TM
    nt = S_ALL // tm
    row = lambda i: (i, 0)
    fixed = lambda i: (0, 0)
    return pl.pallas_call(
        _mla_up_kernel,
        grid=(R_ALL // tm,),
        in_specs=[
            pl.BlockSpec((tm, Q_LORA), row),
            pl.BlockSpec((tm, KV_LORA), row),
            pl.BlockSpec((tm, LANES), row),
            pl.BlockSpec((Q_LORA, H_C * QC_PAD), fixed),
            pl.BlockSpec((KV_LORA, H_C * NOPE_C), fixed),
            pl.BlockSpec((H_C * V_C, KV_LORA), fixed),
            pl.BlockSpec((1, H_C * QC_PAD), fixed),
            pl.BlockSpec((1, QC_PAD), fixed),
            pl.BlockSpec((tm, LANES), lambda i: (i % nt, 0)),
            pl.BlockSpec((tm, LANES), lambda i: (i % nt, 0)),
        ],
        out_specs=[
            pl.BlockSpec((tm, H_C * QC_PAD), row),
            pl.BlockSpec((tm, H_C * QC_PAD), row),
            pl.BlockSpec((H_C, VT_ROWS, tm), lambda i: (0, 0, i)),
        ],
        out_shape=[
            jax.ShapeDtypeStruct((R_ALL, H_C * QC_PAD), bf16),
            jax.ShapeDtypeStruct((R_ALL, H_C * QC_PAD), bf16),
            jax.ShapeDtypeStruct((H_C, VT_ROWS, R_ALL), bf16),
        ],
        compiler_params=_cparams("arbitrary"),
        name="mla_up",
    )(cqn, ckvn, kpe, wq, wkn, wvt, gq, gk, cos, sin)


def _pipelined(n, score_fn, value_fn):
    m = score_fn(0)
    for j in range(n):
        m_next = score_fn(j + 1) if j + 1 < n else None
        value_fn(j, m)
        m = m_next


def _col_max(s):
    return jnp.max(s, axis=0, keepdims=True)


def _prob(s, m):
    return jnp.exp2((s - m).astype(bf16))


def _pair_q(qv, half_even, half_odd):
    low = _lane_iota(qv.shape) < HD
    zero = jnp.zeros_like(qv)
    qa = jnp.where(low, qv, zero) if half_even == 0 else jnp.where(low, zero, pltpu.roll(qv, HD, 1))
    qb = jnp.where(low, zero, qv) if half_odd == 1 else jnp.where(low, pltpu.roll(qv, HD, 1), zero)
    return jnp.concatenate([qa, qb], axis=0).astype(bf16)


def _pair_out(acc, l, n, half_even, half_odd):
    o = acc[:LANES] / l
    t = jnp.concatenate([o[half_even * HD:(half_even + 1) * HD, 0:n],
                         o[half_odd * HD:(half_odd + 1) * HD, n:2 * n]], axis=0)
    return t.T


NA_QROWS = 2
NA_Q = NA_QROWS * GRID_W
NA_KROWS = NA_ROWS + NA_QROWS
NA_KEYS = NA_KROWS * GRID_W
NA_VARIANTS = ((0, 0, 0, -1), (0, -2, 0, -3), (0, -4, 1, -5), (0, -4, 0, -5), (0, -6, 0, -7))


def _na_window_row(j):
    return jnp.clip(NA_QROWS * j - NA_ROWS // 2, 0, GRID_ROWS - NA_ROWS)


NA_SUB = 2
NA_BLOCKS = GRID_ROWS // NA_QROWS


def _natten_kernel(q_ref, k_ref, vt_ref, bias0_ref, bias1_ref, sz_ref, o_ref, s_scr):
    step = pl.program_id(1)
    bias_refs = (bias0_ref, bias1_ref)
    k0 = [pl.multiple_of(_na_window_row(NA_SUB * step + s) * GRID_W, LANES) for s in range(NA_SUB)]
    n_pair = H_A // 2

    def scores(i):
        s, p = divmod(i, n_pair)
        cols = slice(p * LANES, (p + 1) * LANES)
        q2 = _pair_q(q_ref[0, s * NA_Q:(s + 1) * NA_Q, cols].astype(f32), 0, 1)
        s_loc = _dot_nt(k_ref[0, pl.ds(k0[s], NA_KEYS), cols], q2) + bias_refs[s][0, 0, p]
        s_ctx = _dot_nt(k_ref[0, SEQ:S_ALL, cols], q2)
        s_scr[i % 2, 0:NA_KEYS, :] = s_loc
        s_scr[i % 2, NA_KEYS:, :] = s_ctx
        return jnp.maximum(_col_max(s_loc), _col_max(s_ctx))

    def values(i, m):
        s, p = divmod(i, n_pair)
        cols = slice(p * LANES, (p + 1) * LANES)
        rows = slice(s * NA_Q, (s + 1) * NA_Q)
        acc = (_dot(vt_ref[p, :, pl.ds(k0[s], NA_KEYS)], _prob(s_scr[i % 2, 0:NA_KEYS, :], m))
               + _dot(vt_ref[p, :, SEQ:S_ALL], _prob(s_scr[i % 2, NA_KEYS:, :], m)))
        o = _pair_out(acc, acc[LANES:LANES + 1], NA_Q, 0, 1)
        o_ref[0, rows, cols] = (o * sz_ref[0, rows, cols]).astype(bf16)

    _pipelined(NA_SUB * n_pair, scores, values)


def _natten(qk, vt, bias, layer, sz):
    def variant(s):
        def index(b, step):
            j = NA_SUB * step + s
            v = jnp.where(j <= 1, j, jnp.where(j >= NA_BLOCKS - 2, j - (NA_BLOCKS - 5), 2))
            return (layer, v, 0, 0, 0)
        return index

    tq = NA_SUB * NA_Q
    bias_block = (1, 1, H_A // 2, NA_KEYS, 2 * NA_Q)
    return pl.pallas_call(
        _natten_kernel,
        grid=(BATCH, NA_BLOCKS // NA_SUB),
        in_specs=[
            pl.BlockSpec((1, tq, D_A), lambda b, j: (b, j, 0)),
            pl.BlockSpec((1, S_ALL, D_A), lambda b, j: (b, 0, 1)),
            pl.BlockSpec((H_A // 2, VT_ROWS, S_ALL), lambda b, j: (0, 0, b)),
            pl.BlockSpec(bias_block, variant(0)),
            pl.BlockSpec(bias_block, variant(1)),
            pl.BlockSpec((1, tq, D_A), lambda b, j: (b, j, 0)),
        ],
        out_specs=pl.BlockSpec((1, tq, D_A), lambda b, j: (b, j, 0)),
        out_shape=jax.ShapeDtypeStruct((BATCH, SEQ, D_A), bf16),
        scratch_shapes=[pltpu.VMEM((2, NA_KEYS + CTX_LEN, 2 * NA_Q), f32)],
        compiler_params=_cparams("arbitrary", "arbitrary"),
        name="natten",
    )(qk, qk, vt, bias, bias, sz)


def _natten_bias(rpb):
    n_dc = 2 * NA_COLS - 1
    n_dr = NA_KROWS + NA_ROWS - 1
    c = np.arange(LANES) % GRID_W
    kc = np.arange(GRID_W)
    qstart = np.clip(c - NA_COLS // 2, 0, GRID_W - NA_COLS)
    col_ok = (kc[:, None] >= qstart[None, :]) & (kc[:, None] < qstart[None, :] + NA_COLS)
    dc = np.clip(kc[:, None] - c[None, :], -(NA_COLS - 1), NA_COLS - 1) + NA_COLS - 1
    onehot = jnp.asarray((dc[None] == np.arange(n_dc)[:, None, None]).astype(np.float32))
    exp = jnp.einsum("lhrd,dkc->lhrkc", rpb.astype(f32) * LOG2E, onehot, precision=lax.Precision.HIGHEST)
    exp = jnp.pad(exp, ((0, 0), (0, 0), (0, n_dr - exp.shape[2]), (0, 0), (0, 0)))
    col_mask = jnp.asarray(np.where(col_ok, 0.0, NEG_INF).astype(np.float32))
    n_var = len(NA_VARIANTS)
    return pl.pallas_call(
        _natten_bias_kernel,
        grid=(DEPTH, n_var, H_A // 2),
        in_specs=[
            pl.BlockSpec((1, 2, n_dr, GRID_W, LANES), lambda l, v, p: (l, p, 0, 0, 0)),
            pl.BlockSpec((GRID_W, LANES), lambda l, v, p: (0, 0)),
        ],
        out_specs=pl.BlockSpec((1, 1, 1, NA_KEYS, 2 * NA_Q), lambda l, v, p: (l, v, p, 0, 0)),
        out_shape=jax.ShapeDtypeStruct((DEPTH, n_var, H_A // 2, NA_KEYS, 2 * NA_Q), f32),
        compiler_params=_cparams("arbitrary", "arbitrary", "arbitrary"),
        name="natten_bias",
    )(exp, col_mask)


def _natten_bias_kernel(exp_ref, mask_ref, o_ref):
    v = pl.program_id(1)

    def pick(column):
        out = jnp.int32(0)
        for i, var in enumerate(NA_VARIANTS):
            out = jnp.where(v == i, var[column], out)
        return out

    lo = (pick(0), pick(2))
    off = (pick(1), pick(3))
    low = _lane_iota((GRID_W, LANES)) < GRID_W
    mask = mask_ref[...]
    for kr in range(NA_KROWS):
        for e in range(2):
            halves = []
            for qr in range(NA_QROWS):
                blk = exp_ref[0, e, pl.ds(kr + off[qr] + NA_ROWS - 1, 1)][0] + mask
                valid = (kr >= lo[qr]) & (kr < lo[qr] + NA_ROWS)
                halves.append(jnp.where(valid, blk, NEG_INF))
            o_ref[0, 0, 0, kr * GRID_W:(kr + 1) * GRID_W, e * NA_Q:(e + 1) * NA_Q] = jnp.where(
                low, halves[0], halves[1])


SW_Q = 128
SW_SPAN = SW_Q + 2 * SW_WINDOW
G_B = H_B // HKV_B
N_CHUNK_B = H_B // 2


def _gqa_halves(c):
    return ((2 * c) // G_B) % 2, ((2 * c + 1) // G_B) % 2


SW_SUB = 4
SW_BLOCKS = SEQ // SW_Q


def _swa_kernel(q_ref, k_ref, vt_ref, mask_ref, sink_ref, sz_ref, o_ref, s_scr):
    step = pl.program_id(1)
    blocks = [SW_SUB * step + s for s in range(SW_SUB)]
    start = [pl.multiple_of(jnp.clip((n - 1) * SW_Q, 0, SEQ - SW_SPAN), SW_Q) for n in blocks]
    variant = [jnp.where(n == 0, 0, jnp.where(n == SW_BLOCKS - 1, 2, 1)) for n in blocks]

    def scores(i):
        s, c = divmod(i, N_CHUNK_B)
        kp = c // G_B
        cols = slice(c * LANES, (c + 1) * LANES)
        kcols = slice(kp * LANES, (kp + 1) * LANES)
        q2 = _pair_q(q_ref[0, s * SW_Q:(s + 1) * SW_Q, cols].astype(f32), *_gqa_halves(c))
        s_loc = _dot_nt(k_ref[0, pl.ds(start[s], SW_SPAN), kcols], q2) + mask_ref[variant[s]]
        s_ctx = _dot_nt(k_ref[0, SEQ:S_ALL, kcols], q2)
        s_scr[i % 2, 0:SW_SPAN, :] = s_loc
        s_scr[i % 2, SW_SPAN:, :] = s_ctx
        return jnp.maximum(jnp.maximum(_col_max(s_loc), _col_max(s_ctx)), sink_ref[c])

    def values(i, m):
        s, c = divmod(i, N_CHUNK_B)
        kp = c // G_B
        cols = slice(c * LANES, (c + 1) * LANES)
        rows = slice(s * SW_Q, (s + 1) * SW_Q)
        acc = (_dot(vt_ref[kp, :, pl.ds(start[s], SW_SPAN)], _prob(s_scr[i % 2, 0:SW_SPAN, :], m))
               + _dot(vt_ref[kp, :, SEQ:S_ALL], _prob(s_scr[i % 2, SW_SPAN:, :], m)))
        l = acc[LANES:LANES + 1] + jnp.exp2(sink_ref[c] - m)
        o = _pair_out(acc, l, SW_Q, *_gqa_halves(c))
        o_ref[0, rows, cols] = (o * sz_ref[0, rows, cols]).astype(bf16)

    _pipelined(SW_SUB * N_CHUNK_B, scores, values)


def _swa_mask():
    i = np.arange(SW_Q)
    j = np.arange(SW_SPAN)
    out = []
    for d0 in (0, SW_Q, 2 * SW_Q):
        ok = np.abs(d0 + i[None, :] - j[:, None]) <= SW_WINDOW
        m = np.where(ok, 0.0, NEG_INF).astype(np.float32)
        out.append(np.concatenate([m, m], axis=1))
    return jnp.asarray(np.stack(out))


def _sink_cols(sink, n):
    return jnp.repeat(sink.astype(f32).reshape(N_CHUNK_B, 2) * LOG2E, n, axis=1)[:, None, :]


def _swa(qk, vt, sink, sz):
    tq = SW_SUB * SW_Q
    return pl.pallas_call(
        _swa_kernel,
        grid=(BATCH, SW_BLOCKS // SW_SUB),
        in_specs=[
            pl.BlockSpec((1, tq, D_B), lambda b, n: (b, n, 0)),
            pl.BlockSpec((1, S_ALL, KV_B), lambda b, n: (b, 0, D_B // KV_B)),
            pl.BlockSpec((HKV_B // 2, VT_ROWS, S_ALL), lambda b, n: (H_A // HKV_B, 0, b)),
            pl.BlockSpec((3, SW_SPAN, 2 * SW_Q), lambda b, n: (0, 0, 0)),
            pl.BlockSpec((N_CHUNK_B, 1, 2 * SW_Q), lambda b, n: (0, 0, 0)),
            pl.BlockSpec((1, tq, D_B), lambda b, n: (b, n, 0)),
        ],
        out_specs=pl.BlockSpec((1, tq, D_B), lambda b, n: (b, n, 0)),
        out_shape=jax.ShapeDtypeStruct((BATCH, SEQ, D_B), bf16),
        scratch_shapes=[pltpu.VMEM((2, SW_SPAN + CTX_LEN, 2 * SW_Q), f32)],
        compiler_params=_cparams("arbitrary", "arbitrary"),
        name="swa",
    )(qk, qk, vt, _swa_mask(), _sink_cols(sink, SW_Q), sz)


MLA_TQ = 2048
MLA_SUB = 256
MLA_KCHUNK = 256
MLA_SCALE = float(QK_C) ** -0.5


def _mla_kernel(q_ref, k_ref, vt_ref, sz_ref, o_ref, s_scr):
    chunks = [slice(c * MLA_KCHUNK, (c + 1) * MLA_KCHUNK) for c in range(S_ALL // MLA_KCHUNK)]

    def scores(j):
        q = q_ref[0, j * MLA_SUB:(j + 1) * MLA_SUB, :]
        m = None
        for ck in chunks:
            st = _dot_nt(k_ref[0, ck, :], q)
            s_scr[j % 2, ck, :] = st
            mc = _col_max(st)
            m = mc if m is None else jnp.maximum(m, mc)
        return m

    def values(j, m):
        acc = None
        for ck in chunks:
            d = _dot(vt_ref[0, :, ck], _prob(s_scr[j % 2, ck, :], m))
            acc = d if acc is None else acc + d
        o = acc[:V_C] / acc[V_C:V_C + 1]
        rows = slice(j * MLA_SUB, (j + 1) * MLA_SUB)
        o_ref[0, rows, :] = (o.T * sz_ref[0, rows, :]).astype(bf16)

    _pipelined(MLA_TQ // MLA_SUB, scores, values)


def _mla(qc, kc, vt, sz):
    return pl.pallas_call(
        _mla_kernel,
        grid=(BATCH, H_C, SEQ // MLA_TQ),
        in_specs=[
            pl.BlockSpec((1, MLA_TQ, QC_PAD), lambda b, h, i: (b, i, h)),
            pl.BlockSpec((1, S_ALL, QC_PAD), lambda b, h, i: (b, 0, h)),
            pl.BlockSpec((1, VT_ROWS, S_ALL), lambda b, h, i: (h, 0, b)),
            pl.BlockSpec((1, MLA_TQ, V_C), lambda b, h, i: (b, i, h)),
        ],
        out_specs=pl.BlockSpec((1, MLA_TQ, V_C), lambda b, h, i: (b, i, h)),
        out_shape=jax.ShapeDtypeStruct((BATCH, SEQ, D_C), bf16),
        scratch_shapes=[pltpu.VMEM((2, S_ALL, MLA_SUB), f32)],
        compiler_params=_cparams("arbitrary", "arbitrary", "arbitrary"),
        name="mla",
    )(qc, kc, vt, sz)


def _ctx_kernel(qa_ref, ka_ref, qb_ref, kb_ref, vtab_ref, qc_ref, kc_ref, vtc_ref,
                sink_ref, sza_ref, szb_ref, szc_ref, oa_ref, ob_ref, oc_ref):
    n = CTX_LEN

    def item_a(p):
        cols = slice(p * LANES, (p + 1) * LANES)

        def values(s):
            acc = _dot(vtab_ref[p], _prob(s, _col_max(s)))
            o = _pair_out(acc, acc[LANES:LANES + 1], n, 0, 1)
            oa_ref[0, :, cols] = (o * sza_ref[0, :, cols]).astype(bf16)

        return lambda: _dot_nt(ka_ref[0, :, cols], _pair_q(qa_ref[0, :, cols].astype(f32), 0, 1)), values

    def item_b(c):
        kp = c // G_B
        cols = slice(c * LANES, (c + 1) * LANES)

        def values(s):
            m = jnp.maximum(_col_max(s), sink_ref[c])
            acc = _dot(vtab_ref[H_A // 2 + kp], _prob(s, m))
            l = acc[LANES:LANES + 1] + jnp.exp2(sink_ref[c] - m)
            o = _pair_out(acc, l, n, *_gqa_halves(c))
            ob_ref[0, :, cols] = (o * szb_ref[0, :, cols]).astype(bf16)

        return lambda: _dot_nt(kb_ref[0, :, kp * LANES:(kp + 1) * LANES],
                               _pair_q(qb_ref[0, :, cols].astype(f32), *_gqa_halves(c))), values

    def item_c(h):
        qcols = slice(h * QC_PAD, (h + 1) * QC_PAD)
        vcols = slice(h * V_C, (h + 1) * V_C)

        def values(s):
            acc = _dot(vtc_ref[h], _prob(s, _col_max(s)))
            o = acc[:V_C] / acc[V_C:V_C + 1]
            oc_ref[0, :, vcols] = (o.T * szc_ref[0, :, vcols]).astype(bf16)

        return lambda: _dot_nt(kc_ref[0, :, qcols], qc_ref[0, :, qcols]), values

    _run_items([item_a(p) for p in range(H_A // 2)] + [item_b(c) for c in range(N_CHUNK_B)]
               + [item_c(h) for h in range(H_C)])


def _ctx_attention(qka, qkb, vt_ab, qc, kc, vt_c, sink, sza, szb, szc):
    cb = SEQ // CTX_LEN

    def spec(width, col):
        return pl.BlockSpec((1, CTX_LEN, width), lambda b: (b, cb, col))

    def vt_spec(tiles):
        return pl.BlockSpec((tiles, VT_ROWS, CTX_LEN), lambda b: (0, 0, b * (S_ALL // CTX_LEN) + cb))

    return pl.pallas_call(
        _ctx_kernel,
        grid=(BATCH,),
        in_specs=[
            spec(D_A, 0), spec(D_A, 1),
            spec(D_B, 0), spec(KV_B, D_B // KV_B),
            vt_spec((H_A + HKV_B) // 2),
            spec(H_C * QC_PAD, 0), spec(H_C * QC_PAD, 0),
            vt_spec(H_C),
            pl.BlockSpec((N_CHUNK_B, 1, 2 * CTX_LEN), lambda b: (0, 0, 0)),
            spec(D_A, 0), spec(D_B, 0), spec(D_C, 0),
        ],
        out_specs=[
            pl.BlockSpec((1, CTX_LEN, D_A), lambda b: (b, 0, 0)),
            pl.BlockSpec((1, CTX_LEN, D_B), lambda b: (b, 0, 0)),
            pl.BlockSpec((1, CTX_LEN, D_C), lambda b: (b, 0, 0)),
        ],
        out_shape=[
            jax.ShapeDtypeStruct((BATCH, CTX_LEN, D_A), bf16),
            jax.ShapeDtypeStruct((BATCH, CTX_LEN, D_B), bf16),
            jax.ShapeDtypeStruct((BATCH, CTX_LEN, D_C), bf16),
        ],
        compiler_params=_cparams("arbitrary"),
        name="ctx_attention",
    )(qka, qka, qkb, qkb, vt_ab, qc, kc, vt_c, _sink_cols(sink, CTX_LEN), sza, szb, szc)


OUT_TM = 512


def _out_kernel(next_norm, x_ref, ga_ref, gb_ref, gc_ref, w_ref, mod_ref, *rest):
    g = jnp.concatenate([ga_ref[0], gb_ref[0], gc_ref[0]], axis=1)
    x_new = x_ref[0] + mod_ref[0, :, 2 * D_MODEL:] * _dot(g, w_ref[0])
    if next_norm:
        nw_ref, modn_ref = rest[0], rest[1]
        o_ref, h_ref = rest[-2], rest[-1]
        o_ref[0] = x_new
        h_ref[0] = _modulated_norm(x_new, nw_ref[...], modn_ref[0]).astype(bf16)
    else:
        rest[-1][0] = x_new


def _out_proj(x, ga, gb, gc, w_out_bf, layer, mod3, is_ctx, next_norm=None):
    t = x.shape[1]
    tm = min(t, OUT_TM)
    mod_row = (lambda b: BATCH) if is_ctx else (lambda b: b)
    row_block0 = SEQ // tm if is_ctx else 0
    in_specs = [
        pl.BlockSpec((1, tm, D_MODEL), lambda b, i: (b, i, 0)),
        pl.BlockSpec((1, tm, D_A), lambda b, i: (b, i, 0)),
        pl.BlockSpec((1, tm, D_B), lambda b, i: (b, i, 0)),
        pl.BlockSpec((1, tm, D_C), lambda b, i: (b, i, 0)),
        pl.BlockSpec((1, D_MIX, D_MODEL), lambda b, i: (layer, 0, 0), pipeline_mode=pl.Buffered(1)),
        pl.BlockSpec((1, 1, 3 * D_MODEL), lambda b, i: (mod_row(b), 0, 0)),
    ]
    args = [x, ga, gb, gc, w_out_bf, mod3]
    out_specs = [pl.BlockSpec((1, tm, D_MODEL), lambda b, i: (b, i, 0))]
    out_shape = [jax.ShapeDtypeStruct(x.shape, f32)]
    aliases = {}
    if next_norm is not None:
        norm_w, mod3_next, h_buf = next_norm
        in_specs += [pl.BlockSpec((1, D_MODEL), lambda b, i: (0, 0)),
                     pl.BlockSpec((1, 1, 3 * D_MODEL), lambda b, i: (mod_row(b), 0, 0))]
        args += [norm_w.reshape(1, D_MODEL), mod3_next]
        if h_buf is not None:
            in_specs.append(pl.BlockSpec(memory_space=pl.ANY))
            args.append(h_buf)
            aliases = {len(args) - 1: 1}
        out_specs.append(pl.BlockSpec((1, tm, D_MODEL), lambda b, i: (b, row_block0 + i, 0)))
        out_shape.append(jax.ShapeDtypeStruct((BATCH, S_ALL, D_MODEL), bf16))
    return pl.pallas_call(
        functools.partial(_out_kernel, next_norm is not None),
        grid=(BATCH, t // tm),
        in_specs=in_specs,
        out_specs=out_specs,
        out_shape=out_shape,
        input_output_aliases=aliases,
        compiler_params=_cparams("arbitrary", "arbitrary"),
        name="out_proj_ctx" if is_ctx else "out_proj",
    )(*args)


def _rope_tables():
    t = jnp.arange(SEQ)
    row = (t // GRID_W).astype(f32)
    col = (t % GRID_W).astype(f32)
    n_freq = ROPE_DIM // 4
    inv = ROPE_BASE ** (-jnp.arange(n_freq, dtype=f32) / n_freq)
    ar = row[:, None] * inv
    ac = col[:, None] * inv
    ang = jnp.concatenate([ar, ar, ac, ac], axis=-1)
    cos = jnp.cos(ang).astype(f32)
    sin = jnp.sin(ang).astype(f32)
    sign = jnp.asarray(np.where((np.arange(ROPE_DIM) % 32) < 16, -1.0, 1.0), dtype=f32)
    sin = sin * sign
    cos = jnp.concatenate([cos, jnp.ones((CTX_LEN, ROPE_DIM), f32)], axis=0)
    sin = jnp.concatenate([sin, jnp.zeros((CTX_LEN, ROPE_DIM), f32)], axis=0)
    pair = (jnp.concatenate([cos, cos], axis=1), jnp.concatenate([sin, sin], axis=1))
    single = (jnp.concatenate([cos, jnp.ones_like(cos)], axis=1), jnp.concatenate([sin, jnp.zeros_like(sin)], axis=1))
    return pair, single


def _pad_heads(w, width, padded):
    lead = w.shape[:-1]
    w = w.reshape(lead + (H_C, width))
    w = jnp.pad(w, [(0, 0)] * len(lead) + [(0, 0), (0, padded - width)])
    return w.reshape(lead + (H_C * padded,))


def kernel(x, c, ctx, c_ctx, norm_w, w_ada, b_ada, w_in, qn_a, kn_a, rpb_a, qn_b, kn_b, sink_b,
           qa_norm, kva_norm, w_qb, w_kvb, qn_c, kn_c, w_out):
    (cos2, sin2), (cos1, sin1) = _rope_tables()
    cvec = jnp.concatenate([c, c_ctx[None, :], jnp.zeros((MOD_ROWS - BATCH - 1, D_MODEL), f32)], axis=0)
    mod = _modulation(cvec, w_ada, b_ada)
    q_scale_ab = HD ** -0.5 * LOG2E
    q_scale_c = MLA_SCALE * LOG2E
    w_out_bf = w_out.astype(bf16)
    na_bias = _natten_bias(rpb_a)

    o_qa, o_ka, o_va, o_qb, o_kb, o_vb, o_cq, o_ckv, o_kpe, o_z = np.cumsum(
        (0, D_A, D_A, D_A, D_B, KV_B, KV_B, Q_LORA, KV_LORA, ROPE_DIM)).tolist()

    xc = ctx
    mod3_all = [mod[l].reshape(MOD_ROWS, 1, 3 * D_MODEL) for l in range(DEPTH)]
    h3 = _hnorm(x, xc, norm_w[0], mod3_all[0])
    for l in range(DEPTH):
        last = l == DEPTH - 1
        mod3 = mod3_all[l]
        h = h3.reshape(R_ALL, D_MODEL)
        wl = w_in[l]

        gain_hn = jnp.concatenate([jnp.tile(qn_a[l] * q_scale_ab, H_A), jnp.tile(kn_a[l], H_A),
                                   jnp.tile(qn_b[l] * q_scale_ab, H_B), jnp.tile(kn_b[l], HKV_B)])
        w_hn = jnp.concatenate([wl[:, o_qa:o_va], wl[:, o_qb:o_vb]], axis=1).astype(bf16)
        w_lora = jnp.pad(wl[:, o_cq:o_z], ((0, 0), (0, LANES - ROPE_DIM))).astype(bf16)
        wt_v = jnp.concatenate([wl[:, o_va:o_qb], wl[:, o_vb:o_cq]], axis=1).T.astype(bf16)
        qka, qkb, cqn, ckvn, kpe, vt_ab = _proj_main(h, w_hn, w_lora, wt_v, gain_hn, qa_norm[l], kva_norm[l],
                                                      cos2, sin2)
        sza, szb, szc = _proj_gate(h, wl[:, o_z:].astype(bf16))

        wq = _pad_heads(w_qb[l], QK_C, QC_PAD).astype(bf16)
        wkv = w_kvb[l].reshape(KV_LORA, H_C, NOPE_C + V_C)
        wkn = wkv[:, :, :NOPE_C].reshape(KV_LORA, H_C * NOPE_C).astype(bf16)
        wvt = wkv[:, :, NOPE_C:].reshape(KV_LORA, H_C * V_C).T.astype(bf16)
        gq = _pad_heads(jnp.tile(qn_c[l] * q_scale_c, H_C), QK_C, QC_PAD).reshape(1, H_C * QC_PAD)
        gk = jnp.pad(kn_c[l], (0, QC_PAD - QK_C)).reshape(1, QC_PAD)
        qc, kc, vt_c = _mla_up(cqn, ckvn, kpe, wq, wkn, wvt, gq, gk, cos1, sin1)

        def b3(a):
            return a.reshape(BATCH, S_ALL, a.shape[-1])

        qka, qkb, qc, kc, sza, szb, szc = map(b3, (qka, qkb, qc, kc, sza, szb, szc))

        ga = _natten(qka, vt_ab, na_bias, l, sza)
        gb = _swa(qkb, vt_ab, sink_b[l], szb)
        gc = _mla(qc, kc, vt_c, szc)
        if last:
            (x,) = _out_proj(x, ga, gb, gc, w_out_bf, l, mod3, False)
        else:
            ga_c, gb_c, gc_c = _ctx_attention(qka, qkb, vt_ab, qc, kc, vt_c, sink_b[l], sza, szb, szc)
            x, h3 = _out_proj(x, ga, gb, gc, w_out_bf, l, mod3, False,
                              next_norm=(norm_w[l + 1], mod3_all[l + 1], None))
            xc, h3 = _out_proj(xc, ga_c, gb_c, gc_c, w_out_bf, l, mod3, True,
                               next_norm=(norm_w[l + 1], mod3_all[l + 1], h3))
    return x
```

```python
import functools
import math

import numpy as np
import jax
import jax.numpy as jnp
from jax import lax
from jax.experimental import pallas as pl
from jax.experimental.pallas import tpu as pltpu

D_MODEL = 2048
BATCH = 8
SEQ = 2048
DEPTH = 2
GRID_W = 64
CTX_LEN = 256
HD = 64
H_A = 8
D_A = H_A * HD
H_B = 12
HKV_B = 4
D_B = H_B * HD
KV_B = HKV_B * HD
H_C = 6
NOPE_C = 128
ROPE_DIM = 64
QK_C = NOPE_C + ROPE_DIM
V_C = 128
D_C = H_C * V_C
Q_LORA = 768
KV_LORA = 512
D_MIX = D_A + D_B + D_C
NA_ROWS = 8
NA_COLS = 16
SW_WINDOW = 128
ROPE_BASE = 10000.0
EPS = 1e-6
NEG_INF = -1e30

S_ALL = SEQ + CTX_LEN
R_ALL = BATCH * S_ALL
GRID_ROWS = SEQ // GRID_W
LANES = 128
MXU_DIM = 256
BF16_ROWS = 16
QC_PAD = 256
VT_ROWS = LANES + BF16_ROWS
MOD_ROWS = 16
VMEM_LIMIT = 48 * 1024 * 1024
LOG2E = math.log2(math.e)

bf16 = jnp.bfloat16
f32 = jnp.float32


def _cparams(*sem):
    return pltpu.CompilerParams(dimension_semantics=sem, vmem_limit_bytes=VMEM_LIMIT)


def _silu(v):
    return v / (1.0 + jnp.exp(-v))


def _lane_iota(shape):
    return lax.broadcasted_iota(jnp.int32, shape, len(shape) - 1)


def _rope_rotate(xn, cos, sin_signed):
    lane = _lane_iota(xn.shape)
    take_next = (lane % 32) < 16
    rot = jnp.where(take_next, pltpu.roll(xn, LANES - 16, 1), pltpu.roll(xn, 16, 1))
    return xn * cos + rot * sin_signed


def _dot_nt(a, b):
    return lax.dot_general(a, b, (((1,), (1,)), ((), ())), preferred_element_type=f32)


def _dot(a, b):
    return jnp.dot(a, b, preferred_element_type=f32)


def _mod_kernel(c_ref, w_ref, b_ref, o_ref):
    sc = _silu(c_ref[...]).astype(bf16)
    o_ref[0] = _dot(sc, w_ref[0].astype(bf16)) + b_ref[0]


def _modulation(cvec, w_ada, b_ada):
    tn = 1024
    n = 3 * D_MODEL
    return pl.pallas_call(
        _mod_kernel,
        grid=(DEPTH, n // tn),
        in_specs=[
            pl.BlockSpec((MOD_ROWS, D_MODEL), lambda l, j: (0, 0)),
            pl.BlockSpec((1, D_MODEL, tn), lambda l, j: (l, 0, j)),
            pl.BlockSpec((1, 1, tn), lambda l, j: (l, 0, j)),
        ],
        out_specs=pl.BlockSpec((1, MOD_ROWS, tn), lambda l, j: (l, 0, j)),
        out_shape=jax.ShapeDtypeStruct((DEPTH, MOD_ROWS, n), f32),
        compiler_params=_cparams("arbitrary", "arbitrary"),
        name="modulation",
    )(cvec, w_ada, b_ada.reshape(DEPTH, 1, n))


H_ROWS = 256
N_LAT_BLOCKS = SEQ // H_ROWS


def _modulated_norm(x, nw, mod):
    ms = jnp.mean(x * x, axis=-1, keepdims=True)
    y = x * lax.rsqrt(ms + EPS) * nw
    return y * (1.0 + mod[:, D_MODEL:2 * D_MODEL]) + mod[:, 0:D_MODEL]


def _hnorm_kernel(x_ref, ctx_ref, nw_ref, mod_ref, h_ref):
    s = pl.program_id(1)

    def body(src):
        h_ref[0] = _modulated_norm(src[0], nw_ref[...], mod_ref[0]).astype(bf16)

    @pl.when(s < N_LAT_BLOCKS)
    def _():
        body(x_ref)

    @pl.when(s >= N_LAT_BLOCKS)
    def _():
        body(ctx_ref)


def _hnorm(x, xc, norm_w, mod3):
    return pl.pallas_call(
        _hnorm_kernel,
        grid=(BATCH, S_ALL // H_ROWS),
        in_specs=[
            pl.BlockSpec((1, H_ROWS, D_MODEL), lambda b, s: (b, jnp.minimum(s, N_LAT_BLOCKS - 1), 0)),
            pl.BlockSpec((1, H_ROWS, D_MODEL), lambda b, s: (b, 0, 0)),
            pl.BlockSpec((1, D_MODEL), lambda b, s: (0, 0)),
            pl.BlockSpec((1, 1, 3 * D_MODEL), lambda b, s: (jnp.where(s < N_LAT_BLOCKS, b, BATCH), 0, 0)),
        ],
        out_specs=pl.BlockSpec((1, H_ROWS, D_MODEL), lambda b, s: (b, s, 0)),
        out_shape=jax.ShapeDtypeStruct((BATCH, S_ALL, D_MODEL), bf16),
        compiler_params=_cparams("arbitrary", "arbitrary"),
        name="hnorm",
    )(x, xc, norm_w.reshape(1, D_MODEL), mod3)


PROJ_TM = 768


def _group_mean_sq(u):
    sq = u * u
    low = _lane_iota((u.shape[0], LANES)) < HD
    parts = []
    for c in range(0, u.shape[1], LANES):
        t = sq[:, c:c + LANES]
        s_low = jnp.sum(jnp.where(low, t, 0.0), axis=-1, keepdims=True)
        s_high = jnp.sum(jnp.where(low, 0.0, t), axis=-1, keepdims=True)
        parts.append(jnp.where(low, s_low, s_high) * (1.0 / HD))
    return jnp.concatenate(parts, axis=1)


def _run_items(items):
    u = items[0][0]()
    for i, (_, epilogue) in enumerate(items):
        u_next = items[i + 1][0]() if i + 1 < len(items) else None
        epilogue(u)
        u = u_next


def _store_vt_tile(vt_ref, p, ut):
    vt_ref[p, 0:LANES, :] = ut.astype(bf16)
    vt_ref[p, LANES:VT_ROWS, :] = jnp.ones((VT_ROWS - LANES, ut.shape[1]), bf16)


PROJ_TN = 512
N_HN = 2 * D_A + D_B + KV_B
N_LORA = Q_LORA + KV_LORA + LANES
N_VT = D_A + KV_B


def _proj_main_kernel(h_ref, whn_ref, wl_ref, wvt_ref, ghn_ref, cos_ref, sin_ref, gcq_ref, gckv_ref,
                      qka_ref, qkb_ref, cqn_ref, ckvn_ref, kpe_ref, vt_ref):
    def hn_item(k):
        cols = slice(k * PROJ_TN, (k + 1) * PROJ_TN)
        rope = k * PROJ_TN >= 2 * D_A
        o_ref = qkb_ref if rope else qka_ref
        o0 = k * PROJ_TN - (2 * D_A if rope else 0)

        def epilogue(u):
            for c0 in range(0, PROJ_TN, MXU_DIM):
                uc = u[:, c0:c0 + MXU_DIM]
                ms = _group_mean_sq(uc)
                xn = uc * lax.rsqrt(ms + EPS) * ghn_ref[:, k * PROJ_TN + c0:k * PROJ_TN + c0 + MXU_DIM]
                if rope:
                    for c1 in range(0, MXU_DIM, LANES):
                        o_ref[:, o0 + c0 + c1:o0 + c0 + c1 + LANES] = _rope_rotate(
                            xn[:, c1:c1 + LANES], cos_ref[...], sin_ref[...]).astype(bf16)
                else:
                    o_ref[:, o0 + c0:o0 + c0 + MXU_DIM] = xn.astype(bf16)

        return lambda: _dot(h_ref[...], whn_ref[:, cols]), epilogue

    def rowrms_item(c0, width, gain_ref, o_ref):
        def epilogue(u):
            ms = jnp.mean(u * u, axis=-1, keepdims=True)
            o_ref[...] = (u * lax.rsqrt(ms + EPS) * gain_ref[...]).astype(bf16)

        return lambda: _dot(h_ref[...], wl_ref[:, c0:c0 + width]), epilogue

    def kpe_item():
        def epilogue(u):
            kpe_ref[...] = u

        return lambda: _dot(h_ref[...], wl_ref[:, Q_LORA + KV_LORA:N_LORA]), epilogue

    def vt_item(p):
        def epilogue(ut):
            _store_vt_tile(vt_ref, p, ut)

        return lambda: _dot_nt(wvt_ref[p * LANES:(p + 1) * LANES, :], h_ref[...]), epilogue

    _run_items([hn_item(k) for k in range(N_HN // PROJ_TN)]
               + [rowrms_item(0, Q_LORA, gcq_ref, cqn_ref), rowrms_item(Q_LORA, KV_LORA, gckv_ref, ckvn_ref),
                  kpe_item()]
               + [vt_item(p) for p in range(N_VT // LANES)])


def _resident(shape):
    return pl.BlockSpec(shape, lambda i: (0,) * len(shape), pipeline_mode=pl.Buffered(1))


def _proj_main(h2d, whn, wl, wvt, ghn, gcq, gckv, cos, sin):
    tm = PROJ_TM
    nt = S_ALL // tm
    row = lambda i: (i, 0)
    tiles = N_VT // LANES
    return pl.pallas_call(
        _proj_main_kernel,
        grid=(R_ALL // tm,),
        in_specs=[
            pl.BlockSpec((tm, D_MODEL), row),
            _resident((D_MODEL, N_HN)), _resident((D_MODEL, N_LORA)), _resident((N_VT, D_MODEL)),
            _resident((1, N_HN)),
            pl.BlockSpec((tm, LANES), lambda i: (i % nt, 0)),
            pl.BlockSpec((tm, LANES), lambda i: (i % nt, 0)),
            _resident((1, Q_LORA)), _resident((1, KV_LORA)),
        ],
        out_specs=[
            pl.BlockSpec((tm, 2 * D_A), row),
            pl.BlockSpec((tm, D_B + KV_B), row),
            pl.BlockSpec((tm, Q_LORA), row),
            pl.BlockSpec((tm, KV_LORA), row),
            pl.BlockSpec((tm, LANES), row),
            pl.BlockSpec((tiles, VT_ROWS, tm), lambda i: (0, 0, i)),
        ],
        out_shape=[
            jax.ShapeDtypeStruct((R_ALL, 2 * D_A), bf16),
            jax.ShapeDtypeStruct((R_ALL, D_B + KV_B), bf16),
            jax.ShapeDtypeStruct((R_ALL, Q_LORA), bf16),
            jax.ShapeDtypeStruct((R_ALL, KV_LORA), bf16),
            jax.ShapeDtypeStruct((R_ALL, LANES), f32),
            jax.ShapeDtypeStruct((tiles, VT_ROWS, R_ALL), bf16),
        ],
        compiler_params=_cparams("arbitrary"),
        name="proj_main",
    )(h2d, whn, wl, wvt, ghn.reshape(1, N_HN), cos, sin,
      gcq.reshape(1, Q_LORA), gckv.reshape(1, KV_LORA))


def _proj_gate_kernel(h_ref, wz_ref, sza_ref, szb_ref, szc_ref):
    outs = ((sza_ref, 0, D_A), (szb_ref, D_A, D_B), (szc_ref, D_A + D_B, D_C))

    def item(k):
        c0 = k * PROJ_TN

        def epilogue(u):
            sz = _silu(u)
            for o_ref, start, width in outs:
                lo, hi = max(c0, start), min(c0 + PROJ_TN, start + width)
                if lo < hi:
                    o_ref[:, lo - start:hi - start] = sz[:, lo - c0:hi - c0]

        return lambda: _dot(h_ref[...], wz_ref[:, c0:c0 + PROJ_TN]), epilogue

    _run_items([item(k) for k in range(D_MIX // PROJ_TN)])


def _proj_gate(h2d, wz):
    tm = PROJ_TM
    row = lambda i: (i, 0)
    return pl.pallas_call(
        _proj_gate_kernel,
        grid=(R_ALL // tm,),
        in_specs=[pl.BlockSpec((tm, D_MODEL), row), _resident((D_MODEL, D_MIX))],
        out_specs=[pl.BlockSpec((tm, D_A), row), pl.BlockSpec((tm, D_B), row), pl.BlockSpec((tm, D_C), row)],
        out_shape=[jax.ShapeDtypeStruct((R_ALL, D_A), f32), jax.ShapeDtypeStruct((R_ALL, D_B), f32),
                   jax.ShapeDtypeStruct((R_ALL, D_C), f32)],
        compiler_params=_cparams("arbitrary"),
        name="proj_gate",
    )(h2d, wz)


def _mla_up_kernel(cq_ref, ckv_ref, kpe_ref, wq_ref, wkn_ref, wvt_ref, gq_ref, gk_ref,
                   cos_ref, sin_ref, qc_ref, kc_ref, vt_ref):
    cos = cos_ref[...]
    sin = sin_ref[...]
    kpe = kpe_ref[...]
    ss_pe = jnp.sum(kpe * kpe, axis=-1, keepdims=True)
    k_rot = _rope_rotate(kpe * gk_ref[:, LANES:], cos, sin)

    def q_head(h):
        cols = slice(h * QC_PAD, (h + 1) * QC_PAD)
        return lambda: _dot(cq_ref[...], wq_ref[:, cols]), functools.partial(q_epilogue, h)

    def q_epilogue(h, u):
        ms = jnp.sum(u * u, axis=-1, keepdims=True) * (1.0 / QK_C)
        xn = u * lax.rsqrt(ms + EPS) * gq_ref[:, h * QC_PAD:(h + 1) * QC_PAD]
        qc_ref[:, h * QC_PAD:h * QC_PAD + LANES] = xn[:, :LANES].astype(bf16)
        qc_ref[:, h * QC_PAD + LANES:(h + 1) * QC_PAD] = _rope_rotate(xn[:, LANES:], cos, sin).astype(bf16)

    def k_head(h):
        cols = slice(h * NOPE_C, (h + 1) * NOPE_C)
        return lambda: _dot(ckv_ref[...], wkn_ref[:, cols]), functools.partial(k_epilogue, h)

    def k_epilogue(h, u):
        ms = (jnp.sum(u * u, axis=-1, keepdims=True) + ss_pe) * (1.0 / QK_C)
        r = lax.rsqrt(ms + EPS)
        kc_ref[:, h * QC_PAD:h * QC_PAD + LANES] = (u * r * gk_ref[:, 0:LANES]).astype(bf16)
        kc_ref[:, h * QC_PAD + LANES:(h + 1) * QC_PAD] = (k_rot * r).astype(bf16)

    def v_tile(p):
        rows = slice(p * LANES, (p + 1) * LANES)

        def epilogue(ut):
            _store_vt_tile(vt_ref, p, ut)

        return lambda: _dot_nt(wvt_ref[rows, :], ckv_ref[...]), epilogue

    _run_items([q_head(h) for h in range(H_C)] + [k_head(h) for h in range(H_C)] + [v_tile(p) for p in range(H_C)])


def _mla_up(cqn, ckvn, kpe, wq, wkn, wvt, gq, gk, cos, sin):
    tm = PROJ_TM
    nt = S_ALL // tm
    row = lambda i: (i, 0)
    fixed = lambda i: (0, 0)
    return pl.pallas_call(
        _mla_up_kernel,
        grid=(R_ALL // tm,),
        in_specs=[
            pl.BlockSpec((tm, Q_LORA), row),
            pl.BlockSpec((tm, KV_LORA), row),
            pl.BlockSpec((tm, LANES), row),
            pl.BlockSpec((Q_LORA, H_C * QC_PAD), fixed),
            pl.BlockSpec((KV_LORA, H_C * NOPE_C), fixed),
            pl.BlockSpec((H_C * V_C, KV_LORA), fixed),
            pl.BlockSpec((1, H_C * QC_PAD), fixed),
            pl.BlockSpec((1, QC_PAD), fixed),
            pl.BlockSpec((tm, LANES), lambda i: (i % nt, 0)),
            pl.BlockSpec((tm, LANES), lambda i: (i % nt, 0)),
        ],
        out_specs=[
            pl.BlockSpec((tm, H_C * QC_PAD), row),
            pl.BlockSpec((tm, H_C * QC_PAD), row),
            pl.BlockSpec((H_C, VT_ROWS, tm), lambda i: (0, 0, i)),
        ],
        out_shape=[
            jax.ShapeDtypeStruct((R_ALL, H_C * QC_PAD), bf16),
            jax.ShapeDtypeStruct((R_ALL, H_C * QC_PAD), bf16),
            jax.ShapeDtypeStruct((H_C, VT_ROWS, R_ALL), bf16),
        ],
        compiler_params=_cparams("arbitrary"),
        name="mla_up",
    )(cqn, ckvn, kpe, wq, wkn, wvt, gq, gk, cos, sin)


def _pipelined(n, score_fn, value_fn):
    m = score_fn(0)
    for j in range(n):
        m_next = score_fn(j + 1) if j + 1 < n else None
        value_fn(j, m)
        m = m_next


def _col_max(s):
    return jnp.max(s, axis=0, keepdims=True)


def _prob(s, m):
    return jnp.exp2((s - m).astype(bf16))


def _pair_q(qv, half_even, half_odd):
    low = _lane_iota(qv.shape) < HD
    zero = jnp.zeros_like(qv)
    qa = jnp.where(low, qv, zero) if half_even == 0 else jnp.where(low, zero, pltpu.roll(qv, HD, 1))
    qb = jnp.where(low, zero, qv) if half_odd == 1 else jnp.where(low, pltpu.roll(qv, HD, 1), zero)
    return jnp.concatenate([qa, qb], axis=0).astype(bf16)


def _pair_out(acc, l, n, half_even, half_odd):
    o = acc[:LANES] / l
    t = jnp.concatenate([o[half_even * HD:(half_even + 1) * HD, 0:n],
                         o[half_odd * HD:(half_odd + 1) * HD, n:2 * n]], axis=0)
    return t.T


NA_QROWS = 2
NA_Q = NA_QROWS * GRID_W
NA_KROWS = NA_ROWS + NA_QROWS
NA_KEYS = NA_KROWS * GRID_W
NA_VARIANTS = ((0, 0, 0, -1), (0, -2, 0, -3), (0, -4, 1, -5), (0, -4, 0, -5), (0, -6, 0, -7))


def _na_window_row(j):
    return jnp.clip(NA_QROWS * j - NA_ROWS // 2, 0, GRID_ROWS - NA_ROWS)


NA_SUB = 2
NA_BLOCKS = GRID_ROWS // NA_QROWS


def _natten_kernel(q_ref, k_ref, vt_ref, bias0_ref, bias1_ref, sz_ref, o_ref, s_scr):
    step = pl.program_id(1)
    bias_refs = (bias0_ref, bias1_ref)
    k0 = [pl.multiple_of(_na_window_row(NA_SUB * step + s) * GRID_W, LANES) for s in range(NA_SUB)]
    n_pair = H_A // 2

    def scores(i):
        s, p = divmod(i, n_pair)
        cols = slice(p * LANES, (p + 1) * LANES)
        q2 = _pair_q(q_ref[0, s * NA_Q:(s + 1) * NA_Q, cols].astype(f32), 0, 1)
        s_loc = _dot_nt(k_ref[0, pl.ds(k0[s], NA_KEYS), cols], q2) + bias_refs[s][0, 0, p]
        s_ctx = _dot_nt(k_ref[0, SEQ:S_ALL, cols], q2)
        s_scr[i % 2, 0:NA_KEYS, :] = s_loc
        s_scr[i % 2, NA_KEYS:, :] = s_ctx
        return jnp.maximum(_col_max(s_loc), _col_max(s_ctx))

    def values(i, m):
        s, p = divmod(i, n_pair)
        cols = slice(p * LANES, (p + 1) * LANES)
        rows = slice(s * NA_Q, (s + 1) * NA_Q)
        acc = (_dot(vt_ref[p, :, pl.ds(k0[s], NA_KEYS)], _prob(s_scr[i % 2, 0:NA_KEYS, :], m))
               + _dot(vt_ref[p, :, SEQ:S_ALL], _prob(s_scr[i % 2, NA_KEYS:, :], m)))
        o = _pair_out(acc, acc[LANES:LANES + 1], NA_Q, 0, 1)
        o_ref[0, rows, cols] = (o * sz_ref[0, rows, cols]).astype(bf16)

    _pipelined(NA_SUB * n_pair, scores, values)


def _natten(qk, vt, bias, layer, sz):
    def variant(s):
        def index(b, step):
            j = NA_SUB * step + s
            v = jnp.where(j <= 1, j, jnp.where(j >= NA_BLOCKS - 2, j - (NA_BLOCKS - 5), 2))
            return (layer, v, 0, 0, 0)
        return index

    tq = NA_SUB * NA_Q
    bias_block = (1, 1, H_A // 2, NA_KEYS, 2 * NA_Q)
    return pl.pallas_call(
        _natten_kernel,
        grid=(BATCH, NA_BLOCKS // NA_SUB),
        in_specs=[
            pl.BlockSpec((1, tq, D_A), lambda b, j: (b, j, 0)),
            pl.BlockSpec((1, S_ALL, D_A), lambda b, j: (b, 0, 1)),
            pl.BlockSpec((H_A // 2, VT_ROWS, S_ALL), lambda b, j: (0, 0, b)),
            pl.BlockSpec(bias_block, variant(0)),
            pl.BlockSpec(bias_block, variant(1)),
            pl.BlockSpec((1, tq, D_A), lambda b, j: (b, j, 0)),
        ],
        out_specs=pl.BlockSpec((1, tq, D_A), lambda b, j: (b, j, 0)),
        out_shape=jax.ShapeDtypeStruct((BATCH, SEQ, D_A), bf16),
        scratch_shapes=[pltpu.VMEM((2, NA_KEYS + CTX_LEN, 2 * NA_Q), f32)],
        compiler_params=_cparams("arbitrary", "arbitrary"),
        name="natten",
    )(qk, qk, vt, bias, bias, sz)


def _natten_bias(rpb):
    n_dc = 2 * NA_COLS - 1
    n_dr = NA_KROWS + NA_ROWS - 1
    c = np.arange(LANES) % GRID_W
    kc = np.arange(GRID_W)
    qstart = np.clip(c - NA_COLS // 2, 0, GRID_W - NA_COLS)
    col_ok = (kc[:, None] >= qstart[None, :]) & (kc[:, None] < qstart[None, :] + NA_COLS)
    dc = np.clip(kc[:, None] - c[None, :], -(NA_COLS - 1), NA_COLS - 1) + NA_COLS - 1
    onehot = jnp.asarray((dc[None] == np.arange(n_dc)[:, None, None]).astype(np.float32))
    exp = jnp.einsum("lhrd,dkc->lhrkc", rpb.astype(f32) * LOG2E, onehot, precision=lax.Precision.HIGHEST)
    exp = jnp.pad(exp, ((0, 0), (0, 0), (0, n_dr - exp.shape[2]), (0, 0), (0, 0)))
    col_mask = jnp.asarray(np.where(col_ok, 0.0, NEG_INF).astype(np.float32))
    n_var = len(NA_VARIANTS)
    return pl.pallas_call(
        _natten_bias_kernel,
        grid=(DEPTH, n_var, H_A // 2),
        in_specs=[
            pl.BlockSpec((1, 2, n_dr, GRID_W, LANES), lambda l, v, p: (l, p, 0, 0, 0)),
            pl.BlockSpec((GRID_W, LANES), lambda l, v, p: (0, 0)),
        ],
        out_specs=pl.BlockSpec((1, 1, 1, NA_KEYS, 2 * NA_Q), lambda l, v, p: (l, v, p, 0, 0)),
        out_shape=jax.ShapeDtypeStruct((DEPTH, n_var, H_A // 2, NA_KEYS, 2 * NA_Q), f32),
        compiler_params=_cparams("arbitrary", "arbitrary", "arbitrary"),
        name="natten_bias",
    )(exp, col_mask)


def _natten_bias_kernel(exp_ref, mask_ref, o_ref):
    v = pl.program_id(1)

    def pick(column):
        out = jnp.int32(0)
        for i, var in enumerate(NA_VARIANTS):
            out = jnp.where(v == i, var[column], out)
        return out

    lo = (pick(0), pick(2))
    off = (pick(1), pick(3))
    low = _lane_iota((GRID_W, LANES)) < GRID_W
    mask = mask_ref[...]
    for kr in range(NA_KROWS):
        for e in range(2):
            halves = []
            for qr in range(NA_QROWS):
                blk = exp_ref[0, e, pl.ds(kr + off[qr] + NA_ROWS - 1, 1)][0] + mask
                valid = (kr >= lo[qr]) & (kr < lo[qr] + NA_ROWS)
                halves.append(jnp.where(valid, blk, NEG_INF))
            o_ref[0, 0, 0, kr * GRID_W:(kr + 1) * GRID_W, e * NA_Q:(e + 1) * NA_Q] = jnp.where(
                low, halves[0], halves[1])


SW_Q = 128
SW_SPAN = SW_Q + 2 * SW_WINDOW
G_B = H_B // HKV_B
N_CHUNK_B = H_B // 2


def _gqa_halves(c):
    return ((2 * c) // G_B) % 2, ((2 * c + 1) // G_B) % 2


SW_SUB = 4
SW_BLOCKS = SEQ // SW_Q


def _swa_kernel(q_ref, k_ref, vt_ref, mask_ref, sink_ref, sz_ref, o_ref, s_scr):
    step = pl.program_id(1)
    blocks = [SW_SUB * step + s for s in range(SW_SUB)]
    start = [pl.multiple_of(jnp.clip((n - 1) * SW_Q, 0, SEQ - SW_SPAN), SW_Q) for n in blocks]
    variant = [jnp.where(n == 0, 0, jnp.where(n == SW_BLOCKS - 1, 2, 1)) for n in blocks]

    def scores(i):
        s, c = divmod(i, N_CHUNK_B)
        kp = c // G_B
        cols = slice(c * LANES, (c + 1) * LANES)
        kcols = slice(kp * LANES, (kp + 1) * LANES)
        q2 = _pair_q(q_ref[0, s * SW_Q:(s + 1) * SW_Q, cols].astype(f32), *_gqa_halves(c))
        s_loc = _dot_nt(k_ref[0, pl.ds(start[s], SW_SPAN), kcols], q2) + mask_ref[variant[s]]
        s_ctx = _dot_nt(k_ref[0, SEQ:S_ALL, kcols], q2)
        s_scr[i % 2, 0:SW_SPAN, :] = s_loc
        s_scr[i % 2, SW_SPAN:, :] = s_ctx
        return jnp.maximum(jnp.maximum(_col_max(s_loc), _col_max(s_ctx)), sink_ref[c])

    def values(i, m):
        s, c = divmod(i, N_CHUNK_B)
        kp = c // G_B
        cols = slice(c * LANES, (c + 1) * LANES)
        rows = slice(s * SW_Q, (s + 1) * SW_Q)
        acc = (_dot(vt_ref[kp, :, pl.ds(start[s], SW_SPAN)], _prob(s_scr[i % 2, 0:SW_SPAN, :], m))
               + _dot(vt_ref[kp, :, SEQ:S_ALL], _prob(s_scr[i % 2, SW_SPAN:, :], m)))
        l = acc[LANES:LANES + 1] + jnp.exp2(sink_ref[c] - m)
        o = _pair_out(acc, l, SW_Q, *_gqa_halves(c))
        o_ref[0, rows, cols] = (o * sz_ref[0, rows, cols]).astype(bf16)

    _pipelined(SW_SUB * N_CHUNK_B, scores, values)


def _swa_mask():
    i = np.arange(SW_Q)
    j = np.arange(SW_SPAN)
    out = []
    for d0 in (0, SW_Q, 2 * SW_Q):
        ok = np.abs(d0 + i[None, :] - j[:, None]) <= SW_WINDOW
        m = np.where(ok, 0.0, NEG_INF).astype(np.float32)
        out.append(np.concatenate([m, m], axis=1))
    return jnp.asarray(np.stack(out))


def _sink_cols(sink, n):
    return jnp.repeat(sink.astype(f32).reshape(N_CHUNK_B, 2) * LOG2E, n, axis=1)[:, None, :]


def _swa(qk, vt, sink, sz):
    tq = SW_SUB * SW_Q
    return pl.pallas_call(
        _swa_kernel,
        grid=(BATCH, SW_BLOCKS // SW_SUB),
        in_specs=[
            pl.BlockSpec((1, tq, D_B), lambda b, n: (b, n, 0)),
            pl.BlockSpec((1, S_ALL, KV_B), lambda b, n: (b, 0, D_B // KV_B)),
            pl.BlockSpec((HKV_B // 2, VT_ROWS, S_ALL), lambda b, n: (H_A // HKV_B, 0, b)),
            pl.BlockSpec((3, SW_SPAN, 2 * SW_Q), lambda b, n: (0, 0, 0)),
            pl.BlockSpec((N_CHUNK_B, 1, 2 * SW_Q), lambda b, n: (0, 0, 0)),
            pl.BlockSpec((1, tq, D_B), lambda b, n: (b, n, 0)),
        ],
        out_specs=pl.BlockSpec((1, tq, D_B), lambda b, n: (b, n, 0)),
        out_shape=jax.ShapeDtypeStruct((BATCH, SEQ, D_B), bf16),
        scratch_shapes=[pltpu.VMEM((2, SW_SPAN + CTX_LEN, 2 * SW_Q), f32)],
        compiler_params=_cparams("arbitrary", "arbitrary"),
        name="swa",
    )(qk, qk, vt, _swa_mask(), _sink_cols(sink, SW_Q), sz)


MLA_TQ = 2048
MLA_SUB = 256
MLA_KCHUNK = 256
MLA_SCALE = float(QK_C) ** -0.5


def _mla_kernel(q_ref, k_ref, vt_ref, sz_ref, o_ref, s_scr):
    chunks = [slice(c * MLA_KCHUNK, (c + 1) * MLA_KCHUNK) for c in range(S_ALL // MLA_KCHUNK)]

    def scores(j):
        q = q_ref[0, j * MLA_SUB:(j + 1) * MLA_SUB, :]
        m = None
        for ck in chunks:
            st = _dot_nt(k_ref[0, ck, :], q)
            s_scr[j % 2, ck, :] = st
            mc = _col_max(st)
            m = mc if m is None else jnp.maximum(m, mc)
        return m

    def values(j, m):
        acc = None
        for ck in chunks:
            d = _dot(vt_ref[0, :, ck], _prob(s_scr[j % 2, ck, :], m))
            acc = d if acc is None else acc + d
        o = acc[:V_C] / acc[V_C:V_C + 1]
        rows = slice(j * MLA_SUB, (j + 1) * MLA_SUB)
        o_ref[0, rows, :] = (o.T * sz_ref[0, rows, :]).astype(bf16)

    _pipelined(MLA_TQ // MLA_SUB, scores, values)


def _mla(qc, kc, vt, sz):
    return pl.pallas_call(
        _mla_kernel,
        grid=(BATCH, H_C, SEQ // MLA_TQ),
        in_specs=[
            pl.BlockSpec((1, MLA_TQ, QC_PAD), lambda b, h, i: (b, i, h)),
            pl.BlockSpec((1, S_ALL, QC_PAD), lambda b, h, i: (b, 0, h)),
            pl.BlockSpec((1, VT_ROWS, S_ALL), lambda b, h, i: (h, 0, b)),
            pl.BlockSpec((1, MLA_TQ, V_C), lambda b, h, i: (b, i, h)),
        ],
        out_specs=pl.BlockSpec((1, MLA_TQ, V_C), lambda b, h, i: (b, i, h)),
        out_shape=jax.ShapeDtypeStruct((BATCH, SEQ, D_C), bf16),
        scratch_shapes=[pltpu.VMEM((2, S_ALL, MLA_SUB), f32)],
        compiler_params=_cparams("arbitrary", "arbitrary", "arbitrary"),
        name="mla",
    )(qc, kc, vt, sz)


def _ctx_kernel(qa_ref, ka_ref, qb_ref, kb_ref, vtab_ref, qc_ref, kc_ref, vtc_ref,
                sink_ref, sza_ref, szb_ref, szc_ref, oa_ref, ob_ref, oc_ref):
    n = CTX_LEN

    def item_a(p):
        cols = slice(p * LANES, (p + 1) * LANES)

        def values(s):
            acc = _dot(vtab_ref[p], _prob(s, _col_max(s)))
            o = _pair_out(acc, acc[LANES:LANES + 1], n, 0, 1)
            oa_ref[0, :, cols] = (o * sza_ref[0, :, cols]).astype(bf16)

        return lambda: _dot_nt(ka_ref[0, :, cols], _pair_q(qa_ref[0, :, cols].astype(f32), 0, 1)), values

    def item_b(c):
        kp = c // G_B
        cols = slice(c * LANES, (c + 1) * LANES)

        def values(s):
            m = jnp.maximum(_col_max(s), sink_ref[c])
            acc = _dot(vtab_ref[H_A // 2 + kp], _prob(s, m))
            l = acc[LANES:LANES + 1] + jnp.exp2(sink_ref[c] - m)
            o = _pair_out(acc, l, n, *_gqa_halves(c))
            ob_ref[0, :, cols] = (o * szb_ref[0, :, cols]).astype(bf16)

        return lambda: _dot_nt(kb_ref[0, :, kp * LANES:(kp + 1) * LANES],
                               _pair_q(qb_ref[0, :, cols].astype(f32), *_gqa_halves(c))), values

    def item_c(h):
        qcols = slice(h * QC_PAD, (h + 1) * QC_PAD)
        vcols = slice(h * V_C, (h + 1) * V_C)

        def values(s):
            acc = _dot(vtc_ref[h], _prob(s, _col_max(s)))
            o = acc[:V_C] / acc[V_C:V_C + 1]
            oc_ref[0, :, vcols] = (o.T * szc_ref[0, :, vcols]).astype(bf16)

        return lambda: _dot_nt(kc_ref[0, :, qcols], qc_ref[0, :, qcols]), values

    _run_items([item_a(p) for p in range(H_A // 2)] + [item_b(c) for c in range(N_CHUNK_B)]
               + [item_c(h) for h in range(H_C)])


def _ctx_attention(qka, qkb, vt_ab, qc, kc, vt_c, sink, sza, szb, szc):
    cb = SEQ // CTX_LEN

    def spec(width, col):
        return pl.BlockSpec((1, CTX_LEN, width), lambda b: (b, cb, col))

    def vt_spec(tiles):
        return pl.BlockSpec((tiles, VT_ROWS, CTX_LEN), lambda b: (0, 0, b * (S_ALL // CTX_LEN) + cb))

    return pl.pallas_call(
        _ctx_kernel,
        grid=(BATCH,),
        in_specs=[
            spec(D_A, 0), spec(D_A, 1),
            spec(D_B, 0), spec(KV_B, D_B // KV_B),
            vt_spec((H_A + HKV_B) // 2),
            spec(H_C * QC_PAD, 0), spec(H_C * QC_PAD, 0),
            vt_spec(H_C),
            pl.BlockSpec((N_CHUNK_B, 1, 2 * CTX_LEN), lambda b: (0, 0, 0)),
            spec(D_A, 0), spec(D_B, 0), spec(D_C, 0),
        ],
        out_specs=[
            pl.BlockSpec((1, CTX_LEN, D_A), lambda b: (b, 0, 0)),
            pl.BlockSpec((1, CTX_LEN, D_B), lambda b: (b, 0, 0)),
            pl.BlockSpec((1, CTX_LEN, D_C), lambda b: (b, 0, 0)),
        ],
        out_shape=[
            jax.ShapeDtypeStruct((BATCH, CTX_LEN, D_A), bf16),
            jax.ShapeDtypeStruct((BATCH, CTX_LEN, D_B), bf16),
            jax.ShapeDtypeStruct((BATCH, CTX_LEN, D_C), bf16),
        ],
        compiler_params=_cparams("arbitrary"),
        name="ctx_attention",
    )(qka, qka, qkb, qkb, vt_ab, qc, kc, vt_c, _sink_cols(sink, CTX_LEN), sza, szb, szc)


OUT_TM = 512


def _out_kernel(next_norm, x_ref, ga_ref, gb_ref, gc_ref, w_ref, mod_ref, *rest):
    g = jnp.concatenate([ga_ref[0], gb_ref[0], gc_ref[0]], axis=1)
    x_new = x_ref[0] + mod_ref[0, :, 2 * D_MODEL:] * _dot(g, w_ref[0])
    if next_norm:
        nw_ref, modn_ref = rest[0], rest[1]
        o_ref, h_ref = rest[-2], rest[-1]
        o_ref[0] = x_new
        h_ref[0] = _modulated_norm(x_new, nw_ref[...], modn_ref[0]).astype(bf16)
    else:
        rest[-1][0] = x_new


def _out_proj(x, ga, gb, gc, w_out_bf, layer, mod3, is_ctx, next_norm=None):
    t = x.shape[1]
    tm = min(t, OUT_TM)
    mod_row = (lambda b: BATCH) if is_ctx else (lambda b: b)
    row_block0 = SEQ // tm if is_ctx else 0
    in_specs = [
        pl.BlockSpec((1, tm, D_MODEL), lambda b, i: (b, i, 0)),
        pl.BlockSpec((1, tm, D_A), lambda b, i: (b, i, 0)),
        pl.BlockSpec((1, tm, D_B), lambda b, i: (b, i, 0)),
        pl.BlockSpec((1, tm, D_C), lambda b, i: (b, i, 0)),
        pl.BlockSpec((1, D_MIX, D_MODEL), lambda b, i: (layer, 0, 0), pipeline_mode=pl.Buffered(1)),
        pl.BlockSpec((1, 1, 3 * D_MODEL), lambda b, i: (mod_row(b), 0, 0)),
    ]
    args = [x, ga, gb, gc, w_out_bf, mod3]
    out_specs = [pl.BlockSpec((1, tm, D_MODEL), lambda b, i: (b, i, 0))]
    out_shape = [jax.ShapeDtypeStruct(x.shape, f32)]
    aliases = {}
    if next_norm is not None:
        norm_w, mod3_next, h_buf = next_norm
        in_specs += [pl.BlockSpec((1, D_MODEL), lambda b, i: (0, 0)),
                     pl.BlockSpec((1, 1, 3 * D_MODEL), lambda b, i: (mod_row(b), 0, 0))]
        args += [norm_w.reshape(1, D_MODEL), mod3_next]
        if h_buf is not None:
            in_specs.append(pl.BlockSpec(memory_space=pl.ANY))
            args.append(h_buf)
            aliases = {len(args) - 1: 1}
        out_specs.append(pl.BlockSpec((1, tm, D_MODEL), lambda b, i: (b, row_block0 + i, 0)))
        out_shape.append(jax.ShapeDtypeStruct((BATCH, S_ALL, D_MODEL), bf16))
    return pl.pallas_call(
        functools.partial(_out_kernel, next_norm is not None),
        grid=(BATCH, t // tm),
        in_specs=in_specs,
        out_specs=out_specs,
        out_shape=out_shape,
        input_output_aliases=aliases,
        compiler_params=_cparams("arbitrary", "arbitrary"),
        name="out_proj_ctx" if is_ctx else "out_proj",
    )(*args)


def _rope_tables():
    t = jnp.arange(SEQ)
    row = (t // GRID_W).astype(f32)
    col = (t % GRID_W).astype(f32)
    n_freq = ROPE_DIM // 4
    inv = ROPE_BASE ** (-jnp.arange(n_freq, dtype=f32) / n_freq)
    ar = row[:, None] * inv
    ac = col[:, None] * inv
    ang = jnp.concatenate([ar, ar, ac, ac], axis=-1)
    cos = jnp.cos(ang).astype(f32)
    sin = jnp.sin(ang).astype(f32)
    sign = jnp.asarray(np.where((np.arange(ROPE_DIM) % 32) < 16, -1.0, 1.0), dtype=f32)
    sin = sin * sign
    cos = jnp.concatenate([cos, jnp.ones((CTX_LEN, ROPE_DIM), f32)], axis=0)
    sin = jnp.concatenate([sin, jnp.zeros((CTX_LEN, ROPE_DIM), f32)], axis=0)
    pair = (jnp.concatenate([cos, cos], axis=1), jnp.concatenate([sin, sin], axis=1))
    single = (jnp.concatenate([cos, jnp.ones_like(cos)], axis=1), jnp.concatenate([sin, jnp.zeros_like(sin)], axis=1))
    return pair, single


def _pad_heads(w, width, padded):
    lead = w.shape[:-1]
    w = w.reshape(lead + (H_C, width))
    w = jnp.pad(w, [(0, 0)] * len(lead) + [(0, 0), (0, padded - width)])
    return w.reshape(lead + (H_C * padded,))


def kernel(x, c, ctx, c_ctx, norm_w, w_ada, b_ada, w_in, qn_a, kn_a, rpb_a, qn_b, kn_b, sink_b,
           qa_norm, kva_norm, w_qb, w_kvb, qn_c, kn_c, w_out):
    (cos2, sin2), (cos1, sin1) = _rope_tables()
    cvec = jnp.concatenate([c, c_ctx[None, :], jnp.zeros((MOD_ROWS - BATCH - 1, D_MODEL), f32)], axis=0)
    mod = _modulation(cvec, w_ada, b_ada)
    q_scale_ab = HD ** -0.5 * LOG2E
    q_scale_c = MLA_SCALE * LOG2E
    w_out_bf = w_out.astype(bf16)
    na_bias = _natten_bias(rpb_a)

    o_qa, o_ka, o_va, o_qb, o_kb, o_vb, o_cq, o_ckv, o_kpe, o_z = np.cumsum(
        (0, D_A, D_A, D_A, D_B, KV_B, KV_B, Q_LORA, KV_LORA, ROPE_DIM)).tolist()

    xc = ctx
    mod3_all = [mod[l].reshape(MOD_ROWS, 1, 3 * D_MODEL) for l in range(DEPTH)]
    h3 = _hnorm(x, xc, norm_w[0], mod3_all[0])
    for l in range(DEPTH):
        last = l == DEPTH - 1
        mod3 = mod3_all[l]
        h = h3.reshape(R_ALL, D_MODEL)
        wl = w_in[l]

        gain_hn = jnp.concatenate([jnp.tile(qn_a[l] * q_scale_ab, H_A), jnp.tile(kn_a[l], H_A),
                                   jnp.tile(qn_b[l] * q_scale_ab, H_B), jnp.tile(kn_b[l], HKV_B)])
        w_hn = jnp.concatenate([wl[:, o_qa:o_va], wl[:, o_qb:o_vb]], axis=1).astype(bf16)
        w_lora = jnp.pad(wl[:, o_cq:o_z], ((0, 0), (0, LANES - ROPE_DIM))).astype(bf16)
        wt_v = jnp.concatenate([wl[:, o_va:o_qb], wl[:, o_vb:o_cq]], axis=1).T.astype(bf16)
        qka, qkb, cqn, ckvn, kpe, vt_ab = _proj_main(h, w_hn, w_lora, wt_v, gain_hn, qa_norm[l], kva_norm[l],
                                                      cos2, sin2)
        sza, szb, szc = _proj_gate(h, wl[:, o_z:].astype(bf16))

        wq = _pad_heads(w_qb[l], QK_C, QC_PAD).astype(bf16)
        wkv = w_kvb[l].reshape(KV_LORA, H_C, NOPE_C + V_C)
        wkn = wkv[:, :, :NOPE_C].reshape(KV_LORA, H_C * NOPE_C).astype(bf16)
        wvt = wkv[:, :, NOPE_C:].reshape(KV_LORA, H_C * V_C).T.astype(bf16)
        gq = _pad_heads(jnp.tile(qn_c[l] * q_scale_c, H_C), QK_C, QC_PAD).reshape(1, H_C * QC_PAD)
        gk = jnp.pad(kn_c[l], (0, QC_PAD - QK_C)).reshape(1, QC_PAD)
        qc, kc, vt_c = _mla_up(cqn, ckvn, kpe, wq, wkn, wvt, gq, gk, cos1, sin1)

        def b3(a):
            return a.reshape(BATCH, S_ALL, a.shape[-1])

        qka, qkb, qc, kc, sza, szb, szc = map(b3, (qka, qkb, qc, kc, sza, szb, szc))

        ga = _natten(qka, vt_ab, na_bias, l, sza)
        gb = _swa(qkb, vt_ab, sink_b[l], szb)
        gc = _mla(qc, kc, vt_c, szc)
        if last:
            (x,) = _out_proj(x, ga, gb, gc, w_out_bf, l, mod3, False)
        else:
            ga_c, gb_c, gc_c = _ctx_attention(qka, qkb, vt_ab, qc, kc, vt_c, sink_b[l], sza, szb, szc)
            x, h3 = _out_proj(x, ga, gb, gc, w_out_bf, l, mod3, False,
                              next_norm=(norm_w[l + 1], mod3_all[l + 1], None))
            xc, h3 = _out_proj(xc, ga_c, gb_c, gc_c, w_out_bf, l, mod3, True,
                               next_norm=(norm_w[l + 1], mod3_all[l + 1], h3))
    return x
```

```python
import functools
import math

import numpy as np
import jax
import jax.numpy as jnp
from jax import lax
from jax.experimental import pallas as pl
from jax.experimental.pallas import tpu as pltpu

D_MODEL = 2048
BATCH = 8
SEQ = 2048
DEPTH = 2
GRID_W = 64
CTX_LEN = 256
HD = 64
H_A = 8
D_A = H_A * HD
H_B = 12
HKV_B = 4
D_B = H_B * HD
KV_B = HKV_B * HD
H_C = 6
NOPE_C = 128
ROPE_DIM = 64
QK_C = NOPE_C + ROPE_DIM
V_C = 128
D_C = H_C * V_C
Q_LORA = 768
KV_LORA = 512
D_MIX = D_A + D_B + D_C
NA_ROWS = 8
NA_COLS = 16
SW_WINDOW = 128
ROPE_BASE = 10000.0
EPS = 1e-6
NEG_INF = -1e30

S_ALL = SEQ + CTX_LEN
R_ALL = BATCH * S_ALL
GRID_ROWS = SEQ // GRID_W
LANES = 128
MXU_DIM = 256
BF16_ROWS = 16
QC_PAD = 256
VT_ROWS = LANES + BF16_ROWS
MOD_ROWS = 16
VMEM_LIMIT = 48 * 1024 * 1024
LOG2E = math.log2(math.e)

bf16 = jnp.bfloat16
f32 = jnp.float32


def _cparams(*sem):
    return pltpu.CompilerParams(dimension_semantics=sem, vmem_limit_bytes=VMEM_LIMIT)


def _silu(v):
    return v / (1.0 + jnp.exp(-v))


def _lane_iota(shape):
    return lax.broadcasted_iota(jnp.int32, shape, len(shape) - 1)


def _rope_rotate(xn, cos, sin_signed):
    lane = _lane_iota(xn.shape)
    take_next = (lane % 32) < 16
    rot = jnp.where(take_next, pltpu.roll(xn, LANES - 16, 1), pltpu.roll(xn, 16, 1))
    return xn * cos + rot * sin_signed


def _dot_nt(a, b):
    return lax.dot_general(a, b, (((1,), (1,)), ((), ())), preferred_element_type=f32)


def _dot(a, b):
    return jnp.dot(a, b, preferred_element_type=f32)


def _mod_kernel(c_ref, w_ref, b_ref, o_ref):
    sc = _silu(c_ref[...]).astype(bf16)
    o_ref[0] = _dot(sc, w_ref[0].astype(bf16)) + b_ref[0]


def _modulation(cvec, w_ada, b_ada):
    tn = 1024
    n = 3 * D_MODEL
    return pl.pallas_call(
        _mod_kernel,
        grid=(DEPTH, n // tn),
        in_specs=[
            pl.BlockSpec((MOD_ROWS, D_MODEL), lambda l, j: (0, 0)),
            pl.BlockSpec((1, D_MODEL, tn), lambda l, j: (l, 0, j)),
            pl.BlockSpec((1, 1, tn), lambda l, j: (l, 0, j)),
        ],
        out_specs=pl.BlockSpec((1, MOD_ROWS, tn), lambda l, j: (l, 0, j)),
        out_shape=jax.ShapeDtypeStruct((DEPTH, MOD_ROWS, n), f32),
        compiler_params=_cparams("arbitrary", "arbitrary"),
        name="modulation",
    )(cvec, w_ada, b_ada.reshape(DEPTH, 1, n))


H_ROWS = 256
N_LAT_BLOCKS = SEQ // H_ROWS


def _modulated_norm(x, nw, mod):
    ms = jnp.mean(x * x, axis=-1, keepdims=True)
    y = x * lax.rsqrt(ms + EPS) * nw
    return y * (1.0 + mod[:, D_MODEL:2 * D_MODEL]) + mod[:, 0:D_MODEL]


def _hnorm_kernel(x_ref, ctx_ref, nw_ref, mod_ref, h_ref):
    s = pl.program_id(1)

    def body(src):
        h_ref[0] = _modulated_norm(src[0], nw_ref[...], mod_ref[0]).astype(bf16)

    @pl.when(s < N_LAT_BLOCKS)
    def _():
        body(x_ref)

    @pl.when(s >= N_LAT_BLOCKS)
    def _():
        body(ctx_ref)


def _hnorm(x, xc, norm_w, mod3):
    return pl.pallas_call(
        _hnorm_kernel,
        grid=(BATCH, S_ALL // H_ROWS),
        in_specs=[
            pl.BlockSpec((1, H_ROWS, D_MODEL), lambda b, s: (b, jnp.minimum(s, N_LAT_BLOCKS - 1), 0)),
            pl.BlockSpec((1, H_ROWS, D_MODEL), lambda b, s: (b, 0, 0)),
            pl.BlockSpec((1, D_MODEL), lambda b, s: (0, 0)),
            pl.BlockSpec((1, 1, 3 * D_MODEL), lambda b, s: (jnp.where(s < N_LAT_BLOCKS, b, BATCH), 0, 0)),
        ],
        out_specs=pl.BlockSpec((1, H_ROWS, D_MODEL), lambda b, s: (b, s, 0)),
        out_shape=jax.ShapeDtypeStruct((BATCH, S_ALL, D_MODEL), bf16),
        compiler_params=_cparams("arbitrary", "arbitrary"),
        name="hnorm",
    )(x, xc, norm_w.reshape(1, D_MODEL), mod3)


PROJ_TM = 768


def _group_mean_sq(u):
    sq = u * u
    low = _lane_iota((u.shape[0], LANES)) < HD
    parts = []
    for c in range(0, u.shape[1], LANES):
        t = sq[:, c:c + LANES]
        s_low = jnp.sum(jnp.where(low, t, 0.0), axis=-1, keepdims=True)
        s_high = jnp.sum(jnp.where(low, 0.0, t), axis=-1, keepdims=True)
        parts.append(jnp.where(low, s_low, s_high) * (1.0 / HD))
    return jnp.concatenate(parts, axis=1)


def _run_items(items):
    u = items[0][0]()
    for i, (_, epilogue) in enumerate(items):
        u_next = items[i + 1][0]() if i + 1 < len(items) else None
        epilogue(u)
        u = u_next


def _store_vt_tile(vt_ref, p, ut):
    vt_ref[p, 0:LANES, :] = ut.astype(bf16)
    vt_ref[p, LANES:VT_ROWS, :] = jnp.ones((VT_ROWS - LANES, ut.shape[1]), bf16)


PROJ_TN = 512
N_HN = 2 * D_A + D_B + KV_B
N_LORA = Q_LORA + KV_LORA + LANES
N_VT = D_A + KV_B


def _proj_main_kernel(h_ref, whn_ref, wl_ref, wvt_ref, ghn_ref, cos_ref, sin_ref, gcq_ref, gckv_ref,
                      qka_ref, qkb_ref, cqn_ref, ckvn_ref, kpe_ref, vt_ref):
    def hn_item(k):
        cols = slice(k * PROJ_TN, (k + 1) * PROJ_TN)
        rope = k * PROJ_TN >= 2 * D_A
        o_ref = qkb_ref if rope else qka_ref
        o0 = k * PROJ_TN - (2 * D_A if rope else 0)

        def epilogue(u):
            for c0 in range(0, PROJ_TN, MXU_DIM):
                uc = u[:, c0:c0 + MXU_DIM]
                ms = _group_mean_sq(uc)
                xn = uc * lax.rsqrt(ms + EPS) * ghn_ref[:, k * PROJ_TN + c0:k * PROJ_TN + c0 + MXU_DIM]
                if rope:
                    for c1 in range(0, MXU_DIM, LANES):
                        o_ref[:, o0 + c0 + c1:o0 + c0 + c1 + LANES] = _rope_rotate(
                            xn[:, c1:c1 + LANES], cos_ref[...], sin_ref[...]).astype(bf16)
                else:
                    o_ref[:, o0 + c0:o0 + c0 + MXU_DIM] = xn.astype(bf16)

        return lambda: _dot(h_ref[...], whn_ref[:, cols]), epilogue

    def rowrms_item(c0, width, gain_ref, o_ref):
        def epilogue(u):
            ms = jnp.mean(u * u, axis=-1, keepdims=True)
            o_ref[...] = (u * lax.rsqrt(ms + EPS) * gain_ref[...]).astype(bf16)

        return lambda: _dot(h_ref[...], wl_ref[:, c0:c0 + width]), epilogue

    def kpe_item():
        def epilogue(u):
            kpe_ref[...] = u

        return lambda: _dot(h_ref[...], wl_ref[:, Q_LORA + KV_LORA:N_LORA]), epilogue

    def vt_item(p):
        def epilogue(ut):
            _store_vt_tile(vt_ref, p, ut)

        return lambda: _dot_nt(wvt_ref[p * LANES:(p + 1) * LANES, :], h_ref[...]), epilogue

    _run_items([hn_item(k) for k in range(N_HN // PROJ_TN)]
               + [rowrms_item(0, Q_LORA, gcq_ref, cqn_ref), rowrms_item(Q_LORA, KV_LORA, gckv_ref, ckvn_ref),
                  kpe_item()]
               + [vt_item(p) for p in range(N_VT // LANES)])


def _resident(shape):
    return pl.BlockSpec(shape, lambda i: (0,) * len(shape), pipeline_mode=pl.Buffered(1))


def _proj_main(h2d, whn, wl, wvt, ghn, gcq, gckv, cos, sin):
    tm = PROJ_TM
    nt = S_ALL // tm
    row = lambda i: (i, 0)
    tiles = N_VT // LANES
    return pl.pallas_call(
        _proj_main_kernel,
        grid=(R_ALL // tm,),
        in_specs=[
            pl.BlockSpec((tm, D_MODEL), row),
            _resident((D_MODEL, N_HN)), _resident((D_MODEL, N_LORA)), _resident((N_VT, D_MODEL)),
            _resident((1, N_HN)),
            pl.BlockSpec((tm, LANES), lambda i: (i % nt, 0)),
            pl.BlockSpec((tm, LANES), lambda i: (i % nt, 0)),
            _resident((1, Q_LORA)), _resident((1, KV_LORA)),
        ],
        out_specs=[
            pl.BlockSpec((tm, 2 * D_A), row),
            pl.BlockSpec((tm, D_B + KV_B), row),
            pl.BlockSpec((tm, Q_LORA), row),
            pl.BlockSpec((tm, KV_LORA), row),
            pl.BlockSpec((tm, LANES), row),
            pl.BlockSpec((tiles, VT_ROWS, tm), lambda i: (0, 0, i)),
        ],
        out_shape=[
            jax.ShapeDtypeStruct((R_ALL, 2 * D_A), bf16),
            jax.ShapeDtypeStruct((R_ALL, D_B + KV_B), bf16),
            jax.ShapeDtypeStruct((R_ALL, Q_LORA), bf16),
            jax.ShapeDtypeStruct((R_ALL, KV_LORA), bf16),
            jax.ShapeDtypeStruct((R_ALL, LANES), f32),
            jax.ShapeDtypeStruct((tiles, VT_ROWS, R_ALL), bf16),
        ],
        compiler_params=_cparams("arbitrary"),
        name="proj_main",
    )(h2d, whn, wl, wvt, ghn.reshape(1, N_HN), cos, sin,
      gcq.reshape(1, Q_LORA), gckv.reshape(1, KV_LORA))


def _proj_gate_kernel(h_ref, wz_ref, sza_ref, szb_ref, szc_ref):
    outs = ((sza_ref, 0, D_A), (szb_ref, D_A, D_B), (szc_ref, D_A + D_B, D_C))

    def item(k):
        c0 = k * PROJ_TN

        def epilogue(u):
            sz = _silu(u)
            for o_ref, start, width in outs:
                lo, hi = max(c0, start), min(c0 + PROJ_TN, start + width)
                if lo < hi:
                    o_ref[:, lo - start:hi - start] = sz[:, lo - c0:hi - c0]

        return lambda: _dot(h_ref[...], wz_ref[:, c0:c0 + PROJ_TN]), epilogue

    _run_items([item(k) for k in range(D_MIX // PROJ_TN)])


def _proj_gate(h2d, wz):
    tm = PROJ_TM
    row = lambda i: (i, 0)
    return pl.pallas_call(
        _proj_gate_kernel,
        grid=(R_ALL // tm,),
        in_specs=[pl.BlockSpec((tm, D_MODEL), row), _resident((D_MODEL, D_MIX))],
        out_specs=[pl.BlockSpec((tm, D_A), row), pl.BlockSpec((tm, D_B), row), pl.BlockSpec((tm, D_C), row)],
        out_shape=[jax.ShapeDtypeStruct((R_ALL, D_A), f32), jax.ShapeDtypeStruct((R_ALL, D_B), f32),
                   jax.ShapeDtypeStruct((R_ALL, D_C), f32)],
        compiler_params=_cparams("arbitrary"),
        name="proj_gate",
    )(h2d, wz)


def _mla_up_kernel(cq_ref, ckv_ref, kpe_ref, wq_ref, wkn_ref, wv_ref, gq_ref, gk_ref,
                   cos_ref, sin_ref, qc_ref, kc_ref, vc_ref):
    cos = cos_ref[...]
    sin = sin_ref[...]
    kpe = kpe_ref[...]
    ss_pe = jnp.sum(kpe * kpe, axis=-1, keepdims=True)
    k_rot = _rope_rotate(kpe * gk_ref[:, LANES:], cos, sin)

    def q_head(h):
        cols = slice(h * QC_PAD, (h + 1) * QC_PAD)
        return lambda: _dot(cq_ref[...], wq_ref[:, cols]), functools.partial(q_epilogue, h)

    def q_epilogue(h, u):
        ms = jnp.sum(u * u, axis=-1, keepdims=True) * (1.0 / QK_C)
        xn = u * lax.rsqrt(ms + EPS) * gq_ref[:, h * QC_PAD:(h + 1) * QC_PAD]
        qc_ref[:, h * QC_PAD:h * QC_PAD + LANES] = xn[:, :LANES].astype(bf16)
        qc_ref[:, h * QC_PAD + LANES:(h + 1) * QC_PAD] = _rope_rotate(xn[:, LANES:], cos, sin).astype(bf16)

    def k_head(h):
        cols = slice(h * NOPE_C, (h + 1) * NOPE_C)
        return lambda: _dot(ckv_ref[...], wkn_ref[:, cols]), functools.partial(k_epilogue, h)

    def k_epilogue(h, u):
        ms = (jnp.sum(u * u, axis=-1, keepdims=True) + ss_pe) * (1.0 / QK_C)
        r = lax.rsqrt(ms + EPS)
        kc_ref[:, h * QC_PAD:h * QC_PAD + LANES] = (u * r * gk_ref[:, 0:LANES]).astype(bf16)
        kc_ref[:, h * QC_PAD + LANES:(h + 1) * QC_PAD] = (k_rot * r).astype(bf16)

    def v_head(h):
        def epilogue(u):
            vc_ref[:, h * VC_PAD:h * VC_PAD + V_C] = u.astype(bf16)
            vc_ref[:, h * VC_PAD + V_C:(h + 1) * VC_PAD] = jnp.ones((u.shape[0], VC_PAD - V_C), bf16)

        return lambda: _dot(ckv_ref[...], wv_ref[:, h * V_C:(h + 1) * V_C]), epilogue

    _run_items([q_head(h) for h in range(H_C)] + [k_head(h) for h in range(H_C)] + [v_head(h) for h in range(H_C)])


def _mla_up(cqn, ckvn, kpe, wq, wkn, wv, gq, gk, cos, sin):
    tm = PROJ_TM
    nt = S_ALL // tm
    row = lambda i: (i, 0)
    fixed = lambda i: (0, 0)
    return pl.pallas_call(
        _mla_up_kernel,
        grid=(R_ALL // tm,),
        in_specs=[
            pl.BlockSpec((tm, Q_LORA), row),
            pl.BlockSpec((tm, KV_LORA), row),
            pl.BlockSpec((tm, LANES), row),
            pl.BlockSpec((Q_LORA, H_C * QC_PAD), fixed),
            pl.BlockSpec((KV_LORA, H_C * NOPE_C), fixed),
            pl.BlockSpec((KV_LORA, H_C * V_C), fixed),
            pl.BlockSpec((1, H_C * QC_PAD), fixed),
            pl.BlockSpec((1, QC_PAD), fixed),
            pl.BlockSpec((tm, LANES), lambda i: (i % nt, 0)),
            pl.BlockSpec((tm, LANES), lambda i: (i % nt, 0)),
        ],
        out_specs=[
            pl.BlockSpec((tm, H_C * QC_PAD), row),
            pl.BlockSpec((tm, H_C * QC_PAD), row),
            pl.BlockSpec((tm, H_C * VC_PAD), row),
        ],
        out_shape=[
            jax.ShapeDtypeStruct((R_ALL, H_C * QC_PAD), bf16),
            jax.ShapeDtypeStruct((R_ALL, H_C * QC_PAD), bf16),
            jax.ShapeDtypeStruct((R_ALL, H_C * VC_PAD), bf16),
        ],
        compiler_params=_cparams("arbitrary"),
        name="mla_up",
    )(cqn, ckvn, kpe, wq, wkn, wv, gq, gk, cos, sin)


def _pipelined(n, score_fn, value_fn):
    m = score_fn(0)
    for j in range(n):
        m_next = score_fn(j + 1) if j + 1 < n else None
        value_fn(j, m)
        m = m_next


def _col_max(s):
    return jnp.max(s, axis=0, keepdims=True)


def _prob(s, m):
    return jnp.exp2((s - m).astype(bf16))


def _pair_q(qv, half_even, half_odd):
    low = _lane_iota(qv.shape) < HD
    zero = jnp.zeros_like(qv)
    qa = jnp.where(low, qv, zero) if half_even == 0 else jnp.where(low, zero, pltpu.roll(qv, HD, 1))
    qb = jnp.where(low, zero, qv) if half_odd == 1 else jnp.where(low, pltpu.roll(qv, HD, 1), zero)
    return jnp.concatenate([qa, qb], axis=0).astype(bf16)


def _pair_out(acc, l, n, half_even, half_odd):
    o = acc[:LANES] / l
    t = jnp.concatenate([o[half_even * HD:(half_even + 1) * HD, 0:n],
                         o[half_odd * HD:(half_odd + 1) * HD, n:2 * n]], axis=0)
    return t.T


NA_QROWS = 2
NA_Q = NA_QROWS * GRID_W
NA_KROWS = NA_ROWS + NA_QROWS
NA_KEYS = NA_KROWS * GRID_W
NA_VARIANTS = ((0, 0, 0, -1), (0, -2, 0, -3), (0, -4, 1, -5), (0, -4, 0, -5), (0, -6, 0, -7))


def _na_window_row(j):
    return jnp.clip(NA_QROWS * j - NA_ROWS // 2, 0, GRID_ROWS - NA_ROWS)


NA_SUB = 2
NA_BLOCKS = GRID_ROWS // NA_QROWS


def _natten_kernel(q_ref, k_ref, vt_ref, bias0_ref, bias1_ref, sz_ref, o_ref, s_scr):
    step = pl.program_id(1)
    bias_refs = (bias0_ref, bias1_ref)
    k0 = [pl.multiple_of(_na_window_row(NA_SUB * step + s) * GRID_W, LANES) for s in range(NA_SUB)]
    n_pair = H_A // 2

    def scores(i):
        s, p = divmod(i, n_pair)
        cols = slice(p * LANES, (p + 1) * LANES)
        q2 = _pair_q(q_ref[0, s * NA_Q:(s + 1) * NA_Q, cols].astype(f32), 0, 1)
        s_loc = _dot_nt(k_ref[0, pl.ds(k0[s], NA_KEYS), cols], q2) + bias_refs[s][0, 0, p]
        s_ctx = _dot_nt(k_ref[0, SEQ:S_ALL, cols], q2)
        s_scr[i % 2, 0:NA_KEYS, :] = s_loc
        s_scr[i % 2, NA_KEYS:, :] = s_ctx
        return jnp.maximum(_col_max(s_loc), _col_max(s_ctx))

    def values(i, m):
        s, p = divmod(i, n_pair)
        cols = slice(p * LANES, (p + 1) * LANES)
        rows = slice(s * NA_Q, (s + 1) * NA_Q)
        acc = (_dot(vt_ref[p, :, pl.ds(k0[s], NA_KEYS)], _prob(s_scr[i % 2, 0:NA_KEYS, :], m))
               + _dot(vt_ref[p, :, SEQ:S_ALL], _prob(s_scr[i % 2, NA_KEYS:, :], m)))
        o = _pair_out(acc, acc[LANES:LANES + 1], NA_Q, 0, 1)
        o_ref[0, rows, cols] = (o * sz_ref[0, rows, cols]).astype(bf16)

    _pipelined(NA_SUB * n_pair, scores, values)


def _natten(qk, vt, bias, layer, sz):
    def variant(s):
        def index(b, step):
            j = NA_SUB * step + s
            v = jnp.where(j <= 1, j, jnp.where(j >= NA_BLOCKS - 2, j - (NA_BLOCKS - 5), 2))
            return (layer, v, 0, 0, 0)
        return index

    tq = NA_SUB * NA_Q
    bias_block = (1, 1, H_A // 2, NA_KEYS, 2 * NA_Q)
    return pl.pallas_call(
        _natten_kernel,
        grid=(BATCH, NA_BLOCKS // NA_SUB),
        in_specs=[
            pl.BlockSpec((1, tq, D_A), lambda b, j: (b, j, 0)),
            pl.BlockSpec((1, S_ALL, D_A), lambda b, j: (b, 0, 1)),
            pl.BlockSpec((H_A // 2, VT_ROWS, S_ALL), lambda b, j: (0, 0, b)),
            pl.BlockSpec(bias_block, variant(0)),
            pl.BlockSpec(bias_block, variant(1)),
            pl.BlockSpec((1, tq, D_A), lambda b, j: (b, j, 0)),
        ],
        out_specs=pl.BlockSpec((1, tq, D_A), lambda b, j: (b, j, 0)),
        out_shape=jax.ShapeDtypeStruct((BATCH, SEQ, D_A), bf16),
        scratch_shapes=[pltpu.VMEM((2, NA_KEYS + CTX_LEN, 2 * NA_Q), f32)],
        compiler_params=_cparams("arbitrary", "arbitrary"),
        name="natten",
    )(qk, qk, vt, bias, bias, sz)


def _natten_bias(rpb):
    n_dc = 2 * NA_COLS - 1
    n_dr = NA_KROWS + NA_ROWS - 1
    c = np.arange(LANES) % GRID_W
    kc = np.arange(GRID_W)
    qstart = np.clip(c - NA_COLS // 2, 0, GRID_W - NA_COLS)
    col_ok = (kc[:, None] >= qstart[None, :]) & (kc[:, None] < qstart[None, :] + NA_COLS)
    dc = np.clip(kc[:, None] - c[None, :], -(NA_COLS - 1), NA_COLS - 1) + NA_COLS - 1
    onehot = jnp.asarray((dc[None] == np.arange(n_dc)[:, None, None]).astype(np.float32))
    exp = jnp.einsum("lhrd,dkc->lhrkc", rpb.astype(f32) * LOG2E, onehot, precision=lax.Precision.HIGHEST)
    exp = jnp.pad(exp, ((0, 0), (0, 0), (0, n_dr - exp.shape[2]), (0, 0), (0, 0)))
    col_mask = jnp.asarray(np.where(col_ok, 0.0, NEG_INF).astype(np.float32))
    n_var = len(NA_VARIANTS)
    return pl.pallas_call(
        _natten_bias_kernel,
        grid=(DEPTH, n_var, H_A // 2),
        in_specs=[
            pl.BlockSpec((1, 2, n_dr, GRID_W, LANES), lambda l, v, p: (l, p, 0, 0, 0)),
            pl.BlockSpec((GRID_W, LANES), lambda l, v, p: (0, 0)),
        ],
        out_specs=pl.BlockSpec((1, 1, 1, NA_KEYS, 2 * NA_Q), lambda l, v, p: (l, v, p, 0, 0)),
        out_shape=jax.ShapeDtypeStruct((DEPTH, n_var, H_A // 2, NA_KEYS, 2 * NA_Q), f32),
        compiler_params=_cparams("arbitrary", "arbitrary", "arbitrary"),
        name="natten_bias",
    )(exp, col_mask)


def _natten_bias_kernel(exp_ref, mask_ref, o_ref):
    v = pl.program_id(1)

    def pick(column):
        out = jnp.int32(0)
        for i, var in enumerate(NA_VARIANTS):
            out = jnp.where(v == i, var[column], out)
        return out

    lo = (pick(0), pick(2))
    off = (pick(1), pick(3))
    low = _lane_iota((GRID_W, LANES)) < GRID_W
    mask = mask_ref[...]
    for kr in range(NA_KROWS):
        for e in range(2):
            halves = []
            for qr in range(NA_QROWS):
                blk = exp_ref[0, e, pl.ds(kr + off[qr] + NA_ROWS - 1, 1)][0] + mask
                valid = (kr >= lo[qr]) & (kr < lo[qr] + NA_ROWS)
                halves.append(jnp.where(valid, blk, NEG_INF))
            o_ref[0, 0, 0, kr * GRID_W:(kr + 1) * GRID_W, e * NA_Q:(e + 1) * NA_Q] = jnp.where(
                low, halves[0], halves[1])


SW_Q = 128
SW_SPAN = SW_Q + 2 * SW_WINDOW
G_B = H_B // HKV_B
N_CHUNK_B = H_B // 2


def _gqa_halves(c):
    return ((2 * c) // G_B) % 2, ((2 * c + 1) // G_B) % 2


SW_SUB = 4
SW_BLOCKS = SEQ // SW_Q


def _swa_kernel(q_ref, k_ref, vt_ref, mask_ref, sink_ref, sz_ref, o_ref, s_scr):
    step = pl.program_id(1)
    blocks = [SW_SUB * step + s for s in range(SW_SUB)]
    start = [pl.multiple_of(jnp.clip((n - 1) * SW_Q, 0, SEQ - SW_SPAN), SW_Q) for n in blocks]
    variant = [jnp.where(n == 0, 0, jnp.where(n == SW_BLOCKS - 1, 2, 1)) for n in blocks]

    def scores(i):
        s, c = divmod(i, N_CHUNK_B)
        kp = c // G_B
        cols = slice(c * LANES, (c + 1) * LANES)
        kcols = slice(kp * LANES, (kp + 1) * LANES)
        q2 = _pair_q(q_ref[0, s * SW_Q:(s + 1) * SW_Q, cols].astype(f32), *_gqa_halves(c))
        s_loc = _dot_nt(k_ref[0, pl.ds(start[s], SW_SPAN), kcols], q2) + mask_ref[variant[s]]
        s_ctx = _dot_nt(k_ref[0, SEQ:S_ALL, kcols], q2)
        s_scr[i % 2, 0:SW_SPAN, :] = s_loc
        s_scr[i % 2, SW_SPAN:, :] = s_ctx
        return jnp.maximum(jnp.maximum(_col_max(s_loc), _col_max(s_ctx)), sink_ref[c])

    def values(i, m):
        s, c = divmod(i, N_CHUNK_B)
        kp = c // G_B
        cols = slice(c * LANES, (c + 1) * LANES)
        rows = slice(s * SW_Q, (s + 1) * SW_Q)
        acc = (_dot(vt_ref[kp, :, pl.ds(start[s], SW_SPAN)], _prob(s_scr[i % 2, 0:SW_SPAN, :], m))
               + _dot(vt_ref[kp, :, SEQ:S_ALL], _prob(s_scr[i % 2, SW_SPAN:, :], m)))
        l = acc[LANES:LANES + 1] + jnp.exp2(sink_ref[c] - m)
        o = _pair_out(acc, l, SW_Q, *_gqa_halves(c))
        o_ref[0, rows, cols] = (o * sz_ref[0, rows, cols]).astype(bf16)

    _pipelined(SW_SUB * N_CHUNK_B, scores, values)


def _swa_mask():
    i = np.arange(SW_Q)
    j = np.arange(SW_SPAN)
    out = []
    for d0 in (0, SW_Q, 2 * SW_Q):
        ok = np.abs(d0 + i[None, :] - j[:, None]) <= SW_WINDOW
        m = np.where(ok, 0.0, NEG_INF).astype(np.float32)
        out.append(np.concatenate([m, m], axis=1))
    return jnp.asarray(np.stack(out))


def _sink_cols(sink, n):
    return jnp.repeat(sink.astype(f32).reshape(N_CHUNK_B, 2) * LOG2E, n, axis=1)[:, None, :]


def _swa(qk, vt, sink, sz):
    tq = SW_SUB * SW_Q
    return pl.pallas_call(
        _swa_kernel,
        grid=(BATCH, SW_BLOCKS // SW_SUB),
        in_specs=[
            pl.BlockSpec((1, tq, D_B), lambda b, n: (b, n, 0)),
            pl.BlockSpec((1, S_ALL, KV_B), lambda b, n: (b, 0, D_B // KV_B)),
            pl.BlockSpec((HKV_B // 2, VT_ROWS, S_ALL), lambda b, n: (H_A // HKV_B, 0, b)),
            pl.BlockSpec((3, SW_SPAN, 2 * SW_Q), lambda b, n: (0, 0, 0)),
            pl.BlockSpec((N_CHUNK_B, 1, 2 * SW_Q), lambda b, n: (0, 0, 0)),
            pl.BlockSpec((1, tq, D_B), lambda b, n: (b, n, 0)),
        ],
        out_specs=pl.BlockSpec((1, tq, D_B), lambda b, n: (b, n, 0)),
        out_shape=jax.ShapeDtypeStruct((BATCH, SEQ, D_B), bf16),
        scratch_shapes=[pltpu.VMEM((2, SW_SPAN + CTX_LEN, 2 * SW_Q), f32)],
        compiler_params=_cparams("arbitrary", "arbitrary"),
        name="swa",
    )(qk, qk, vt, _swa_mask(), _sink_cols(sink, SW_Q), sz)


MLA_TQ = SEQ
MLA_SUB = 512
MLA_KTILE = MXU_DIM
MLA_SCALE = float(QK_C) ** -0.5
VC_PAD = 2 * V_C


def _mla_kernel(q_ref, k_ref, v_ref, sz_ref, o_ref, s_scr, p_scr):
    tiles = [slice(c * MLA_KTILE, (c + 1) * MLA_KTILE) for c in range(S_ALL // MLA_KTILE)]

    def scores(j):
        q = q_ref[0, j * MLA_SUB:(j + 1) * MLA_SUB, :]
        m = None
        for kt in tiles:
            s = _dot_nt(q, k_ref[0, kt, :])
            s_scr[j % 2, :, kt] = s
            mc = jnp.maximum(s[:, :LANES], s[:, LANES:])
            m = mc if m is None else jnp.maximum(m, mc)
        return jnp.max(m, axis=-1, keepdims=True)

    def values(j, m):
        for kt in tiles:
            p_scr[:, kt] = _prob(s_scr[j % 2, :, kt], m)
        acc = _dot(p_scr[...], v_ref[0])
        rows = slice(j * MLA_SUB, (j + 1) * MLA_SUB)
        o_ref[0, rows, :] = (acc[:, :V_C] / acc[:, V_C:] * sz_ref[0, rows, :]).astype(bf16)

    _pipelined(MLA_TQ // MLA_SUB, scores, values)


def _mla(qc, kc, vc, sz):
    return pl.pallas_call(
        _mla_kernel,
        grid=(BATCH, H_C),
        in_specs=[
            pl.BlockSpec((1, MLA_TQ, QC_PAD), lambda b, h: (b, 0, h)),
            pl.BlockSpec((1, S_ALL, QC_PAD), lambda b, h: (b, 0, h)),
            pl.BlockSpec((1, S_ALL, VC_PAD), lambda b, h: (b, 0, h)),
            pl.BlockSpec((1, MLA_TQ, V_C), lambda b, h: (b, 0, h)),
        ],
        out_specs=pl.BlockSpec((1, MLA_TQ, V_C), lambda b, h: (b, 0, h)),
        out_shape=jax.ShapeDtypeStruct((BATCH, SEQ, D_C), bf16),
        scratch_shapes=[pltpu.VMEM((2, MLA_SUB, S_ALL), f32), pltpu.VMEM((MLA_SUB, S_ALL), bf16)],
        compiler_params=_cparams("arbitrary", "arbitrary"),
        name="mla",
    )(qc, kc, vc, sz)


def _ctx_kernel(qa_ref, ka_ref, qb_ref, kb_ref, vtab_ref, qc_ref, kc_ref, vc_ref,
                sink_ref, sza_ref, szb_ref, szc_ref, oa_ref, ob_ref, oc_ref):
    n = CTX_LEN

    def item_a(p):
        cols = slice(p * LANES, (p + 1) * LANES)

        def values(s):
            acc = _dot(vtab_ref[p], _prob(s, _col_max(s)))
            o = _pair_out(acc, acc[LANES:LANES + 1], n, 0, 1)
            oa_ref[0, :, cols] = (o * sza_ref[0, :, cols]).astype(bf16)

        return lambda: _dot_nt(ka_ref[0, :, cols], _pair_q(qa_ref[0, :, cols].astype(f32), 0, 1)), values

    def item_b(c):
        kp = c // G_B
        cols = slice(c * LANES, (c + 1) * LANES)

        def values(s):
            m = jnp.maximum(_col_max(s), sink_ref[c])
            acc = _dot(vtab_ref[H_A // 2 + kp], _prob(s, m))
            l = acc[LANES:LANES + 1] + jnp.exp2(sink_ref[c] - m)
            o = _pair_out(acc, l, n, *_gqa_halves(c))
            ob_ref[0, :, cols] = (o * szb_ref[0, :, cols]).astype(bf16)

        return lambda: _dot_nt(kb_ref[0, :, kp * LANES:(kp + 1) * LANES],
                               _pair_q(qb_ref[0, :, cols].astype(f32), *_gqa_halves(c))), values

    def item_c(h):
        qcols = slice(h * QC_PAD, (h + 1) * QC_PAD)
        vcols = slice(h * V_C, (h + 1) * V_C)

        def values(s):
            p = _prob(s, jnp.max(s, axis=-1, keepdims=True))
            acc = _dot(p, vc_ref[0, :, h * VC_PAD:(h + 1) * VC_PAD])
            oc_ref[0, :, vcols] = (acc[:, :V_C] / acc[:, V_C:] * szc_ref[0, :, vcols]).astype(bf16)

        return lambda: _dot_nt(qc_ref[0, :, qcols], kc_ref[0, :, qcols]), values

    _run_items([item_a(p) for p in range(H_A // 2)] + [item_b(c) for c in range(N_CHUNK_B)]
               + [item_c(h) for h in range(H_C)])


def _ctx_attention(qka, qkb, vt_ab, qc, kc, vc, sink, sza, szb, szc):
    cb = SEQ // CTX_LEN

    def spec(width, col):
        return pl.BlockSpec((1, CTX_LEN, width), lambda b: (b, cb, col))

    def vt_spec(tiles):
        return pl.BlockSpec((tiles, VT_ROWS, CTX_LEN), lambda b: (0, 0, b * (S_ALL // CTX_LEN) + cb))

    return pl.pallas_call(
        _ctx_kernel,
        grid=(BATCH,),
        in_specs=[
            spec(D_A, 0), spec(D_A, 1),
            spec(D_B, 0), spec(KV_B, D_B // KV_B),
            vt_spec((H_A + HKV_B) // 2),
            spec(H_C * QC_PAD, 0), spec(H_C * QC_PAD, 0), spec(H_C * VC_PAD, 0),
            pl.BlockSpec((N_CHUNK_B, 1, 2 * CTX_LEN), lambda b: (0, 0, 0)),
            spec(D_A, 0), spec(D_B, 0), spec(D_C, 0),
        ],
        out_specs=[
            pl.BlockSpec((1, CTX_LEN, D_A), lambda b: (b, 0, 0)),
            pl.BlockSpec((1, CTX_LEN, D_B), lambda b: (b, 0, 0)),
            pl.BlockSpec((1, CTX_LEN, D_C), lambda b: (b, 0, 0)),
        ],
        out_shape=[
            jax.ShapeDtypeStruct((BATCH, CTX_LEN, D_A), bf16),
            jax.ShapeDtypeStruct((BATCH, CTX_LEN, D_B), bf16),
            jax.ShapeDtypeStruct((BATCH, CTX_LEN, D_C), bf16),
        ],
        compiler_params=_cparams("arbitrary"),
        name="ctx_attention",
    )(qka, qka, qkb, qkb, vt_ab, qc, kc, vc, _sink_cols(sink, CTX_LEN), sza, szb, szc)


OUT_TM = 512


def _out_kernel(next_norm, x_ref, ga_ref, gb_ref, gc_ref, w_ref, mod_ref, *rest):
    g = jnp.concatenate([ga_ref[0], gb_ref[0], gc_ref[0]], axis=1)
    x_new = x_ref[0] + mod_ref[0, :, 2 * D_MODEL:] * _dot(g, w_ref[0])
    if next_norm:
        nw_ref, modn_ref = rest[0], rest[1]
        o_ref, h_ref = rest[-2], rest[-1]
        o_ref[0] = x_new
        h_ref[0] = _modulated_norm(x_new, nw_ref[...], modn_ref[0]).astype(bf16)
    else:
        rest[-1][0] = x_new


def _out_proj(x, ga, gb, gc, w_out_bf, layer, mod3, is_ctx, next_norm=None):
    t = x.shape[1]
    tm = min(t, OUT_TM)
    mod_row = (lambda b: BATCH) if is_ctx else (lambda b: b)
    row_block0 = SEQ // tm if is_ctx else 0
    in_specs = [
        pl.BlockSpec((1, tm, D_MODEL), lambda b, i: (b, i, 0)),
        pl.BlockSpec((1, tm, D_A), lambda b, i: (b, i, 0)),
        pl.BlockSpec((1, tm, D_B), lambda b, i: (b, i, 0)),
        pl.BlockSpec((1, tm, D_C), lambda b, i: (b, i, 0)),
        pl.BlockSpec((1, D_MIX, D_MODEL), lambda b, i: (layer, 0, 0), pipeline_mode=pl.Buffered(1)),
        pl.BlockSpec((1, 1, 3 * D_MODEL), lambda b, i: (mod_row(b), 0, 0)),
    ]
    args = [x, ga, gb, gc, w_out_bf, mod3]
    out_specs = [pl.BlockSpec((1, tm, D_MODEL), lambda b, i: (b, i, 0))]
    out_shape = [jax.ShapeDtypeStruct(x.shape, f32)]
    aliases = {}
    if next_norm is not None:
        norm_w, mod3_next, h_buf = next_norm
        in_specs += [pl.BlockSpec((1, D_MODEL), lambda b, i: (0, 0)),
                     pl.BlockSpec((1, 1, 3 * D_MODEL), lambda b, i: (mod_row(b), 0, 0))]
        args += [norm_w.reshape(1, D_MODEL), mod3_next]
        if h_buf is not None:
            in_specs.append(pl.BlockSpec(memory_space=pl.ANY))
            args.append(h_buf)
            aliases = {len(args) - 1: 1}
        out_specs.append(pl.BlockSpec((1, tm, D_MODEL), lambda b, i: (b, row_block0 + i, 0)))
        out_shape.append(jax.ShapeDtypeStruct((BATCH, S_ALL, D_MODEL), bf16))
    return pl.pallas_call(
        functools.partial(_out_kernel, next_norm is not None),
        grid=(BATCH, t // tm),
        in_specs=in_specs,
        out_specs=out_specs,
        out_shape=out_shape,
        input_output_aliases=aliases,
        compiler_params=_cparams("arbitrary", "arbitrary"),
        name="out_proj_ctx" if is_ctx else "out_proj",
    )(*args)


def _rope_tables():
    t = jnp.arange(SEQ)
    row = (t // GRID_W).astype(f32)
    col = (t % GRID_W).astype(f32)
    n_freq = ROPE_DIM // 4
    inv = ROPE_BASE ** (-jnp.arange(n_freq, dtype=f32) / n_freq)
    ar = row[:, None] * inv
    ac = col[:, None] * inv
    ang = jnp.concatenate([ar, ar, ac, ac], axis=-1)
    cos = jnp.cos(ang).astype(f32)
    sin = jnp.sin(ang).astype(f32)
    sign = jnp.asarray(np.where((np.arange(ROPE_DIM) % 32) < 16, -1.0, 1.0), dtype=f32)
    sin = sin * sign
    cos = jnp.concatenate([cos, jnp.ones((CTX_LEN, ROPE_DIM), f32)], axis=0)
    sin = jnp.concatenate([sin, jnp.zeros((CTX_LEN, ROPE_DIM), f32)], axis=0)
    pair = (jnp.concatenate([cos, cos], axis=1), jnp.concatenate([sin, sin], axis=1))
    single = (jnp.concatenate([cos, jnp.ones_like(cos)], axis=1), jnp.concatenate([sin, jnp.zeros_like(sin)], axis=1))
    return pair, single


def _pad_heads(w, width, padded):
    lead = w.shape[:-1]
    w = w.reshape(lead + (H_C, width))
    w = jnp.pad(w, [(0, 0)] * len(lead) + [(0, 0), (0, padded - width)])
    return w.reshape(lead + (H_C * padded,))


def kernel(x, c, ctx, c_ctx, norm_w, w_ada, b_ada, w_in, qn_a, kn_a, rpb_a, qn_b, kn_b, sink_b,
           qa_norm, kva_norm, w_qb, w_kvb, qn_c, kn_c, w_out):
    (cos2, sin2), (cos1, sin1) = _rope_tables()
    cvec = jnp.concatenate([c, c_ctx[None, :], jnp.zeros((MOD_ROWS - BATCH - 1, D_MODEL), f32)], axis=0)
    mod = _modulation(cvec, w_ada, b_ada)
    q_scale_ab = HD ** -0.5 * LOG2E
    q_scale_c = MLA_SCALE * LOG2E
    w_out_bf = w_out.astype(bf16)
    na_bias = _natten_bias(rpb_a)

    o_qa, o_ka, o_va, o_qb, o_kb, o_vb, o_cq, o_ckv, o_kpe, o_z = np.cumsum(
        (0, D_A, D_A, D_A, D_B, KV_B, KV_B, Q_LORA, KV_LORA, ROPE_DIM)).tolist()

    xc = ctx
    mod3_all = [mod[l].reshape(MOD_ROWS, 1, 3 * D_MODEL) for l in range(DEPTH)]
    h3 = _hnorm(x, xc, norm_w[0], mod3_all[0])
    for l in range(DEPTH):
        last = l == DEPTH - 1
        mod3 = mod3_all[l]
        h = h3.reshape(R_ALL, D_MODEL)
        wl = w_in[l]

        gain_hn = jnp.concatenate([jnp.tile(qn_a[l] * q_scale_ab, H_A), jnp.tile(kn_a[l], H_A),
                                   jnp.tile(qn_b[l] * q_scale_ab, H_B), jnp.tile(kn_b[l], HKV_B)])
        w_hn = jnp.concatenate([wl[:, o_qa:o_va], wl[:, o_qb:o_vb]], axis=1).astype(bf16)
        w_lora = jnp.pad(wl[:, o_cq:o_z], ((0, 0), (0, LANES - ROPE_DIM))).astype(bf16)
        wt_v = jnp.concatenate([wl[:, o_va:o_qb], wl[:, o_vb:o_cq]], axis=1).T.astype(bf16)
        qka, qkb, cqn, ckvn, kpe, vt_ab = _proj_main(h, w_hn, w_lora, wt_v, gain_hn, qa_norm[l], kva_norm[l],
                                                      cos2, sin2)
        sza, szb, szc = _proj_gate(h, wl[:, o_z:].astype(bf16))

        wq = _pad_heads(w_qb[l], QK_C, QC_PAD).astype(bf16)
        wkv = w_kvb[l].reshape(KV_LORA, H_C, NOPE_C + V_C)
        wkn = wkv[:, :, :NOPE_C].reshape(KV_LORA, H_C * NOPE_C).astype(bf16)
        wv = wkv[:, :, NOPE_C:].reshape(KV_LORA, H_C * V_C).astype(bf16)
        gq = _pad_heads(jnp.tile(qn_c[l] * q_scale_c, H_C), QK_C, QC_PAD).reshape(1, H_C * QC_PAD)
        gk = jnp.pad(kn_c[l], (0, QC_PAD - QK_C)).reshape(1, QC_PAD)
        qc, kc, vc = _mla_up(cqn, ckvn, kpe, wq, wkn, wv, gq, gk, cos1, sin1)

        def b3(a):
            return a.reshape(BATCH, S_ALL, a.shape[-1])

        qka, qkb, qc, kc, vc, sza, szb, szc = map(b3, (qka, qkb, qc, kc, vc, sza, szb, szc))

        ga = _natten(qka, vt_ab, na_bias, l, sza)
        gb = _swa(qkb, vt_ab, sink_b[l], szb)
        gc = _mla(qc, kc, vc, szc)
        if last:
            (x,) = _out_proj(x, ga, gb, gc, w_out_bf, l, mod3, False)
        else:
            ga_c, gb_c, gc_c = _ctx_attention(qka, qkb, vt_ab, qc, kc, vc, sink_b[l], sza, szb, szc)
            x, h3 = _out_proj(x, ga, gb, gc, w_out_bf, l, mod3, False,
                              next_norm=(norm_w[l + 1], mod3_all[l + 1], h3))
            xc, h3 = _out_proj(xc, ga_c, gb_c, gc_c, w_out_bf, l, mod3, True,
                               next_norm=(norm_w[l + 1], mod3_all[l + 1], h3))
    return x
```

```python
import functools
import math

import numpy as np
import jax
import jax.numpy as jnp
from jax import lax
from jax.experimental import pallas as pl
from jax.experimental.pallas import tpu as pltpu

D_MODEL = 2048
BATCH = 8
SEQ = 2048
DEPTH = 2
GRID_W = 64
CTX_LEN = 256
HD = 64
H_A = 8
D_A = H_A * HD
H_B = 12
HKV_B = 4
D_B = H_B * HD
KV_B = HKV_B * HD
H_C = 6
NOPE_C = 128
ROPE_DIM = 64
QK_C = NOPE_C + ROPE_DIM
V_C = 128
D_C = H_C * V_C
Q_LORA = 768
KV_LORA = 512
D_MIX = D_A + D_B + D_C
NA_ROWS = 8
NA_COLS = 16
SW_WINDOW = 128
ROPE_BASE = 10000.0
EPS = 1e-6
NEG_INF = -1e30

S_ALL = SEQ + CTX_LEN
R_ALL = BATCH * S_ALL
GRID_ROWS = SEQ // GRID_W
LANES = 128
MXU_DIM = 256
BF16_ROWS = 16
QC_PAD = 256
VT_ROWS = LANES + BF16_ROWS
MOD_ROWS = 16
VMEM_LIMIT = 48 * 1024 * 1024
LOG2E = math.log2(math.e)

bf16 = jnp.bfloat16
f32 = jnp.float32


def _cparams(*sem):
    return pltpu.CompilerParams(dimension_semantics=sem, vmem_limit_bytes=VMEM_LIMIT)


def _silu(v):
    return v / (1.0 + jnp.exp(-v))


def _lane_iota(shape):
    return lax.broadcasted_iota(jnp.int32, shape, len(shape) - 1)


def _rope_rotate(xn, cos, sin_signed):
    lane = _lane_iota(xn.shape)
    take_next = (lane % 32) < 16
    rot = jnp.where(take_next, pltpu.roll(xn, LANES - 16, 1), pltpu.roll(xn, 16, 1))
    return xn * cos + rot * sin_signed


def _dot_nt(a, b):
    return lax.dot_general(a, b, (((1,), (1,)), ((), ())), preferred_element_type=f32)


def _dot(a, b):
    return jnp.dot(a, b, preferred_element_type=f32)


def _mod_kernel(c_ref, w_ref, b_ref, o_ref):
    sc = _silu(c_ref[...]).astype(bf16)
    o_ref[0] = _dot(sc, w_ref[0].astype(bf16)) + b_ref[0]


def _modulation(cvec, w_ada, b_ada):
    tn = 1024
    n = 3 * D_MODEL
    return pl.pallas_call(
        _mod_kernel,
        grid=(DEPTH, n // tn),
        in_specs=[
            pl.BlockSpec((MOD_ROWS, D_MODEL), lambda l, j: (0, 0)),
            pl.BlockSpec((1, D_MODEL, tn), lambda l, j: (l, 0, j)),
            pl.BlockSpec((1, 1, tn), lambda l, j: (l, 0, j)),
        ],
        out_specs=pl.BlockSpec((1, MOD_ROWS, tn), lambda l, j: (l, 0, j)),
        out_shape=jax.ShapeDtypeStruct((DEPTH, MOD_ROWS, n), f32),
        compiler_params=_cparams("arbitrary", "arbitrary"),
        name="modulation",
    )(cvec, w_ada, b_ada.reshape(DEPTH, 1, n))


H_ROWS = 256
N_LAT_BLOCKS = SEQ // H_ROWS


def _modulated_norm(x, nw, mod):
    ms = jnp.mean(x * x, axis=-1, keepdims=True)
    y = x * lax.rsqrt(ms + EPS) * nw
    return y * (1.0 + mod[:, D_MODEL:2 * D_MODEL]) + mod[:, 0:D_MODEL]


def _hnorm_kernel(x_ref, ctx_ref, nw_ref, mod_ref, h_ref):
    s = pl.program_id(1)

    def body(src):
        h_ref[0] = _modulated_norm(src[0], nw_ref[...], mod_ref[0]).astype(bf16)

    @pl.when(s < N_LAT_BLOCKS)
    def _():
        body(x_ref)

    @pl.when(s >= N_LAT_BLOCKS)
    def _():
        body(ctx_ref)


def _hnorm(x, xc, norm_w, mod3):
    return pl.pallas_call(
        _hnorm_kernel,
        grid=(BATCH, S_ALL // H_ROWS),
        in_specs=[
            pl.BlockSpec((1, H_ROWS, D_MODEL), lambda b, s: (b, jnp.minimum(s, N_LAT_BLOCKS - 1), 0)),
            pl.BlockSpec((1, H_ROWS, D_MODEL), lambda b, s: (b, 0, 0)),
            pl.BlockSpec((1, D_MODEL), lambda b, s: (0, 0)),
            pl.BlockSpec((1, 1, 3 * D_MODEL), lambda b, s: (jnp.where(s < N_LAT_BLOCKS, b, BATCH), 0, 0)),
        ],
        out_specs=pl.BlockSpec((1, H_ROWS, D_MODEL), lambda b, s: (b, s, 0)),
        out_shape=jax.ShapeDtypeStruct((BATCH, S_ALL, D_MODEL), bf16),
        compiler_params=_cparams("arbitrary", "arbitrary"),
        name="hnorm",
    )(x, xc, norm_w.reshape(1, D_MODEL), mod3)


PROJ_TM = 768


def _group_mean_sq(u):
    sq = u * u
    low = _lane_iota((u.shape[0], LANES)) < HD
    parts = []
    for c in range(0, u.shape[1], LANES):
        t = sq[:, c:c + LANES]
        s_low = jnp.sum(jnp.where(low, t, 0.0), axis=-1, keepdims=True)
        s_high = jnp.sum(jnp.where(low, 0.0, t), axis=-1, keepdims=True)
        parts.append(jnp.where(low, s_low, s_high) * (1.0 / HD))
    return jnp.concatenate(parts, axis=1)


def _run_items(items):
    u = items[0][0]()
    for i, (_, epilogue) in enumerate(items):
        u_next = items[i + 1][0]() if i + 1 < len(items) else None
        epilogue(u)
        u = u_next


def _store_vt_tile(vt_ref, p, ut):
    vt_ref[p, 0:LANES, :] = ut.astype(bf16)
    vt_ref[p, LANES:VT_ROWS, :] = jnp.ones((VT_ROWS - LANES, ut.shape[1]), bf16)


PROJ_TN = 512
N_HN = 2 * D_A + D_B + KV_B
N_LORA = Q_LORA + KV_LORA + LANES
N_VT = D_A + KV_B


def _proj_main_kernel(h_ref, whn_ref, wl_ref, wvt_ref, ghn_ref, cos_ref, sin_ref, gcq_ref, gckv_ref,
                      qka_ref, qkb_ref, cqn_ref, ckvn_ref, kpe_ref, vt_ref):
    def hn_item(k):
        cols = slice(k * PROJ_TN, (k + 1) * PROJ_TN)
        rope = k * PROJ_TN >= 2 * D_A
        o_ref = qkb_ref if rope else qka_ref
        o0 = k * PROJ_TN - (2 * D_A if rope else 0)

        def epilogue(u):
            for c0 in range(0, PROJ_TN, MXU_DIM):
                uc = u[:, c0:c0 + MXU_DIM]
                ms = _group_mean_sq(uc)
                xn = uc * lax.rsqrt(ms + EPS) * ghn_ref[:, k * PROJ_TN + c0:k * PROJ_TN + c0 + MXU_DIM]
                if rope:
                    for c1 in range(0, MXU_DIM, LANES):
                        o_ref[:, o0 + c0 + c1:o0 + c0 + c1 + LANES] = _rope_rotate(
                            xn[:, c1:c1 + LANES], cos_ref[...], sin_ref[...]).astype(bf16)
                else:
                    o_ref[:, o0 + c0:o0 + c0 + MXU_DIM] = xn.astype(bf16)

        return lambda: _dot(h_ref[...], whn_ref[:, cols]), epilogue

    def rowrms_item(c0, width, gain_ref, o_ref):
        def epilogue(u):
            ms = jnp.mean(u * u, axis=-1, keepdims=True)
            o_ref[...] = (u * lax.rsqrt(ms + EPS) * gain_ref[...]).astype(bf16)

        return lambda: _dot(h_ref[...], wl_ref[:, c0:c0 + width]), epilogue

    def kpe_item():
        def epilogue(u):
            kpe_ref[...] = u

        return lambda: _dot(h_ref[...], wl_ref[:, Q_LORA + KV_LORA:N_LORA]), epilogue

    def vt_item(p):
        def epilogue(ut):
            _store_vt_tile(vt_ref, p, ut)

        return lambda: _dot_nt(wvt_ref[p * LANES:(p + 1) * LANES, :], h_ref[...]), epilogue

    _run_items([hn_item(k) for k in range(N_HN // PROJ_TN)]
               + [rowrms_item(0, Q_LORA, gcq_ref, cqn_ref), rowrms_item(Q_LORA, KV_LORA, gckv_ref, ckvn_ref),
                  kpe_item()]
               + [vt_item(p) for p in range(N_VT // LANES)])


def _resident(shape):
    return pl.BlockSpec(shape, lambda i: (0,) * len(shape), pipeline_mode=pl.Buffered(1))


def _proj_main(h2d, whn, wl, wvt, ghn, gcq, gckv, cos, sin):
    tm = PROJ_TM
    nt = S_ALL // tm
    row = lambda i: (i, 0)
    tiles = N_VT // LANES
    return pl.pallas_call(
        _proj_main_kernel,
        grid=(R_ALL // tm,),
        in_specs=[
            pl.BlockSpec((tm, D_MODEL), row),
            _resident((D_MODEL, N_HN)), _resident((D_MODEL, N_LORA)), _resident((N_VT, D_MODEL)),
            _resident((1, N_HN)),
            pl.BlockSpec((tm, LANES), lambda i: (i % nt, 0)),
            pl.BlockSpec((tm, LANES), lambda i: (i % nt, 0)),
            _resident((1, Q_LORA)), _resident((1, KV_LORA)),
        ],
        out_specs=[
            pl.BlockSpec((tm, 2 * D_A), row),
            pl.BlockSpec((tm, D_B + KV_B), row),
            pl.BlockSpec((tm, Q_LORA), row),
            pl.BlockSpec((tm, KV_LORA), row),
            pl.BlockSpec((tm, LANES), row),
            pl.BlockSpec((tiles, VT_ROWS, tm), lambda i: (0, 0, i)),
        ],
        out_shape=[
            jax.ShapeDtypeStruct((R_ALL, 2 * D_A), bf16),
            jax.ShapeDtypeStruct((R_ALL, D_B + KV_B), bf16),
            jax.ShapeDtypeStruct((R_ALL, Q_LORA), bf16),
            jax.ShapeDtypeStruct((R_ALL, KV_LORA), bf16),
            jax.ShapeDtypeStruct((R_ALL, LANES), f32),
            jax.ShapeDtypeStruct((tiles, VT_ROWS, R_ALL), bf16),
        ],
        compiler_params=_cparams("arbitrary"),
        name="proj_main",
    )(h2d, whn, wl, wvt, ghn.reshape(1, N_HN), cos, sin,
      gcq.reshape(1, Q_LORA), gckv.reshape(1, KV_LORA))


def _proj_gate_kernel(h_ref, wz_ref, sza_ref, szb_ref, szc_ref):
    outs = ((sza_ref, 0, D_A), (szb_ref, D_A, D_B), (szc_ref, D_A + D_B, D_C))

    def item(k):
        c0 = k * PROJ_TN

        def epilogue(u):
            sz = _silu(u)
            for o_ref, start, width in outs:
                lo, hi = max(c0, start), min(c0 + PROJ_TN, start + width)
                if lo < hi:
                    o_ref[:, lo - start:hi - start] = sz[:, lo - c0:hi - c0]

        return lambda: _dot(h_ref[...], wz_ref[:, c0:c0 + PROJ_TN]), epilogue

    _run_items([item(k) for k in range(D_MIX // PROJ_TN)])


def _proj_gate(h2d, wz):
    tm = PROJ_TM
    row = lambda i: (i, 0)
    return pl.pallas_call(
        _proj_gate_kernel,
        grid=(R_ALL // tm,),
        in_specs=[pl.BlockSpec((tm, D_MODEL), row), _resident((D_MODEL, D_MIX))],
        out_specs=[pl.BlockSpec((tm, D_A), row), pl.BlockSpec((tm, D_B), row), pl.BlockSpec((tm, D_C), row)],
        out_shape=[jax.ShapeDtypeStruct((R_ALL, D_A), f32), jax.ShapeDtypeStruct((R_ALL, D_B), f32),
                   jax.ShapeDtypeStruct((R_ALL, D_C), f32)],
        compiler_params=_cparams("arbitrary"),
        name="proj_gate",
    )(h2d, wz)


def _mla_up_kernel(cq_ref, ckv_ref, kpe_ref, wq_ref, wkn_ref, wv_ref, gq_ref, gk_ref,
                   cos_ref, sin_ref, qc_ref, kc_ref, vc_ref):
    cos = cos_ref[...]
    sin = sin_ref[...]
    kpe = kpe_ref[...]
    ss_pe = jnp.sum(kpe * kpe, axis=-1, keepdims=True)
    k_rot = _rope_rotate(kpe * gk_ref[:, LANES:], cos, sin)

    def q_head(h):
        cols = slice(h * QC_PAD, (h + 1) * QC_PAD)
        return lambda: _dot(cq_ref[...], wq_ref[:, cols]), functools.partial(q_epilogue, h)

    def q_epilogue(h, u):
        ms = jnp.sum(u * u, axis=-1, keepdims=True) * (1.0 / QK_C)
        xn = u * lax.rsqrt(ms + EPS) * gq_ref[:, h * QC_PAD:(h + 1) * QC_PAD]
        qc_ref[:, h * QC_PAD:h * QC_PAD + LANES] = xn[:, :LANES].astype(bf16)
        qc_ref[:, h * QC_PAD + LANES:(h + 1) * QC_PAD] = _rope_rotate(xn[:, LANES:], cos, sin).astype(bf16)

    def k_head(h):
        cols = slice(h * NOPE_C, (h + 1) * NOPE_C)
        return lambda: _dot(ckv_ref[...], wkn_ref[:, cols]), functools.partial(k_epilogue, h)

    def k_epilogue(h, u):
        ms = (jnp.sum(u * u, axis=-1, keepdims=True) + ss_pe) * (1.0 / QK_C)
        r = lax.rsqrt(ms + EPS)
        kc_ref[:, h * QC_PAD:h * QC_PAD + LANES] = (u * r * gk_ref[:, 0:LANES]).astype(bf16)
        kc_ref[:, h * QC_PAD + LANES:(h + 1) * QC_PAD] = (k_rot * r).astype(bf16)

    def v_head(h):
        def epilogue(u):
            vc_ref[:, h * VC_PAD:h * VC_PAD + V_C] = u.astype(bf16)
            vc_ref[:, h * VC_PAD + V_C:(h + 1) * VC_PAD] = jnp.ones((u.shape[0], VC_PAD - V_C), bf16)

        return lambda: _dot(ckv_ref[...], wv_ref[:, h * V_C:(h + 1) * V_C]), epilogue

    _run_items([q_head(h) for h in range(H_C)] + [k_head(h) for h in range(H_C)] + [v_head(h) for h in range(H_C)])


def _mla_up(cqn, ckvn, kpe, wq, wkn, wv, gq, gk, cos, sin):
    tm = PROJ_TM
    nt = S_ALL // tm
    row = lambda i: (i, 0)
    fixed = lambda i: (0, 0)
    return pl.pallas_call(
        _mla_up_kernel,
        grid=(R_ALL // tm,),
        in_specs=[
            pl.BlockSpec((tm, Q_LORA), row),
            pl.BlockSpec((tm, KV_LORA), row),
            pl.BlockSpec((tm, LANES), row),
            pl.BlockSpec((Q_LORA, H_C * QC_PAD), fixed),
            pl.BlockSpec((KV_LORA, H_C * NOPE_C), fixed),
            pl.BlockSpec((KV_LORA, H_C * V_C), fixed),
            pl.BlockSpec((1, H_C * QC_PAD), fixed),
            pl.BlockSpec((1, QC_PAD), fixed),
            pl.BlockSpec((tm, LANES), lambda i: (i % nt, 0)),
            pl.BlockSpec((tm, LANES), lambda i: (i % nt, 0)),
        ],
        out_specs=[
            pl.BlockSpec((tm, H_C * QC_PAD), row),
            pl.BlockSpec((tm, H_C * QC_PAD), row),
            pl.BlockSpec((tm, H_C * VC_PAD), row),
        ],
        out_shape=[
            jax.ShapeDtypeStruct((R_ALL, H_C * QC_PAD), bf16),
            jax.ShapeDtypeStruct((R_ALL, H_C * QC_PAD), bf16),
            jax.ShapeDtypeStruct((R_ALL, H_C * VC_PAD), bf16),
        ],
        compiler_params=_cparams("arbitrary"),
        name="mla_up",
    )(cqn, ckvn, kpe, wq, wkn, wv, gq, gk, cos, sin)


def _pipelined(n, score_fn, value_fn):
    m = score_fn(0)
    for j in range(n):
        m_next = score_fn(j + 1) if j + 1 < n else None
        value_fn(j, m)
        m = m_next


def _col_max(s):
    return jnp.max(s, axis=0, keepdims=True)


def _prob(s, m):
    return jnp.exp2((s - m).astype(bf16))


def _pair_q(qv, half_even, half_odd):
    low = _lane_iota(qv.shape) < HD
    zero = jnp.zeros_like(qv)
    qa = jnp.where(low, qv, zero) if half_even == 0 else jnp.where(low, zero, pltpu.roll(qv, HD, 1))
    qb = jnp.where(low, zero, qv) if half_odd == 1 else jnp.where(low, pltpu.roll(qv, HD, 1), zero)
    return jnp.concatenate([qa, qb], axis=0).astype(bf16)


def _pair_out(acc, l, n, half_even, half_odd):
    o = acc[:LANES] / l
    t = jnp.concatenate([o[half_even * HD:(half_even + 1) * HD, 0:n],
                         o[half_odd * HD:(half_odd + 1) * HD, n:2 * n]], axis=0)
    return t.T


NA_QROWS = 2
NA_Q = NA_QROWS * GRID_W
NA_KROWS = NA_ROWS + NA_QROWS
NA_KEYS = NA_KROWS * GRID_W
NA_VARIANTS = ((0, 0, 0, -1), (0, -2, 0, -3), (0, -4, 1, -5), (0, -4, 0, -5), (0, -6, 0, -7))


def _na_window_row(j):
    return jnp.clip(NA_QROWS * j - NA_ROWS // 2, 0, GRID_ROWS - NA_ROWS)


NA_SUB = 4
NA_BLOCKS = GRID_ROWS // NA_QROWS


def _natten_kernel(q_ref, k_ref, vt_ref, *rest):
    bias_refs = rest[:NA_SUB]
    sz_ref, o_ref, s_scr = rest[NA_SUB:]
    step = pl.program_id(1)
    k0 = [pl.multiple_of(_na_window_row(NA_SUB * step + s) * GRID_W, LANES) for s in range(NA_SUB)]
    n_pair = H_A // 2

    def scores(i):
        s, p = divmod(i, n_pair)
        cols = slice(p * LANES, (p + 1) * LANES)
        q2 = _pair_q(q_ref[0, s * NA_Q:(s + 1) * NA_Q, cols].astype(f32), 0, 1)
        s_loc = _dot_nt(k_ref[0, pl.ds(k0[s], NA_KEYS), cols], q2) + bias_refs[s][0, 0, p]
        s_ctx = _dot_nt(k_ref[0, SEQ:S_ALL, cols], q2)
        s_scr[i % 2, 0:NA_KEYS, :] = s_loc
        s_scr[i % 2, NA_KEYS:, :] = s_ctx
        return jnp.maximum(_col_max(s_loc), _col_max(s_ctx))

    def values(i, m):
        s, p = divmod(i, n_pair)
        cols = slice(p * LANES, (p + 1) * LANES)
        rows = slice(s * NA_Q, (s + 1) * NA_Q)
        acc = (_dot(vt_ref[p, :, pl.ds(k0[s], NA_KEYS)], _prob(s_scr[i % 2, 0:NA_KEYS, :], m))
               + _dot(vt_ref[p, :, SEQ:S_ALL], _prob(s_scr[i % 2, NA_KEYS:, :], m)))
        o = _pair_out(acc, acc[LANES:LANES + 1], NA_Q, 0, 1)
        o_ref[0, rows, cols] = (o * sz_ref[0, rows, cols]).astype(bf16)

    _pipelined(NA_SUB * n_pair, scores, values)


def _natten(qk, vt, bias, layer, sz):
    def variant(s):
        def index(b, step):
            j = NA_SUB * step + s
            v = jnp.where(j <= 1, j, jnp.where(j >= NA_BLOCKS - 2, j - (NA_BLOCKS - 5), 2))
            return (layer, v, 0, 0, 0)
        return index

    tq = NA_SUB * NA_Q
    bias_block = (1, 1, H_A // 2, NA_KEYS, 2 * NA_Q)
    return pl.pallas_call(
        _natten_kernel,
        grid=(BATCH, NA_BLOCKS // NA_SUB),
        in_specs=[
            pl.BlockSpec((1, tq, D_A), lambda b, j: (b, j, 0)),
            pl.BlockSpec((1, S_ALL, D_A), lambda b, j: (b, 0, 1)),
            pl.BlockSpec((H_A // 2, VT_ROWS, S_ALL), lambda b, j: (0, 0, b)),
            *[pl.BlockSpec(bias_block, variant(s)) for s in range(NA_SUB)],
            pl.BlockSpec((1, tq, D_A), lambda b, j: (b, j, 0)),
        ],
        out_specs=pl.BlockSpec((1, tq, D_A), lambda b, j: (b, j, 0)),
        out_shape=jax.ShapeDtypeStruct((BATCH, SEQ, D_A), bf16),
        scratch_shapes=[pltpu.VMEM((2, NA_KEYS + CTX_LEN, 2 * NA_Q), f32)],
        compiler_params=_cparams("arbitrary", "arbitrary"),
        name="natten",
    )(qk, qk, vt, *([bias] * NA_SUB), sz)


def _natten_bias(rpb):
    n_dc = 2 * NA_COLS - 1
    n_dr = NA_KROWS + NA_ROWS - 1
    c = np.arange(LANES) % GRID_W
    kc = np.arange(GRID_W)
    qstart = np.clip(c - NA_COLS // 2, 0, GRID_W - NA_COLS)
    col_ok = (kc[:, None] >= qstart[None, :]) & (kc[:, None] < qstart[None, :] + NA_COLS)
    dc = np.clip(kc[:, None] - c[None, :], -(NA_COLS - 1), NA_COLS - 1) + NA_COLS - 1
    onehot = jnp.asarray((dc[None] == np.arange(n_dc)[:, None, None]).astype(np.float32))
    exp = jnp.einsum("lhrd,dkc->lhrkc", rpb.astype(f32) * LOG2E, onehot, precision=lax.Precision.HIGHEST)
    exp = jnp.pad(exp, ((0, 0), (0, 0), (0, n_dr - exp.shape[2]), (0, 0), (0, 0)))
    col_mask = jnp.asarray(np.where(col_ok, 0.0, NEG_INF).astype(np.float32))
    n_var = len(NA_VARIANTS)
    return pl.pallas_call(
        _natten_bias_kernel,
        grid=(DEPTH, n_var),
        in_specs=[
            pl.BlockSpec((1, H_A, n_dr, GRID_W, LANES), lambda l, v: (l, 0, 0, 0, 0)),
            pl.BlockSpec((GRID_W, LANES), lambda l, v: (0, 0)),
        ],
        out_specs=pl.BlockSpec((1, 1, H_A // 2, NA_KEYS, 2 * NA_Q), lambda l, v: (l, v, 0, 0, 0)),
        out_shape=jax.ShapeDtypeStruct((DEPTH, n_var, H_A // 2, NA_KEYS, 2 * NA_Q), f32),
        compiler_params=_cparams("arbitrary", "arbitrary"),
        name="natten_bias",
    )(exp, col_mask)


def _natten_bias_kernel(exp_ref, mask_ref, o_ref):
    v = pl.program_id(1)

    def pick(column):
        out = jnp.int32(0)
        for i, var in enumerate(NA_VARIANTS):
            out = jnp.where(v == i, var[column], out)
        return out

    lo = (pick(0), pick(2))
    off = (pick(1), pick(3))
    low = _lane_iota((GRID_W, LANES)) < GRID_W
    mask = mask_ref[...]
    for kr in range(NA_KROWS):
        valid = [(kr >= lo[qr]) & (kr < lo[qr] + NA_ROWS) for qr in range(NA_QROWS)]
        for h in range(H_A):
            halves = []
            for qr in range(NA_QROWS):
                blk = exp_ref[0, h, pl.ds(kr + off[qr] + NA_ROWS - 1, 1)][0] + mask
                halves.append(jnp.where(valid[qr], blk, NEG_INF))
            e = h % 2
            o_ref[0, 0, h // 2, kr * GRID_W:(kr + 1) * GRID_W, e * NA_Q:(e + 1) * NA_Q] = jnp.where(
                low, halves[0], halves[1])


SW_Q = 128
SW_SPAN = SW_Q + 2 * SW_WINDOW
G_B = H_B // HKV_B
N_CHUNK_B = H_B // 2


def _gqa_halves(c):
    return ((2 * c) // G_B) % 2, ((2 * c + 1) // G_B) % 2


SW_SUB = 8
SW_BLOCKS = SEQ // SW_Q


def _swa_kernel(q_ref, k_ref, vt_ref, mask_ref, sink_ref, sz_ref, o_ref, s_scr):
    step = pl.program_id(1)
    blocks = [SW_SUB * step + s for s in range(SW_SUB)]
    start = [pl.multiple_of(jnp.clip((n - 1) * SW_Q, 0, SEQ - SW_SPAN), SW_Q) for n in blocks]
    variant = [jnp.where(n == 0, 0, jnp.where(n == SW_BLOCKS - 1, 2, 1)) for n in blocks]

    def scores(i):
        s, c = divmod(i, N_CHUNK_B)
        kp = c // G_B
        cols = slice(c * LANES, (c + 1) * LANES)
        kcols = slice(kp * LANES, (kp + 1) * LANES)
        q2 = _pair_q(q_ref[0, s * SW_Q:(s + 1) * SW_Q, cols].astype(f32), *_gqa_halves(c))
        s_loc = _dot_nt(k_ref[0, pl.ds(start[s], SW_SPAN), kcols], q2) + mask_ref[variant[s]]
        s_ctx = _dot_nt(k_ref[0, SEQ:S_ALL, kcols], q2)
        s_scr[i % 2, 0:SW_SPAN, :] = s_loc
        s_scr[i % 2, SW_SPAN:, :] = s_ctx
        return jnp.maximum(jnp.maximum(_col_max(s_loc), _col_max(s_ctx)), sink_ref[c])

    def values(i, m):
        s, c = divmod(i, N_CHUNK_B)
        kp = c // G_B
        cols = slice(c * LANES, (c + 1) * LANES)
        rows = slice(s * SW_Q, (s + 1) * SW_Q)
        acc = (_dot(vt_ref[kp, :, pl.ds(start[s], SW_SPAN)], _prob(s_scr[i % 2, 0:SW_SPAN, :], m))
               + _dot(vt_ref[kp, :, SEQ:S_ALL], _prob(s_scr[i % 2, SW_SPAN:, :], m)))
        l = acc[LANES:LANES + 1] + jnp.exp2(sink_ref[c] - m)
        o = _pair_out(acc, l, SW_Q, *_gqa_halves(c))
        o_ref[0, rows, cols] = (o * sz_ref[0, rows, cols]).astype(bf16)

    _pipelined(SW_SUB * N_CHUNK_B, scores, values)


def _swa_mask():
    i = np.arange(SW_Q)
    j = np.arange(SW_SPAN)
    out = []
    for d0 in (0, SW_Q, 2 * SW_Q):
        ok = np.abs(d0 + i[None, :] - j[:, None]) <= SW_WINDOW
        m = np.where(ok, 0.0, NEG_INF).astype(np.float32)
        out.append(np.concatenate([m, m], axis=1))
    return jnp.asarray(np.stack(out))


def _sink_cols(sink, n):
    return jnp.repeat(sink.astype(f32).reshape(N_CHUNK_B, 2) * LOG2E, n, axis=1)[:, None, :]


def _swa(qk, vt, sink, sz):
    tq = SW_SUB * SW_Q
    return pl.pallas_call(
        _swa_kernel,
        grid=(BATCH, SW_BLOCKS // SW_SUB),
        in_specs=[
            pl.BlockSpec((1, tq, D_B), lambda b, n: (b, n, 0)),
            pl.BlockSpec((1, S_ALL, KV_B), lambda b, n: (b, 0, D_B // KV_B)),
            pl.BlockSpec((HKV_B // 2, VT_ROWS, S_ALL), lambda b, n: (H_A // HKV_B, 0, b)),
            pl.BlockSpec((3, SW_SPAN, 2 * SW_Q), lambda b, n: (0, 0, 0)),
            pl.BlockSpec((N_CHUNK_B, 1, 2 * SW_Q), lambda b, n: (0, 0, 0)),
            pl.BlockSpec((1, tq, D_B), lambda b, n: (b, n, 0)),
        ],
        out_specs=pl.BlockSpec((1, tq, D_B), lambda b, n: (b, n, 0)),
        out_shape=jax.ShapeDtypeStruct((BATCH, SEQ, D_B), bf16),
        scratch_shapes=[pltpu.VMEM((2, SW_SPAN + CTX_LEN, 2 * SW_Q), f32)],
        compiler_params=_cparams("arbitrary", "arbitrary"),
        name="swa",
    )(qk, qk, vt, _swa_mask(), _sink_cols(sink, SW_Q), sz)


MLA_TQ = SEQ
MLA_SUB = 512
MLA_KTILE = MXU_DIM
MLA_SCALE = float(QK_C) ** -0.5
VC_PAD = 2 * V_C


def _mla_kernel(q_ref, k_ref, v_ref, sz_ref, o_ref, s_scr, p_scr):
    tiles = [slice(c * MLA_KTILE, (c + 1) * MLA_KTILE) for c in range(S_ALL // MLA_KTILE)]

    def scores(j):
        q = q_ref[0, j * MLA_SUB:(j + 1) * MLA_SUB, :]
        m = None
        for kt in tiles:
            s = _dot_nt(q, k_ref[0, kt, :])
            s_scr[j % 2, :, kt] = s
            mc = jnp.maximum(s[:, :LANES], s[:, LANES:])
            m = mc if m is None else jnp.maximum(m, mc)
        return jnp.max(m, axis=-1, keepdims=True)

    def values(j, m):
        for kt in tiles:
            p_scr[:, kt] = _prob(s_scr[j % 2, :, kt], m)
        acc = _dot(p_scr[...], v_ref[0])
        rows = slice(j * MLA_SUB, (j + 1) * MLA_SUB)
        o_ref[0, rows, :] = (acc[:, :V_C] / acc[:, V_C:] * sz_ref[0, rows, :]).astype(bf16)

    _pipelined(MLA_TQ // MLA_SUB, scores, values)


def _mla(qc, kc, vc, sz):
    return pl.pallas_call(
        _mla_kernel,
        grid=(BATCH, H_C),
        in_specs=[
            pl.BlockSpec((1, MLA_TQ, QC_PAD), lambda b, h: (b, 0, h)),
            pl.BlockSpec((1, S_ALL, QC_PAD), lambda b, h: (b, 0, h)),
            pl.BlockSpec((1, S_ALL, VC_PAD), lambda b, h: (b, 0, h)),
            pl.BlockSpec((1, MLA_TQ, V_C), lambda b, h: (b, 0, h)),
        ],
        out_specs=pl.BlockSpec((1, MLA_TQ, V_C), lambda b, h: (b, 0, h)),
        out_shape=jax.ShapeDtypeStruct((BATCH, SEQ, D_C), bf16),
        scratch_shapes=[pltpu.VMEM((2, MLA_SUB, S_ALL), f32), pltpu.VMEM((MLA_SUB, S_ALL), bf16)],
        compiler_params=_cparams("arbitrary", "arbitrary"),
        name="mla",
    )(qc, kc, vc, sz)


def _ctx_kernel(qa_ref, ka_ref, qb_ref, kb_ref, vtab_ref, qc_ref, kc_ref, vc_ref,
                sink_ref, sza_ref, szb_ref, szc_ref, oa_ref, ob_ref, oc_ref):
    n = CTX_LEN

    def item_a(p):
        cols = slice(p * LANES, (p + 1) * LANES)

        def values(s):
            acc = _dot(vtab_ref[p], _prob(s, _col_max(s)))
            o = _pair_out(acc, acc[LANES:LANES + 1], n, 0, 1)
            oa_ref[0, :, cols] = (o * sza_ref[0, :, cols]).astype(bf16)

        return lambda: _dot_nt(ka_ref[0, :, cols], _pair_q(qa_ref[0, :, cols].astype(f32), 0, 1)), values

    def item_b(c):
        kp = c // G_B
        cols = slice(c * LANES, (c + 1) * LANES)

        def values(s):
            m = jnp.maximum(_col_max(s), sink_ref[c])
            acc = _dot(vtab_ref[H_A // 2 + kp], _prob(s, m))
            l = acc[LANES:LANES + 1] + jnp.exp2(sink_ref[c] - m)
            o = _pair_out(acc, l, n, *_gqa_halves(c))
            ob_ref[0, :, cols] = (o * szb_ref[0, :, cols]).astype(bf16)

        return lambda: _dot_nt(kb_ref[0, :, kp * LANES:(kp + 1) * LANES],
                               _pair_q(qb_ref[0, :, cols].astype(f32), *_gqa_halves(c))), values

    def item_c(h):
        qcols = slice(h * QC_PAD, (h + 1) * QC_PAD)
        vcols = slice(h * V_C, (h + 1) * V_C)

        def values(s):
            p = _prob(s, jnp.max(s, axis=-1, keepdims=True))
            acc = _dot(p, vc_ref[0, :, h * VC_PAD:(h + 1) * VC_PAD])
            oc_ref[0, :, vcols] = (acc[:, :V_C] / acc[:, V_C:] * szc_ref[0, :, vcols]).astype(bf16)

        return lambda: _dot_nt(qc_ref[0, :, qcols], kc_ref[0, :, qcols]), values

    _run_items([item_a(p) for p in range(H_A // 2)] + [item_b(c) for c in range(N_CHUNK_B)]
               + [item_c(h) for h in range(H_C)])


def _ctx_attention(qka, qkb, vt_ab, qc, kc, vc, sink, sza, szb, szc):
    cb = SEQ // CTX_LEN

    def spec(width, col):
        return pl.BlockSpec((1, CTX_LEN, width), lambda b: (b, cb, col))

    def vt_spec(tiles):
        return pl.BlockSpec((tiles, VT_ROWS, CTX_LEN), lambda b: (0, 0, b * (S_ALL // CTX_LEN) + cb))

    return pl.pallas_call(
        _ctx_kernel,
        grid=(BATCH,),
        in_specs=[
            spec(D_A, 0), spec(D_A, 1),
            spec(D_B, 0), spec(KV_B, D_B // KV_B),
            vt_spec((H_A + HKV_B) // 2),
            spec(H_C * QC_PAD, 0), spec(H_C * QC_PAD, 0), spec(H_C * VC_PAD, 0),
            pl.BlockSpec((N_CHUNK_B, 1, 2 * CTX_LEN), lambda b: (0, 0, 0)),
            spec(D_A, 0), spec(D_B, 0), spec(D_C, 0),
        ],
        out_specs=[
            pl.BlockSpec((1, CTX_LEN, D_A), lambda b: (b, 0, 0)),
            pl.BlockSpec((1, CTX_LEN, D_B), lambda b: (b, 0, 0)),
            pl.BlockSpec((1, CTX_LEN, D_C), lambda b: (b, 0, 0)),
        ],
        out_shape=[
            jax.ShapeDtypeStruct((BATCH, CTX_LEN, D_A), bf16),
            jax.ShapeDtypeStruct((BATCH, CTX_LEN, D_B), bf16),
            jax.ShapeDtypeStruct((BATCH, CTX_LEN, D_C), bf16),
        ],
        compiler_params=_cparams("arbitrary"),
        name="ctx_attention",
    )(qka, qka, qkb, qkb, vt_ab, qc, kc, vc, _sink_cols(sink, CTX_LEN), sza, szb, szc)


OUT_TM = 512


def _out_kernel(next_norm, x_ref, ga_ref, gb_ref, gc_ref, w_ref, mod_ref, *rest):
    g = jnp.concatenate([ga_ref[0], gb_ref[0], gc_ref[0]], axis=1)
    x_new = x_ref[0] + mod_ref[0, :, 2 * D_MODEL:] * _dot(g, w_ref[0])
    if next_norm:
        nw_ref, modn_ref = rest[0], rest[1]
        o_ref, h_ref = rest[-2], rest[-1]
        o_ref[0] = x_new
        h_ref[0] = _modulated_norm(x_new, nw_ref[...], modn_ref[0]).astype(bf16)
    else:
        rest[-1][0] = x_new


def _out_proj(x, ga, gb, gc, w_out_bf, layer, mod3, is_ctx, next_norm=None):
    t = x.shape[1]
    tm = min(t, OUT_TM)
    mod_row = (lambda b: BATCH) if is_ctx else (lambda b: b)
    row_block0 = SEQ // tm if is_ctx else 0
    in_specs = [
        pl.BlockSpec((1, tm, D_MODEL), lambda b, i: (b, i, 0)),
        pl.BlockSpec((1, tm, D_A), lambda b, i: (b, i, 0)),
        pl.BlockSpec((1, tm, D_B), lambda b, i: (b, i, 0)),
        pl.BlockSpec((1, tm, D_C), lambda b, i: (b, i, 0)),
        pl.BlockSpec((1, D_MIX, D_MODEL), lambda b, i: (layer, 0, 0), pipeline_mode=pl.Buffered(1)),
        pl.BlockSpec((1, 1, 3 * D_MODEL), lambda b, i: (mod_row(b), 0, 0)),
    ]
    args = [x, ga, gb, gc, w_out_bf, mod3]
    out_specs = [pl.BlockSpec((1, tm, D_MODEL), lambda b, i: (b, i, 0))]
    out_shape = [jax.ShapeDtypeStruct(x.shape, f32)]
    aliases = {}
    if next_norm is not None:
        norm_w, mod3_next, h_buf = next_norm
        in_specs += [pl.BlockSpec((1, D_MODEL), lambda b, i: (0, 0)),
                     pl.BlockSpec((1, 1, 3 * D_MODEL), lambda b, i: (mod_row(b), 0, 0))]
        args += [norm_w.reshape(1, D_MODEL), mod3_next]
        if h_buf is not None:
            in_specs.append(pl.BlockSpec(memory_space=pl.ANY))
            args.append(h_buf)
            aliases = {len(args) - 1: 1}
        out_specs.append(pl.BlockSpec((1, tm, D_MODEL), lambda b, i: (b, row_block0 + i, 0)))
        out_shape.append(jax.ShapeDtypeStruct((BATCH, S_ALL, D_MODEL), bf16))
    return pl.pallas_call(
        functools.partial(_out_kernel, next_norm is not None),
        grid=(BATCH, t // tm),
        in_specs=in_specs,
        out_specs=out_specs,
        out_shape=out_shape,
        input_output_aliases=aliases,
        compiler_params=_cparams("arbitrary", "arbitrary"),
        name="out_proj_ctx" if is_ctx else "out_proj",
    )(*args)


def _rope_tables():
    t = jnp.arange(SEQ)
    row = (t // GRID_W).astype(f32)
    col = (t % GRID_W).astype(f32)
    n_freq = ROPE_DIM // 4
    inv = ROPE_BASE ** (-jnp.arange(n_freq, dtype=f32) / n_freq)
    ar = row[:, None] * inv
    ac = col[:, None] * inv
    ang = jnp.concatenate([ar, ar, ac, ac], axis=-1)
    cos = jnp.cos(ang).astype(f32)
    sin = jnp.sin(ang).astype(f32)
    sign = jnp.asarray(np.where((np.arange(ROPE_DIM) % 32) < 16, -1.0, 1.0), dtype=f32)
    sin = sin * sign
    cos = jnp.concatenate([cos, jnp.ones((CTX_LEN, ROPE_DIM), f32)], axis=0)
    sin = jnp.concatenate([sin, jnp.zeros((CTX_LEN, ROPE_DIM), f32)], axis=0)
    pair = (jnp.concatenate([cos, cos], axis=1), jnp.concatenate([sin, sin], axis=1))
    single = (jnp.concatenate([cos, jnp.ones_like(cos)], axis=1), jnp.concatenate([sin, jnp.zeros_like(sin)], axis=1))
    return pair, single


def _pad_heads(w, width, padded):
    lead = w.shape[:-1]
    w = w.reshape(lead + (H_C, width))
    w = jnp.pad(w, [(0, 0)] * len(lead) + [(0, 0), (0, padded - width)])
    return w.reshape(lead + (H_C * padded,))


def kernel(x, c, ctx, c_ctx, norm_w, w_ada, b_ada, w_in, qn_a, kn_a, rpb_a, qn_b, kn_b, sink_b,
           qa_norm, kva_norm, w_qb, w_kvb, qn_c, kn_c, w_out):
    (cos2, sin2), (cos1, sin1) = _rope_tables()
    cvec = jnp.concatenate([c, c_ctx[None, :], jnp.zeros((MOD_ROWS - BATCH - 1, D_MODEL), f32)], axis=0)
    mod = _modulation(cvec, w_ada, b_ada)
    q_scale_ab = HD ** -0.5 * LOG2E
    q_scale_c = MLA_SCALE * LOG2E
    w_out_bf = w_out.astype(bf16)
    na_bias = _natten_bias(rpb_a)

    o_qa, o_ka, o_va, o_qb, o_kb, o_vb, o_cq, o_ckv, o_kpe, o_z = np.cumsum(
        (0, D_A, D_A, D_A, D_B, KV_B, KV_B, Q_LORA, KV_LORA, ROPE_DIM)).tolist()

    xc = ctx
    mod3_all = [mod[l].reshape(MOD_ROWS, 1, 3 * D_MODEL) for l in range(DEPTH)]
    h3 = _hnorm(x, xc, norm_w[0], mod3_all[0])
    for l in range(DEPTH):
        last = l == DEPTH - 1
        mod3 = mod3_all[l]
        h = h3.reshape(R_ALL, D_MODEL)
        wl = w_in[l]

        gain_hn = jnp.concatenate([jnp.tile(qn_a[l] * q_scale_ab, H_A), jnp.tile(kn_a[l], H_A),
                                   jnp.tile(qn_b[l] * q_scale_ab, H_B), jnp.tile(kn_b[l], HKV_B)])
        w_hn = jnp.concatenate([wl[:, o_qa:o_va], wl[:, o_qb:o_vb]], axis=1).astype(bf16)
        w_lora = jnp.pad(wl[:, o_cq:o_z], ((0, 0), (0, LANES - ROPE_DIM))).astype(bf16)
        wt_v = jnp.concatenate([wl[:, o_va:o_qb], wl[:, o_vb:o_cq]], axis=1).T.astype(bf16)
        qka, qkb, cqn, ckvn, kpe, vt_ab = _proj_main(h, w_hn, w_lora, wt_v, gain_hn, qa_norm[l], kva_norm[l],
                                                      cos2, sin2)
        sza, szb, szc = _proj_gate(h, wl[:, o_z:].astype(bf16))

        wq = _pad_heads(w_qb[l], QK_C, QC_PAD).astype(bf16)
        wkv = w_kvb[l].reshape(KV_LORA, H_C, NOPE_C + V_C)
        wkn = wkv[:, :, :NOPE_C].reshape(KV_LORA, H_C * NOPE_C).astype(bf16)
        wv = wkv[:, :, NOPE_C:].reshape(KV_LORA, H_C * V_C).astype(bf16)
        gq = _pad_heads(jnp.tile(qn_c[l] * q_scale_c, H_C), QK_C, QC_PAD).reshape(1, H_C * QC_PAD)
        gk = jnp.pad(kn_c[l], (0, QC_PAD - QK_C)).reshape(1, QC_PAD)
        qc, kc, vc = _mla_up(cqn, ckvn, kpe, wq, wkn, wv, gq, gk, cos1, sin1)

        def b3(a):
            return a.reshape(BATCH, S_ALL, a.shape[-1])

        qka, qkb, qc, kc, vc, sza, szb, szc = map(b3, (qka, qkb, qc, kc, vc, sza, szb, szc))

        ga = _natten(qka, vt_ab, na_bias, l, sza)
        gb = _swa(qkb, vt_ab, sink_b[l], szb)
        gc = _mla(qc, kc, vc, szc)
        if last:
            (x,) = _out_proj(x, ga, gb, gc, w_out_bf, l, mod3, False)
        else:
            ga_c, gb_c, gc_c = _ctx_attention(qka, qkb, vt_ab, qc, kc, vc, sink_b[l], sza, szb, szc)
            x, h3 = _out_proj(x, ga, gb, gc, w_out_bf, l, mod3, False,
                              next_norm=(norm_w[l + 1], mod3_all[l + 1], h3))
            xc, h3 = _out_proj(xc, ga_c, gb_c, gc_c, w_out_bf, l, mod3, True,
                               next_norm=(norm_w[l + 1], mod3_all[l + 1], h3))
    return x
```

```python
import functools
import math

import numpy as np
import jax
import jax.numpy as jnp
from jax import lax
from jax.experimental import pallas as pl
from jax.experimental.pallas import tpu as pltpu

D_MODEL = 2048
BATCH = 8
SEQ = 2048
DEPTH = 2
GRID_W = 64
CTX_LEN = 256
HD = 64
H_A = 8
D_A = H_A * HD
H_B = 12
HKV_B = 4
D_B = H_B * HD
KV_B = HKV_B * HD
H_C = 6
NOPE_C = 128
ROPE_DIM = 64
QK_C = NOPE_C + ROPE_DIM
V_C = 128
D_C = H_C * V_C
Q_LORA = 768
KV_LORA = 512
D_MIX = D_A + D_B + D_C
NA_ROWS = 8
NA_COLS = 16
SW_WINDOW = 128
ROPE_BASE = 10000.0
EPS = 1e-6
NEG_INF = -1e30

S_ALL = SEQ + CTX_LEN
R_ALL = BATCH * S_ALL
GRID_ROWS = SEQ // GRID_W
LANES = 128
MXU_DIM = 256
BF16_ROWS = 16
QC_PAD = 256
VT_ROWS = LANES + BF16_ROWS
MOD_ROWS = 16
VMEM_LIMIT = 48 * 1024 * 1024
LOG2E = math.log2(math.e)

bf16 = jnp.bfloat16
f32 = jnp.float32


def _cparams(*sem):
    return pltpu.CompilerParams(dimension_semantics=sem, vmem_limit_bytes=VMEM_LIMIT)


def _silu(v):
    return v / (1.0 + jnp.exp(-v))


def _lane_iota(shape):
    return lax.broadcasted_iota(jnp.int32, shape, len(shape) - 1)


def _rope_rotate(xn, cos, sin_signed):
    lane = _lane_iota(xn.shape)
    take_next = (lane % 32) < 16
    rot = jnp.where(take_next, pltpu.roll(xn, LANES - 16, 1), pltpu.roll(xn, 16, 1))
    return xn * cos + rot * sin_signed


def _dot_nt(a, b):
    return lax.dot_general(a, b, (((1,), (1,)), ((), ())), preferred_element_type=f32)


def _dot(a, b):
    return jnp.dot(a, b, preferred_element_type=f32)


def _mod_kernel(c_ref, w_ref, b_ref, o_ref):
    sc = _silu(c_ref[...]).astype(bf16)
    o_ref[0] = _dot(sc, w_ref[0].astype(bf16)) + b_ref[0]


def _modulation(cvec, w_ada, b_ada):
    tn = 1024
    n = 3 * D_MODEL
    return pl.pallas_call(
        _mod_kernel,
        grid=(DEPTH, n // tn),
        in_specs=[
            pl.BlockSpec((MOD_ROWS, D_MODEL), lambda l, j: (0, 0)),
            pl.BlockSpec((1, D_MODEL, tn), lambda l, j: (l, 0, j)),
            pl.BlockSpec((1, 1, tn), lambda l, j: (l, 0, j)),
        ],
        out_specs=pl.BlockSpec((1, MOD_ROWS, tn), lambda l, j: (l, 0, j)),
        out_shape=jax.ShapeDtypeStruct((DEPTH, MOD_ROWS, n), f32),
        compiler_params=_cparams("arbitrary", "arbitrary"),
        name="modulation",
    )(cvec, w_ada, b_ada.reshape(DEPTH, 1, n))


H_ROWS = 256
N_LAT_BLOCKS = SEQ // H_ROWS


def _modulated_norm(x, nw, mod):
    ms = jnp.mean(x * x, axis=-1, keepdims=True)
    y = x * lax.rsqrt(ms + EPS) * nw
    return y * (1.0 + mod[:, D_MODEL:2 * D_MODEL]) + mod[:, 0:D_MODEL]


def _hnorm_kernel(x_ref, ctx_ref, nw_ref, mod_ref, h_ref):
    s = pl.program_id(1)

    def body(src):
        h_ref[0] = _modulated_norm(src[0], nw_ref[...], mod_ref[0]).astype(bf16)

    @pl.when(s < N_LAT_BLOCKS)
    def _():
        body(x_ref)

    @pl.when(s >= N_LAT_BLOCKS)
    def _():
        body(ctx_ref)


def _hnorm(x, xc, norm_w, mod3):
    return pl.pallas_call(
        _hnorm_kernel,
        grid=(BATCH, S_ALL // H_ROWS),
        in_specs=[
            pl.BlockSpec((1, H_ROWS, D_MODEL), lambda b, s: (b, jnp.minimum(s, N_LAT_BLOCKS - 1), 0)),
            pl.BlockSpec((1, H_ROWS, D_MODEL), lambda b, s: (b, 0, 0)),
            pl.BlockSpec((1, D_MODEL), lambda b, s: (0, 0)),
            pl.BlockSpec((1, 1, 3 * D_MODEL), lambda b, s: (jnp.where(s < N_LAT_BLOCKS, b, BATCH), 0, 0)),
        ],
        out_specs=pl.BlockSpec((1, H_ROWS, D_MODEL), lambda b, s: (b, s, 0)),
        out_shape=jax.ShapeDtypeStruct((BATCH, S_ALL, D_MODEL), bf16),
        compiler_params=_cparams("arbitrary", "arbitrary"),
        name="hnorm",
    )(x, xc, norm_w.reshape(1, D_MODEL), mod3)


PROJ_TM = 768


def _group_mean_sq(u):
    sq = u * u
    low = _lane_iota((u.shape[0], LANES)) < HD
    parts = []
    for c in range(0, u.shape[1], LANES):
        t = sq[:, c:c + LANES]
        s_low = jnp.sum(jnp.where(low, t, 0.0), axis=-1, keepdims=True)
        s_high = jnp.sum(jnp.where(low, 0.0, t), axis=-1, keepdims=True)
        parts.append(jnp.where(low, s_low, s_high) * (1.0 / HD))
    return jnp.concatenate(parts, axis=1)


def _run_items(items):
    u = items[0][0]()
    for i, (_, epilogue) in enumerate(items):
        u_next = items[i + 1][0]() if i + 1 < len(items) else None
        epilogue(u)
        u = u_next


def _store_vt_tile(vt_ref, p, ut):
    vt_ref[p, 0:LANES, :] = ut.astype(bf16)
    vt_ref[p, LANES:VT_ROWS, :] = jnp.ones((VT_ROWS - LANES, ut.shape[1]), bf16)


PROJ_TN = 512
N_HN = 2 * D_A + D_B + KV_B
N_LORA = Q_LORA + KV_LORA + LANES
N_VT = D_A + KV_B


def _proj_main_kernel(h_ref, whn_ref, wl_ref, wvt_ref, ghn_ref, cos_ref, sin_ref, gcq_ref, gckv_ref,
                      qka_ref, qkb_ref, cqn_ref, ckvn_ref, kpe_ref, vt_ref):
    def hn_item(k):
        cols = slice(k * PROJ_TN, (k + 1) * PROJ_TN)
        rope = k * PROJ_TN >= 2 * D_A
        o_ref = qkb_ref if rope else qka_ref
        o0 = k * PROJ_TN - (2 * D_A if rope else 0)

        def epilogue(u):
            for c0 in range(0, PROJ_TN, MXU_DIM):
                uc = u[:, c0:c0 + MXU_DIM]
                ms = _group_mean_sq(uc)
                xn = uc * lax.rsqrt(ms + EPS) * ghn_ref[:, k * PROJ_TN + c0:k * PROJ_TN + c0 + MXU_DIM]
                if rope:
                    for c1 in range(0, MXU_DIM, LANES):
                        o_ref[:, o0 + c0 + c1:o0 + c0 + c1 + LANES] = _rope_rotate(
                            xn[:, c1:c1 + LANES], cos_ref[...], sin_ref[...]).astype(bf16)
                else:
                    o_ref[:, o0 + c0:o0 + c0 + MXU_DIM] = xn.astype(bf16)

        return lambda: _dot(h_ref[...], whn_ref[:, cols]), epilogue

    def rowrms_item(c0, width, gain_ref, o_ref):
        def epilogue(u):
            ms = jnp.mean(u * u, axis=-1, keepdims=True)
            o_ref[...] = (u * lax.rsqrt(ms + EPS) * gain_ref[...]).astype(bf16)

        return lambda: _dot(h_ref[...], wl_ref[:, c0:c0 + width]), epilogue

    def kpe_item():
        def epilogue(u):
            kpe_ref[...] = u

        return lambda: _dot(h_ref[...], wl_ref[:, Q_LORA + KV_LORA:N_LORA]), epilogue

    def vt_item(p):
        def epilogue(ut):
            _store_vt_tile(vt_ref, p, ut)

        return lambda: _dot_nt(wvt_ref[p * LANES:(p + 1) * LANES, :], h_ref[...]), epilogue

    _run_items([hn_item(k) for k in range(N_HN // PROJ_TN)]
               + [rowrms_item(0, Q_LORA, gcq_ref, cqn_ref), rowrms_item(Q_LORA, KV_LORA, gckv_ref, ckvn_ref),
                  kpe_item()]
               + [vt_item(p) for p in range(N_VT // LANES)])


def _resident(shape):
    return pl.BlockSpec(shape, lambda i: (0,) * len(shape), pipeline_mode=pl.Buffered(1))


def _proj_main(h2d, whn, wl, wvt, ghn, gcq, gckv, cos, sin):
    tm = PROJ_TM
    nt = S_ALL // tm
    row = lambda i: (i, 0)
    tiles = N_VT // LANES
    return pl.pallas_call(
        _proj_main_kernel,
        grid=(R_ALL // tm,),
        in_specs=[
            pl.BlockSpec((tm, D_MODEL), row),
            _resident((D_MODEL, N_HN)), _resident((D_MODEL, N_LORA)), _resident((N_VT, D_MODEL)),
            _resident((1, N_HN)),
            pl.BlockSpec((tm, LANES), lambda i: (i % nt, 0)),
            pl.BlockSpec((tm, LANES), lambda i: (i % nt, 0)),
            _resident((1, Q_LORA)), _resident((1, KV_LORA)),
        ],
        out_specs=[
            pl.BlockSpec((tm, 2 * D_A), row),
            pl.BlockSpec((tm, D_B + KV_B), row),
            pl.BlockSpec((tm, Q_LORA), row),
            pl.BlockSpec((tm, KV_LORA), row),
            pl.BlockSpec((tm, LANES), row),
            pl.BlockSpec((tiles, VT_ROWS, tm), lambda i: (0, 0, i)),
        ],
        out_shape=[
            jax.ShapeDtypeStruct((R_ALL, 2 * D_A), bf16),
            jax.ShapeDtypeStruct((R_ALL, D_B + KV_B), bf16),
            jax.ShapeDtypeStruct((R_ALL, Q_LORA), bf16),
            jax.ShapeDtypeStruct((R_ALL, KV_LORA), bf16),
            jax.ShapeDtypeStruct((R_ALL, LANES), f32),
            jax.ShapeDtypeStruct((tiles, VT_ROWS, R_ALL), bf16),
        ],
        compiler_params=_cparams("arbitrary"),
        name="proj_main",
    )(h2d, whn, wl, wvt, ghn.reshape(1, N_HN), cos, sin,
      gcq.reshape(1, Q_LORA), gckv.reshape(1, KV_LORA))


def _proj_gate_kernel(h_ref, wz_ref, sza_ref, szb_ref, szc_ref):
    outs = ((sza_ref, 0, D_A), (szb_ref, D_A, D_B), (szc_ref, D_A + D_B, D_C))

    def item(k):
        c0 = k * PROJ_TN

        def epilogue(u):
            sz = _silu(u)
            for o_ref, start, width in outs:
                lo, hi = max(c0, start), min(c0 + PROJ_TN, start + width)
                if lo < hi:
                    o_ref[:, lo - start:hi - start] = sz[:, lo - c0:hi - c0]

        return lambda: _dot(h_ref[...], wz_ref[:, c0:c0 + PROJ_TN]), epilogue

    _run_items([item(k) for k in range(D_MIX // PROJ_TN)])


def _proj_gate(h2d, wz):
    tm = PROJ_TM
    row = lambda i: (i, 0)
    return pl.pallas_call(
        _proj_gate_kernel,
        grid=(R_ALL // tm,),
        in_specs=[pl.BlockSpec((tm, D_MODEL), row), _resident((D_MODEL, D_MIX))],
        out_specs=[pl.BlockSpec((tm, D_A), row), pl.BlockSpec((tm, D_B), row), pl.BlockSpec((tm, D_C), row)],
        out_shape=[jax.ShapeDtypeStruct((R_ALL, D_A), f32), jax.ShapeDtypeStruct((R_ALL, D_B), f32),
                   jax.ShapeDtypeStruct((R_ALL, D_C), f32)],
        compiler_params=_cparams("arbitrary"),
        name="proj_gate",
    )(h2d, wz)


def _mla_up_kernel(cq_ref, ckv_ref, kpe_ref, wq_ref, wkn_ref, wv_ref, gq_ref, gk_ref,
                   cos_ref, sin_ref, qc_ref, kc_ref, vc_ref):
    cos = cos_ref[...]
    sin = sin_ref[...]
    kpe = kpe_ref[...]
    ss_pe = jnp.sum(kpe * kpe, axis=-1, keepdims=True)
    k_rot = _rope_rotate(kpe * gk_ref[:, LANES:], cos, sin)

    def q_head(h):
        cols = slice(h * QC_PAD, (h + 1) * QC_PAD)
        return lambda: _dot(cq_ref[...], wq_ref[:, cols]), functools.partial(q_epilogue, h)

    def q_epilogue(h, u):
        ms = jnp.sum(u * u, axis=-1, keepdims=True) * (1.0 / QK_C)
        xn = u * lax.rsqrt(ms + EPS) * gq_ref[:, h * QC_PAD:(h + 1) * QC_PAD]
        qc_ref[:, h * QC_PAD:h * QC_PAD + LANES] = xn[:, :LANES].astype(bf16)
        qc_ref[:, h * QC_PAD + LANES:(h + 1) * QC_PAD] = _rope_rotate(xn[:, LANES:], cos, sin).astype(bf16)

    def k_head(h):
        cols = slice(h * NOPE_C, (h + 1) * NOPE_C)
        return lambda: _dot(ckv_ref[...], wkn_ref[:, cols]), functools.partial(k_epilogue, h)

    def k_epilogue(h, u):
        ms = (jnp.sum(u * u, axis=-1, keepdims=True) + ss_pe) * (1.0 / QK_C)
        r = lax.rsqrt(ms + EPS)
        kc_ref[:, h * QC_PAD:h * QC_PAD + LANES] = (u * r * gk_ref[:, 0:LANES]).astype(bf16)
        kc_ref[:, h * QC_PAD + LANES:(h + 1) * QC_PAD] = (k_rot * r).astype(bf16)

    def v_head(h):
        def epilogue(u):
            vc_ref[:, h * VC_PAD:h * VC_PAD + V_C] = u.astype(bf16)
            vc_ref[:, h * VC_PAD + V_C:(h + 1) * VC_PAD] = jnp.ones((u.shape[0], VC_PAD - V_C), bf16)

        return lambda: _dot(ckv_ref[...], wv_ref[:, h * V_C:(h + 1) * V_C]), epilogue

    _run_items([q_head(h) for h in range(H_C)] + [k_head(h) for h in range(H_C)] + [v_head(h) for h in range(H_C)])


def _mla_up(cqn, ckvn, kpe, wq, wkn, wv, gq, gk, cos, sin):
    tm = PROJ_TM
    nt = S_ALL // tm
    row = lambda i: (i, 0)
    fixed = lambda i: (0, 0)
    return pl.pallas_call(
        _mla_up_kernel,
        grid=(R_ALL // tm,),
        in_specs=[
            pl.BlockSpec((tm, Q_LORA), row),
            pl.BlockSpec((tm, KV_LORA), row),
            pl.BlockSpec((tm, LANES), row),
            pl.BlockSpec((Q_LORA, H_C * QC_PAD), fixed),
            pl.BlockSpec((KV_LORA, H_C * NOPE_C), fixed),
            pl.BlockSpec((KV_LORA, H_C * V_C), fixed),
            pl.BlockSpec((1, H_C * QC_PAD), fixed),
            pl.BlockSpec((1, QC_PAD), fixed),
            pl.BlockSpec((tm, LANES), lambda i: (i % nt, 0)),
            pl.BlockSpec((tm, LANES), lambda i: (i % nt, 0)),
        ],
        out_specs=[
            pl.BlockSpec((tm, H_C * QC_PAD), row),
            pl.BlockSpec((tm, H_C * QC_PAD), row),
            pl.BlockSpec((tm, H_C * VC_PAD), row),
        ],
        out_shape=[
            jax.ShapeDtypeStruct((R_ALL, H_C * QC_PAD), bf16),
            jax.ShapeDtypeStruct((R_ALL, H_C * QC_PAD), bf16),
            jax.ShapeDtypeStruct((R_ALL, H_C * VC_PAD), bf16),
        ],
        compiler_params=_cparams("arbitrary"),
        name="mla_up",
    )(cqn, ckvn, kpe, wq, wkn, wv, gq, gk, cos, sin)


def _pipelined(n, score_fn, value_fn):
    _run_items([(functools.partial(score_fn, j), functools.partial(value_fn, j)) for j in range(n)])


def _col_max(s):
    return jnp.max(s, axis=0, keepdims=True)


def _prob(s, m):
    return jnp.exp2((s - m).astype(bf16))


def _pair_q(qv, half_even, half_odd):
    low = _lane_iota(qv.shape) < HD
    zero = jnp.zeros_like(qv)
    qa = jnp.where(low, qv, zero) if half_even == 0 else jnp.where(low, zero, pltpu.roll(qv, HD, 1))
    qb = jnp.where(low, zero, qv) if half_odd == 1 else jnp.where(low, pltpu.roll(qv, HD, 1), zero)
    return jnp.concatenate([qa, qb], axis=0).astype(bf16)


def _pair_out(acc, l, n, half_even, half_odd):
    o = acc[:LANES] / l
    t = jnp.concatenate([o[half_even * HD:(half_even + 1) * HD, 0:n],
                         o[half_odd * HD:(half_odd + 1) * HD, n:2 * n]], axis=0)
    return t.T


NA_QROWS = 2
NA_Q = NA_QROWS * GRID_W
NA_KROWS = NA_ROWS + NA_QROWS
NA_KEYS = NA_KROWS * GRID_W
NA_VARIANTS = ((0, 0, 0, -1), (0, -2, 0, -3), (0, -4, 1, -5), (0, -4, 0, -5), (0, -6, 0, -7))


def _na_window_row(j):
    return jnp.clip(NA_QROWS * j - NA_ROWS // 2, 0, GRID_ROWS - NA_ROWS)


NA_SUB = 4
NA_BLOCKS = GRID_ROWS // NA_QROWS


def _natten_kernel(q_ref, k_ref, vt_ref, *rest):
    bias_refs = rest[:NA_SUB]
    sz_ref, o_ref, s_scr = rest[NA_SUB:]
    step = pl.program_id(1)
    k0 = [pl.multiple_of(_na_window_row(NA_SUB * step + s) * GRID_W, LANES) for s in range(NA_SUB)]
    n_pair = H_A // 2

    def scores(i):
        s, p = divmod(i, n_pair)
        cols = slice(p * LANES, (p + 1) * LANES)
        q2 = _pair_q(q_ref[0, s * NA_Q:(s + 1) * NA_Q, cols].astype(f32), 0, 1)
        s_loc = _dot_nt(k_ref[0, pl.ds(k0[s], NA_KEYS), cols], q2) + bias_refs[s][0, 0, p]
        s_ctx = _dot_nt(k_ref[0, SEQ:S_ALL, cols], q2)
        s_scr[i % 2, 0:NA_KEYS, :] = s_loc
        s_scr[i % 2, NA_KEYS:, :] = s_ctx
        return jnp.maximum(_col_max(s_loc), _col_max(s_ctx))

    def values(i, m):
        s, p = divmod(i, n_pair)
        cols = slice(p * LANES, (p + 1) * LANES)
        rows = slice(s * NA_Q, (s + 1) * NA_Q)
        acc = (_dot(vt_ref[p, :, pl.ds(k0[s], NA_KEYS)], _prob(s_scr[i % 2, 0:NA_KEYS, :], m))
               + _dot(vt_ref[p, :, SEQ:S_ALL], _prob(s_scr[i % 2, NA_KEYS:, :], m)))
        o = _pair_out(acc, acc[LANES:LANES + 1], NA_Q, 0, 1)
        o_ref[0, rows, cols] = (o * sz_ref[0, rows, cols]).astype(bf16)

    _pipelined(NA_SUB * n_pair, scores, values)


def _natten(qk, vt, bias, layer, sz):
    def variant(s):
        def index(b, step):
            j = NA_SUB * step + s
            v = jnp.where(j <= 1, j, jnp.where(j >= NA_BLOCKS - 2, j - (NA_BLOCKS - 5), 2))
            return (layer, v, 0, 0, 0)
        return index

    tq = NA_SUB * NA_Q
    bias_block = (1, 1, H_A // 2, NA_KEYS, 2 * NA_Q)
    return pl.pallas_call(
        _natten_kernel,
        grid=(BATCH, NA_BLOCKS // NA_SUB),
        in_specs=[
            pl.BlockSpec((1, tq, D_A), lambda b, j: (b, j, 0)),
            pl.BlockSpec((1, S_ALL, D_A), lambda b, j: (b, 0, 1)),
            pl.BlockSpec((H_A // 2, VT_ROWS, S_ALL), lambda b, j: (0, 0, b)),
            *[pl.BlockSpec(bias_block, variant(s)) for s in range(NA_SUB)],
            pl.BlockSpec((1, tq, D_A), lambda b, j: (b, j, 0)),
        ],
        out_specs=pl.BlockSpec((1, tq, D_A), lambda b, j: (b, j, 0)),
        out_shape=jax.ShapeDtypeStruct((BATCH, SEQ, D_A), bf16),
        scratch_shapes=[pltpu.VMEM((2, NA_KEYS + CTX_LEN, 2 * NA_Q), f32)],
        compiler_params=_cparams("arbitrary", "arbitrary"),
        name="natten",
    )(qk, qk, vt, *([bias] * NA_SUB), sz)


def _natten_bias(rpb):
    n_dc = 2 * NA_COLS - 1
    n_dr = NA_KROWS + NA_ROWS - 1
    c = np.arange(LANES) % GRID_W
    kc = np.arange(GRID_W)
    qstart = np.clip(c - NA_COLS // 2, 0, GRID_W - NA_COLS)
    col_ok = (kc[:, None] >= qstart[None, :]) & (kc[:, None] < qstart[None, :] + NA_COLS)
    dc = np.clip(kc[:, None] - c[None, :], -(NA_COLS - 1), NA_COLS - 1) + NA_COLS - 1
    onehot = jnp.asarray((dc[None] == np.arange(n_dc)[:, None, None]).astype(np.float32))
    exp = jnp.einsum("lhrd,dkc->lhrkc", rpb.astype(f32) * LOG2E, onehot, precision=lax.Precision.HIGHEST)
    exp = jnp.pad(exp, ((0, 0), (0, 0), (0, n_dr - exp.shape[2]), (0, 0), (0, 0)))
    col_mask = jnp.asarray(np.where(col_ok, 0.0, NEG_INF).astype(np.float32))
    n_var = len(NA_VARIANTS)
    return pl.pallas_call(
        _natten_bias_kernel,
        grid=(DEPTH, n_var),
        in_specs=[
            pl.BlockSpec((1, H_A, n_dr, GRID_W, LANES), lambda l, v: (l, 0, 0, 0, 0)),
            pl.BlockSpec((GRID_W, LANES), lambda l, v: (0, 0)),
        ],
        out_specs=pl.BlockSpec((1, 1, H_A // 2, NA_KEYS, 2 * NA_Q), lambda l, v: (l, v, 0, 0, 0)),
        out_shape=jax.ShapeDtypeStruct((DEPTH, n_var, H_A // 2, NA_KEYS, 2 * NA_Q), f32),
        compiler_params=_cparams("arbitrary", "arbitrary"),
        name="natten_bias",
    )(exp, col_mask)


def _natten_bias_kernel(exp_ref, mask_ref, o_ref):
    v = pl.program_id(1)

    def pick(column):
        out = jnp.int32(0)
        for i, var in enumerate(NA_VARIANTS):
            out = jnp.where(v == i, var[column], out)
        return out

    lo = (pick(0), pick(2))
    off = (pick(1), pick(3))
    low = _lane_iota((GRID_W, LANES)) < GRID_W
    mask = mask_ref[...]
    for kr in range(NA_KROWS):
        valid = [(kr >= lo[qr]) & (kr < lo[qr] + NA_ROWS) for qr in range(NA_QROWS)]
        for h in range(H_A):
            halves = []
            for qr in range(NA_QROWS):
                blk = exp_ref[0, h, pl.ds(kr + off[qr] + NA_ROWS - 1, 1)][0] + mask
                halves.append(jnp.where(valid[qr], blk, NEG_INF))
            e = h % 2
            o_ref[0, 0, h // 2, kr * GRID_W:(kr + 1) * GRID_W, e * NA_Q:(e + 1) * NA_Q] = jnp.where(
                low, halves[0], halves[1])


SW_Q = 128
SW_SPAN = SW_Q + 2 * SW_WINDOW
G_B = H_B // HKV_B
N_CHUNK_B = H_B // 2


def _gqa_halves(c):
    return ((2 * c) // G_B) % 2, ((2 * c + 1) // G_B) % 2


SW_SUB = 8
SW_BLOCKS = SEQ // SW_Q


def _swa_kernel(q_ref, k_ref, vt_ref, mask_ref, sink_ref, sz_ref, o_ref, s_scr):
    step = pl.program_id(1)
    blocks = [SW_SUB * step + s for s in range(SW_SUB)]
    start = [pl.multiple_of(jnp.clip((n - 1) * SW_Q, 0, SEQ - SW_SPAN), SW_Q) for n in blocks]
    variant = [jnp.where(n == 0, 0, jnp.where(n == SW_BLOCKS - 1, 2, 1)) for n in blocks]

    def scores(i):
        s, c = divmod(i, N_CHUNK_B)
        kp = c // G_B
        cols = slice(c * LANES, (c + 1) * LANES)
        kcols = slice(kp * LANES, (kp + 1) * LANES)
        q2 = _pair_q(q_ref[0, s * SW_Q:(s + 1) * SW_Q, cols].astype(f32), *_gqa_halves(c))
        s_loc = _dot_nt(k_ref[0, pl.ds(start[s], SW_SPAN), kcols], q2) + mask_ref[variant[s]]
        s_ctx = _dot_nt(k_ref[0, SEQ:S_ALL, kcols], q2)
        s_scr[i % 2, 0:SW_SPAN, :] = s_loc
        s_scr[i % 2, SW_SPAN:, :] = s_ctx
        return jnp.maximum(jnp.maximum(_col_max(s_loc), _col_max(s_ctx)), sink_ref[c])

    def values(i, m):
        s, c = divmod(i, N_CHUNK_B)
        kp = c // G_B
        cols = slice(c * LANES, (c + 1) * LANES)
        rows = slice(s * SW_Q, (s + 1) * SW_Q)
        acc = (_dot(vt_ref[kp, :, pl.ds(start[s], SW_SPAN)], _prob(s_scr[i % 2, 0:SW_SPAN, :], m))
               + _dot(vt_ref[kp, :, SEQ:S_ALL], _prob(s_scr[i % 2, SW_SPAN:, :], m)))
        l = acc[LANES:LANES + 1] + jnp.exp2(sink_ref[c] - m)
        o = _pair_out(acc, l, SW_Q, *_gqa_halves(c))
        o_ref[0, rows, cols] = (o * sz_ref[0, rows, cols]).astype(bf16)

    _pipelined(SW_SUB * N_CHUNK_B, scores, values)


def _swa_mask():
    i = np.arange(SW_Q)
    j = np.arange(SW_SPAN)
    out = []
    for d0 in (0, SW_Q, 2 * SW_Q):
        ok = np.abs(d0 + i[None, :] - j[:, None]) <= SW_WINDOW
        m = np.where(ok, 0.0, NEG_INF).astype(np.float32)
        out.append(np.concatenate([m, m], axis=1))
    return jnp.asarray(np.stack(out))


def _sink_cols(sink, n):
    return jnp.repeat(sink.astype(f32).reshape(N_CHUNK_B, 2) * LOG2E, n, axis=1)[:, None, :]


def _swa(qk, vt, sink, sz):
    tq = SW_SUB * SW_Q
    return pl.pallas_call(
        _swa_kernel,
        grid=(BATCH, SW_BLOCKS // SW_SUB),
        in_specs=[
            pl.BlockSpec((1, tq, D_B), lambda b, n: (b, n, 0)),
            pl.BlockSpec((1, S_ALL, KV_B), lambda b, n: (b, 0, D_B // KV_B)),
            pl.BlockSpec((HKV_B // 2, VT_ROWS, S_ALL), lambda b, n: (H_A // HKV_B, 0, b)),
            pl.BlockSpec((3, SW_SPAN, 2 * SW_Q), lambda b, n: (0, 0, 0)),
            pl.BlockSpec((N_CHUNK_B, 1, 2 * SW_Q), lambda b, n: (0, 0, 0)),
            pl.BlockSpec((1, tq, D_B), lambda b, n: (b, n, 0)),
        ],
        out_specs=pl.BlockSpec((1, tq, D_B), lambda b, n: (b, n, 0)),
        out_shape=jax.ShapeDtypeStruct((BATCH, SEQ, D_B), bf16),
        scratch_shapes=[pltpu.VMEM((2, SW_SPAN + CTX_LEN, 2 * SW_Q), f32)],
        compiler_params=_cparams("arbitrary", "arbitrary"),
        name="swa",
    )(qk, qk, vt, _swa_mask(), _sink_cols(sink, SW_Q), sz)


MLA_TQ = SEQ
MLA_SUB = 512
MLA_KTILE = MXU_DIM
MLA_SCALE = float(QK_C) ** -0.5
VC_PAD = 2 * V_C


def _mla_kernel(q_ref, k_ref, v_ref, sz_ref, o_ref, s_scr, p_scr):
    tiles = [slice(c * MLA_KTILE, (c + 1) * MLA_KTILE) for c in range(S_ALL // MLA_KTILE)]

    def scores(j):
        q = q_ref[0, j * MLA_SUB:(j + 1) * MLA_SUB, :]
        m = None
        for kt in tiles:
            s = _dot_nt(q, k_ref[0, kt, :])
            s_scr[j % 2, :, kt] = s
            mc = jnp.maximum(s[:, :LANES], s[:, LANES:])
            m = mc if m is None else jnp.maximum(m, mc)
        return jnp.max(m, axis=-1, keepdims=True)

    def values(j, m):
        for kt in tiles:
            p_scr[:, kt] = _prob(s_scr[j % 2, :, kt], m)
        acc = _dot(p_scr[...], v_ref[0])
        rows = slice(j * MLA_SUB, (j + 1) * MLA_SUB)
        o_ref[0, rows, :] = (acc[:, :V_C] / acc[:, V_C:] * sz_ref[0, rows, :]).astype(bf16)

    _pipelined(MLA_TQ // MLA_SUB, scores, values)


def _mla(qc, kc, vc, sz):
    return pl.pallas_call(
        _mla_kernel,
        grid=(BATCH, H_C),
        in_specs=[
            pl.BlockSpec((1, MLA_TQ, QC_PAD), lambda b, h: (b, 0, h)),
            pl.BlockSpec((1, S_ALL, QC_PAD), lambda b, h: (b, 0, h)),
            pl.BlockSpec((1, S_ALL, VC_PAD), lambda b, h: (b, 0, h)),
            pl.BlockSpec((1, MLA_TQ, V_C), lambda b, h: (b, 0, h)),
        ],
        out_specs=pl.BlockSpec((1, MLA_TQ, V_C), lambda b, h: (b, 0, h)),
        out_shape=jax.ShapeDtypeStruct((BATCH, SEQ, D_C), bf16),
        scratch_shapes=[pltpu.VMEM((2, MLA_SUB, S_ALL), f32), pltpu.VMEM((MLA_SUB, S_ALL), bf16)],
        compiler_params=_cparams("arbitrary", "arbitrary"),
        name="mla",
    )(qc, kc, vc, sz)


def _ctx_kernel(qa_ref, ka_ref, qb_ref, kb_ref, vtab_ref, qc_ref, kc_ref, vc_ref,
                sink_ref, sza_ref, szb_ref, szc_ref, oa_ref, ob_ref, oc_ref):
    n = CTX_LEN

    def item_a(p):
        cols = slice(p * LANES, (p + 1) * LANES)

        def values(s):
            acc = _dot(vtab_ref[p], _prob(s, _col_max(s)))
            o = _pair_out(acc, acc[LANES:LANES + 1], n, 0, 1)
            oa_ref[0, :, cols] = (o * sza_ref[0, :, cols]).astype(bf16)

        return lambda: _dot_nt(ka_ref[0, :, cols], _pair_q(qa_ref[0, :, cols].astype(f32), 0, 1)), values

    def item_b(c):
        kp = c // G_B
        cols = slice(c * LANES, (c + 1) * LANES)

        def values(s):
            m = jnp.maximum(_col_max(s), sink_ref[c])
            acc = _dot(vtab_ref[H_A // 2 + kp], _prob(s, m))
            l = acc[LANES:LANES + 1] + jnp.exp2(sink_ref[c] - m)
            o = _pair_out(acc, l, n, *_gqa_halves(c))
            ob_ref[0, :, cols] = (o * szb_ref[0, :, cols]).astype(bf16)

        return lambda: _dot_nt(kb_ref[0, :, kp * LANES:(kp + 1) * LANES],
                               _pair_q(qb_ref[0, :, cols].astype(f32), *_gqa_halves(c))), values

    def item_c(h):
        qcols = slice(h * QC_PAD, (h + 1) * QC_PAD)
        vcols = slice(h * V_C, (h + 1) * V_C)

        def values(s):
            p = _prob(s, jnp.max(s, axis=-1, keepdims=True))
            acc = _dot(p, vc_ref[0, :, h * VC_PAD:(h + 1) * VC_PAD])
            oc_ref[0, :, vcols] = (acc[:, :V_C] / acc[:, V_C:] * szc_ref[0, :, vcols]).astype(bf16)

        return lambda: _dot_nt(qc_ref[0, :, qcols], kc_ref[0, :, qcols]), values

    _run_items([item_a(p) for p in range(H_A // 2)] + [item_b(c) for c in range(N_CHUNK_B)]
               + [item_c(h) for h in range(H_C)])


def _ctx_attention(qka, qkb, vt_ab, qc, kc, vc, sink, sza, szb, szc):
    cb = SEQ // CTX_LEN

    def spec(width, col):
        return pl.BlockSpec((1, CTX_LEN, width), lambda b: (b, cb, col))

    def vt_spec(tiles):
        return pl.BlockSpec((tiles, VT_ROWS, CTX_LEN), lambda b: (0, 0, b * (S_ALL // CTX_LEN) + cb))

    return pl.pallas_call(
        _ctx_kernel,
        grid=(BATCH,),
        in_specs=[
            spec(D_A, 0), spec(D_A, 1),
            spec(D_B, 0), spec(KV_B, D_B // KV_B),
            vt_spec((H_A + HKV_B) // 2),
            spec(H_C * QC_PAD, 0), spec(H_C * QC_PAD, 0), spec(H_C * VC_PAD, 0),
            pl.BlockSpec((N_CHUNK_B, 1, 2 * CTX_LEN), lambda b: (0, 0, 0)),
            spec(D_A, 0), spec(D_B, 0), spec(D_C, 0),
        ],
        out_specs=[
            pl.BlockSpec((1, CTX_LEN, D_A), lambda b: (b, 0, 0)),
            pl.BlockSpec((1, CTX_LEN, D_B), lambda b: (b, 0, 0)),
            pl.BlockSpec((1, CTX_LEN, D_C), lambda b: (b, 0, 0)),
        ],
        out_shape=[
            jax.ShapeDtypeStruct((BATCH, CTX_LEN, D_A), bf16),
            jax.ShapeDtypeStruct((BATCH, CTX_LEN, D_B), bf16),
            jax.ShapeDtypeStruct((BATCH, CTX_LEN, D_C), bf16),
        ],
        compiler_params=_cparams("arbitrary"),
        name="ctx_attention",
    )(qka, qka, qkb, qkb, vt_ab, qc, kc, vc, _sink_cols(sink, CTX_LEN), sza, szb, szc)


OUT_TM = 512


def _out_kernel(next_norm, x_ref, ga_ref, gb_ref, gc_ref, w_ref, mod_ref, *rest):
    g = jnp.concatenate([ga_ref[0], gb_ref[0], gc_ref[0]], axis=1)
    x_new = x_ref[0] + mod_ref[0, :, 2 * D_MODEL:] * _dot(g, w_ref[0])
    if next_norm:
        nw_ref, modn_ref = rest[0], rest[1]
        o_ref, h_ref = rest[-2], rest[-1]
        o_ref[0] = x_new
        h_ref[0] = _modulated_norm(x_new, nw_ref[...], modn_ref[0]).astype(bf16)
    else:
        rest[-1][0] = x_new


def _out_proj(x, ga, gb, gc, w_out_bf, layer, mod3, is_ctx, next_norm=None):
    t = x.shape[1]
    tm = min(t, OUT_TM)
    mod_row = (lambda b: BATCH) if is_ctx else (lambda b: b)
    row_block0 = SEQ // tm if is_ctx else 0
    in_specs = [
        pl.BlockSpec((1, tm, D_MODEL), lambda b, i: (b, i, 0)),
        pl.BlockSpec((1, tm, D_A), lambda b, i: (b, i, 0)),
        pl.BlockSpec((1, tm, D_B), lambda b, i: (b, i, 0)),
        pl.BlockSpec((1, tm, D_C), lambda b, i: (b, i, 0)),
        pl.BlockSpec((1, D_MIX, D_MODEL), lambda b, i: (layer, 0, 0), pipeline_mode=pl.Buffered(1)),
        pl.BlockSpec((1, 1, 3 * D_MODEL), lambda b, i: (mod_row(b), 0, 0)),
    ]
    args = [x, ga, gb, gc, w_out_bf, mod3]
    out_specs = [pl.BlockSpec((1, tm, D_MODEL), lambda b, i: (b, i, 0))]
    out_shape = [jax.ShapeDtypeStruct(x.shape, f32)]
    aliases = {}
    if next_norm is not None:
        norm_w, mod3_next, h_buf = next_norm
        in_specs += [pl.BlockSpec((1, D_MODEL), lambda b, i: (0, 0)),
                     pl.BlockSpec((1, 1, 3 * D_MODEL), lambda b, i: (mod_row(b), 0, 0))]
        args += [norm_w.reshape(1, D_MODEL), mod3_next]
        if h_buf is not None:
            in_specs.append(pl.BlockSpec(memory_space=pl.ANY))
            args.append(h_buf)
            aliases = {len(args) - 1: 1}
        out_specs.append(pl.BlockSpec((1, tm, D_MODEL), lambda b, i: (b, row_block0 + i, 0)))
        out_shape.append(jax.ShapeDtypeStruct((BATCH, S_ALL, D_MODEL), bf16))
    return pl.pallas_call(
        functools.partial(_out_kernel, next_norm is not None),
        grid=(BATCH, t // tm),
        in_specs=in_specs,
        out_specs=out_specs,
        out_shape=out_shape,
        input_output_aliases=aliases,
        compiler_params=_cparams("arbitrary", "arbitrary"),
        name="out_proj_ctx" if is_ctx else "out_proj",
    )(*args)


def _rope_tables():
    t = jnp.arange(SEQ)
    row = (t // GRID_W).astype(f32)
    col = (t % GRID_W).astype(f32)
    n_freq = ROPE_DIM // 4
    inv = ROPE_BASE ** (-jnp.arange(n_freq, dtype=f32) / n_freq)
    ar = row[:, None] * inv
    ac = col[:, None] * inv
    ang = jnp.concatenate([ar, ar, ac, ac], axis=-1)
    cos = jnp.cos(ang).astype(f32)
    sin = jnp.sin(ang).astype(f32)
    sign = jnp.asarray(np.where((np.arange(ROPE_DIM) % 32) < 16, -1.0, 1.0), dtype=f32)
    sin = sin * sign
    cos = jnp.concatenate([cos, jnp.ones((CTX_LEN, ROPE_DIM), f32)], axis=0)
    sin = jnp.concatenate([sin, jnp.zeros((CTX_LEN, ROPE_DIM), f32)], axis=0)
    pair = (jnp.concatenate([cos, cos], axis=1), jnp.concatenate([sin, sin], axis=1))
    single = (jnp.concatenate([cos, jnp.ones_like(cos)], axis=1), jnp.concatenate([sin, jnp.zeros_like(sin)], axis=1))
    return pair, single


def _pad_heads(w, width, padded):
    lead = w.shape[:-1]
    w = w.reshape(lead + (H_C, width))
    w = jnp.pad(w, [(0, 0)] * len(lead) + [(0, 0), (0, padded - width)])
    return w.reshape(lead + (H_C * padded,))


def kernel(x, c, ctx, c_ctx, norm_w, w_ada, b_ada, w_in, qn_a, kn_a, rpb_a, qn_b, kn_b, sink_b,
           qa_norm, kva_norm, w_qb, w_kvb, qn_c, kn_c, w_out):
    (cos2, sin2), (cos1, sin1) = _rope_tables()
    cvec = jnp.concatenate([c, c_ctx[None, :], jnp.zeros((MOD_ROWS - BATCH - 1, D_MODEL), f32)], axis=0)
    mod = _modulation(cvec, w_ada, b_ada)
    q_scale_ab = HD ** -0.5 * LOG2E
    q_scale_c = MLA_SCALE * LOG2E
    w_out_bf = w_out.astype(bf16)
    na_bias = _natten_bias(rpb_a)

    o_qa, o_ka, o_va, o_qb, o_kb, o_vb, o_cq, o_ckv, o_kpe, o_z = np.cumsum(
        (0, D_A, D_A, D_A, D_B, KV_B, KV_B, Q_LORA, KV_LORA, ROPE_DIM)).tolist()

    xc = ctx
    mod3_all = [mod[l].reshape(MOD_ROWS, 1, 3 * D_MODEL) for l in range(DEPTH)]
    h3 = _hnorm(x, xc, norm_w[0], mod3_all[0])
    for l in range(DEPTH):
        last = l == DEPTH - 1
        mod3 = mod3_all[l]
        h = h3.reshape(R_ALL, D_MODEL)
        wl = w_in[l]

        gain_hn = jnp.concatenate([jnp.tile(qn_a[l] * q_scale_ab, H_A), jnp.tile(kn_a[l], H_A),
                                   jnp.tile(qn_b[l] * q_scale_ab, H_B), jnp.tile(kn_b[l], HKV_B)])
        w_hn = jnp.concatenate([wl[:, o_qa:o_va], wl[:, o_qb:o_vb]], axis=1).astype(bf16)
        w_lora = jnp.pad(wl[:, o_cq:o_z], ((0, 0), (0, LANES - ROPE_DIM))).astype(bf16)
        wt_v = jnp.concatenate([wl[:, o_va:o_qb], wl[:, o_vb:o_cq]], axis=1).T.astype(bf16)
        qka, qkb, cqn, ckvn, kpe, vt_ab = _proj_main(h, w_hn, w_lora, wt_v, gain_hn, qa_norm[l], kva_norm[l],
                                                      cos2, sin2)
        sza, szb, szc = _proj_gate(h, wl[:, o_z:].astype(bf16))

        wq = _pad_heads(w_qb[l], QK_C, QC_PAD).astype(bf16)
        wkv = w_kvb[l].reshape(KV_LORA, H_C, NOPE_C + V_C)
        wkn = wkv[:, :, :NOPE_C].reshape(KV_LORA, H_C * NOPE_C).astype(bf16)
        wv = wkv[:, :, NOPE_C:].reshape(KV_LORA, H_C * V_C).astype(bf16)
        gq = _pad_heads(jnp.tile(qn_c[l] * q_scale_c, H_C), QK_C, QC_PAD).reshape(1, H_C * QC_PAD)
        gk = jnp.pad(kn_c[l], (0, QC_PAD - QK_C)).reshape(1, QC_PAD)
        qc, kc, vc = _mla_up(cqn, ckvn, kpe, wq, wkn, wv, gq, gk, cos1, sin1)

        def b3(a):
            return a.reshape(BATCH, S_ALL, a.shape[-1])

        qka, qkb, qc, kc, vc, sza, szb, szc = map(b3, (qka, qkb, qc, kc, vc, sza, szb, szc))

        ga = _natten(qka, vt_ab, na_bias, l, sza)
        gb = _swa(qkb, vt_ab, sink_b[l], szb)
        gc = _mla(qc, kc, vc, szc)
        if last:
            (x,) = _out_proj(x, ga, gb, gc, w_out_bf, l, mod3, False)
        else:
            ga_c, gb_c, gc_c = _ctx_attention(qka, qkb, vt_ab, qc, kc, vc, sink_b[l], sza, szb, szc)
            x, h3 = _out_proj(x, ga, gb, gc, w_out_bf, l, mod3, False,
                              next_norm=(norm_w[l + 1], mod3_all[l + 1], h3))
            xc, h3 = _out_proj(xc, ga_c, gb_c, gc_c, w_out_bf, l, mod3, True,
                               next_norm=(norm_w[l + 1], mod3_all[l + 1], h3))
    return x
```

```python
import functools
import math

import numpy as np
import jax
import jax.numpy as jnp
from jax import lax
from jax.experimental import pallas as pl
from jax.experimental.pallas import tpu as pltpu

D_MODEL = 2048
BATCH = 8
SEQ = 2048
DEPTH = 2
GRID_W = 64
CTX_LEN = 256
HD = 64
H_A = 8
D_A = H_A * HD
H_B = 12
HKV_B = 4
D_B = H_B * HD
KV_B = HKV_B * HD
H_C = 6
NOPE_C = 128
ROPE_DIM = 64
QK_C = NOPE_C + ROPE_DIM
V_C = 128
D_C = H_C * V_C
Q_LORA = 768
KV_LORA = 512
D_MIX = D_A + D_B + D_C
NA_ROWS = 8
NA_COLS = 16
SW_WINDOW = 128
ROPE_BASE = 10000.0
EPS = 1e-6
NEG_INF = -1e30

S_ALL = SEQ + CTX_LEN
R_ALL = BATCH * S_ALL
GRID_ROWS = SEQ // GRID_W
LANES = 128
MXU_DIM = 256
BF16_ROWS = 16
QC_PAD = 256
VT_ROWS = LANES + BF16_ROWS
MOD_ROWS = 16
VMEM_LIMIT = 48 * 1024 * 1024
LOG2E = math.log2(math.e)

bf16 = jnp.bfloat16
f32 = jnp.float32


def _cparams(*sem):
    return pltpu.CompilerParams(dimension_semantics=sem, vmem_limit_bytes=VMEM_LIMIT)


def _silu(v):
    return v / (1.0 + jnp.exp(-v))


def _lane_iota(shape):
    return lax.broadcasted_iota(jnp.int32, shape, len(shape) - 1)


def _rope_rotate(xn, cos, sin_signed):
    lane = _lane_iota(xn.shape)
    take_next = (lane % 32) < 16
    rot = jnp.where(take_next, pltpu.roll(xn, LANES - 16, 1), pltpu.roll(xn, 16, 1))
    return xn * cos + rot * sin_signed


def _dot_nt(a, b):
    return lax.dot_general(a, b, (((1,), (1,)), ((), ())), preferred_element_type=f32)


def _dot(a, b):
    return jnp.dot(a, b, preferred_element_type=f32)


def _mod_kernel(c_ref, w_ref, b_ref, o_ref):
    sc = _silu(c_ref[...]).astype(bf16)
    o_ref[0] = _dot(sc, w_ref[0].astype(bf16)) + b_ref[0]


def _modulation(cvec, w_ada, b_ada):
    tn = 1024
    n = 3 * D_MODEL
    return pl.pallas_call(
        _mod_kernel,
        grid=(DEPTH, n // tn),
        in_specs=[
            pl.BlockSpec((MOD_ROWS, D_MODEL), lambda l, j: (0, 0)),
            pl.BlockSpec((1, D_MODEL, tn), lambda l, j: (l, 0, j)),
            pl.BlockSpec((1, 1, tn), lambda l, j: (l, 0, j)),
        ],
        out_specs=pl.BlockSpec((1, MOD_ROWS, tn), lambda l, j: (l, 0, j)),
        out_shape=jax.ShapeDtypeStruct((DEPTH, MOD_ROWS, n), f32),
        compiler_params=_cparams("arbitrary", "arbitrary"),
        name="modulation",
    )(cvec, w_ada, b_ada.reshape(DEPTH, 1, n))


H_ROWS = CTX_LEN
N_LAT_BLOCKS = SEQ // H_ROWS


def _modulated_norm(x, nw, mod):
    ms = jnp.mean(x * x, axis=-1, keepdims=True)
    y = x * lax.rsqrt(ms + EPS) * nw
    return y * (1.0 + mod[:, D_MODEL:2 * D_MODEL]) + mod[:, 0:D_MODEL]


PROJ_TM = 768


def _group_mean_sq(u):
    sq = u * u
    low = _lane_iota((u.shape[0], LANES)) < HD
    parts = []
    for c in range(0, u.shape[1], LANES):
        t = sq[:, c:c + LANES]
        s_low = jnp.sum(jnp.where(low, t, 0.0), axis=-1, keepdims=True)
        s_high = jnp.sum(jnp.where(low, 0.0, t), axis=-1, keepdims=True)
        parts.append(jnp.where(low, s_low, s_high) * (1.0 / HD))
    return jnp.concatenate(parts, axis=1)


def _run_items(items):
    u = items[0][0]()
    for i, (_, epilogue) in enumerate(items):
        u_next = items[i + 1][0]() if i + 1 < len(items) else None
        epilogue(u)
        u = u_next


def _store_vt_tile(vt_ref, p, ut):
    vt_ref[p, 0:LANES, :] = ut.astype(bf16)
    vt_ref[p, LANES:VT_ROWS, :] = jnp.ones((VT_ROWS - LANES, ut.shape[1]), bf16)


PROJ_TN = 512
N_HN = 2 * D_A + D_B + KV_B
N_LORA = Q_LORA + KV_LORA + LANES
N_VT = D_A + KV_B


def _proj_main_kernel(h_ref, whn_ref, wl_ref, wvt_ref, ghn_ref, cos_ref, sin_ref, gcq_ref, gckv_ref,
                      qka_ref, qkb_ref, cqn_ref, ckvn_ref, kpe_ref, vt_ref):
    def hn_item(k):
        cols = slice(k * PROJ_TN, (k + 1) * PROJ_TN)
        rope = k * PROJ_TN >= 2 * D_A
        o_ref = qkb_ref if rope else qka_ref
        o0 = k * PROJ_TN - (2 * D_A if rope else 0)

        def epilogue(u):
            for c0 in range(0, PROJ_TN, MXU_DIM):
                uc = u[:, c0:c0 + MXU_DIM]
                ms = _group_mean_sq(uc)
                xn = uc * lax.rsqrt(ms + EPS) * ghn_ref[:, k * PROJ_TN + c0:k * PROJ_TN + c0 + MXU_DIM]
                if rope:
                    for c1 in range(0, MXU_DIM, LANES):
                        o_ref[:, o0 + c0 + c1:o0 + c0 + c1 + LANES] = _rope_rotate(
                            xn[:, c1:c1 + LANES], cos_ref[...], sin_ref[...]).astype(bf16)
                else:
                    o_ref[:, o0 + c0:o0 + c0 + MXU_DIM] = xn.astype(bf16)

        return lambda: _dot(h_ref[...], whn_ref[:, cols]), epilogue

    def rowrms_item(c0, width, gain_ref, o_ref):
        def epilogue(u):
            ms = jnp.mean(u * u, axis=-1, keepdims=True)
            o_ref[...] = (u * lax.rsqrt(ms + EPS) * gain_ref[...]).astype(bf16)

        return lambda: _dot(h_ref[...], wl_ref[:, c0:c0 + width]), epilogue

    def kpe_item():
        def epilogue(u):
            kpe_ref[...] = u

        return lambda: _dot(h_ref[...], wl_ref[:, Q_LORA + KV_LORA:N_LORA]), epilogue

    def vt_item(p):
        def epilogue(ut):
            _store_vt_tile(vt_ref, p, ut)

        return lambda: _dot_nt(wvt_ref[p * LANES:(p + 1) * LANES, :], h_ref[...]), epilogue

    _run_items([hn_item(k) for k in range(N_HN // PROJ_TN)]
               + [rowrms_item(0, Q_LORA, gcq_ref, cqn_ref), rowrms_item(Q_LORA, KV_LORA, gckv_ref, ckvn_ref),
                  kpe_item()]
               + [vt_item(p) for p in range(N_VT // LANES)])


def _resident(shape):
    return pl.BlockSpec(shape, lambda i: (0,) * len(shape), pipeline_mode=pl.Buffered(1))


def _proj_main(h2d, whn, wl, wvt, ghn, gcq, gckv, cos, sin):
    tm = PROJ_TM
    nt = S_ALL // tm
    row = lambda i: (i, 0)
    tiles = N_VT // LANES
    return pl.pallas_call(
        _proj_main_kernel,
        grid=(R_ALL // tm,),
        in_specs=[
            pl.BlockSpec((tm, D_MODEL), row),
            _resident((D_MODEL, N_HN)), _resident((D_MODEL, N_LORA)), _resident((N_VT, D_MODEL)),
            _resident((1, N_HN)),
            pl.BlockSpec((tm, LANES), lambda i: (i % nt, 0)),
            pl.BlockSpec((tm, LANES), lambda i: (i % nt, 0)),
            _resident((1, Q_LORA)), _resident((1, KV_LORA)),
        ],
        out_specs=[
            pl.BlockSpec((tm, 2 * D_A), row),
            pl.BlockSpec((tm, D_B + KV_B), row),
            pl.BlockSpec((tm, Q_LORA), row),
            pl.BlockSpec((tm, KV_LORA), row),
            pl.BlockSpec((tm, LANES), row),
            pl.BlockSpec((tiles, VT_ROWS, tm), lambda i: (0, 0, i)),
        ],
        out_shape=[
            jax.ShapeDtypeStruct((R_ALL, 2 * D_A), bf16),
            jax.ShapeDtypeStruct((R_ALL, D_B + KV_B), bf16),
            jax.ShapeDtypeStruct((R_ALL, Q_LORA), bf16),
            jax.ShapeDtypeStruct((R_ALL, KV_LORA), bf16),
            jax.ShapeDtypeStruct((R_ALL, LANES), f32),
            jax.ShapeDtypeStruct((tiles, VT_ROWS, R_ALL), bf16),
        ],
        compiler_params=_cparams("arbitrary"),
        name="proj_main",
    )(h2d, whn, wl, wvt, ghn.reshape(1, N_HN), cos, sin,
      gcq.reshape(1, Q_LORA), gckv.reshape(1, KV_LORA))


N_CHUNK_TILE = PROJ_TM // H_ROWS
TILES_PER_BATCH = S_ALL // PROJ_TM


def _proj_gate_norm_kernel(*refs):
    x_refs = refs[:N_CHUNK_TILE]
    ctx_ref, nw_ref, modl_ref, modc_ref, wz_ref, sza_ref, szb_ref, szc_ref, h_ref = refs[N_CHUNK_TILE:]
    last_tile = pl.program_id(0) % TILES_PER_BATCH == TILES_PER_BATCH - 1
    for j, x_ref in enumerate(x_refs):
        x, mod = x_ref[0], modl_ref[0]
        if j == N_CHUNK_TILE - 1:
            x = jnp.where(last_tile, ctx_ref[0], x)
            mod = jnp.where(last_tile, modc_ref[0], mod)
        h_ref[j * H_ROWS:(j + 1) * H_ROWS, :] = _modulated_norm(x, nw_ref[...], mod).astype(bf16)
    _proj_gate_kernel(h_ref, wz_ref, sza_ref, szb_ref, szc_ref)


def _proj_gate_kernel(h_ref, wz_ref, sza_ref, szb_ref, szc_ref):
    outs = ((sza_ref, 0, D_A), (szb_ref, D_A, D_B), (szc_ref, D_A + D_B, D_C))

    def item(k):
        c0 = k * PROJ_TN

        def epilogue(u):
            sz = _silu(u)
            for o_ref, start, width in outs:
                lo, hi = max(c0, start), min(c0 + PROJ_TN, start + width)
                if lo < hi:
                    o_ref[:, lo - start:hi - start] = sz[:, lo - c0:hi - c0]

        return lambda: _dot(h_ref[...], wz_ref[:, c0:c0 + PROJ_TN]), epilogue

    _run_items([item(k) for k in range(D_MIX // PROJ_TN)])


def _proj_gate_norm(x, xc, norm_w, mod3, wz):
    tm = PROJ_TM
    row = lambda i: (i, 0)

    def chunk(j):
        return pl.BlockSpec((1, H_ROWS, D_MODEL), lambda i: (
            i // TILES_PER_BATCH, jnp.minimum((i % TILES_PER_BATCH) * N_CHUNK_TILE + j, N_LAT_BLOCKS - 1), 0))

    return pl.pallas_call(
        _proj_gate_norm_kernel,
        grid=(R_ALL // tm,),
        in_specs=[
            *[chunk(j) for j in range(N_CHUNK_TILE)],
            pl.BlockSpec((1, H_ROWS, D_MODEL), lambda i: (i // TILES_PER_BATCH, 0, 0)),
            pl.BlockSpec((1, D_MODEL), lambda i: (0, 0)),
            pl.BlockSpec((1, 1, 3 * D_MODEL), lambda i: (i // TILES_PER_BATCH, 0, 0)),
            pl.BlockSpec((1, 1, 3 * D_MODEL), lambda i: (BATCH, 0, 0)),
            _resident((D_MODEL, D_MIX)),
        ],
        out_specs=[pl.BlockSpec((tm, D_A), row), pl.BlockSpec((tm, D_B), row), pl.BlockSpec((tm, D_C), row),
                   pl.BlockSpec((tm, D_MODEL), row)],
        out_shape=[jax.ShapeDtypeStruct((R_ALL, D_A), f32), jax.ShapeDtypeStruct((R_ALL, D_B), f32),
                   jax.ShapeDtypeStruct((R_ALL, D_C), f32), jax.ShapeDtypeStruct((R_ALL, D_MODEL), bf16)],
        compiler_params=_cparams("arbitrary"),
        name="proj_gate_norm",
    )(*([x] * N_CHUNK_TILE), xc, norm_w.reshape(1, D_MODEL), mod3, mod3, wz)


def _proj_gate(h2d, wz):
    tm = PROJ_TM
    row = lambda i: (i, 0)
    return pl.pallas_call(
        _proj_gate_kernel,
        grid=(R_ALL // tm,),
        in_specs=[pl.BlockSpec((tm, D_MODEL), row), _resident((D_MODEL, D_MIX))],
        out_specs=[pl.BlockSpec((tm, D_A), row), pl.BlockSpec((tm, D_B), row), pl.BlockSpec((tm, D_C), row)],
        out_shape=[jax.ShapeDtypeStruct((R_ALL, D_A), f32), jax.ShapeDtypeStruct((R_ALL, D_B), f32),
                   jax.ShapeDtypeStruct((R_ALL, D_C), f32)],
        compiler_params=_cparams("arbitrary"),
        name="proj_gate",
    )(h2d, wz)


def _mla_up_kernel(cq_ref, ckv_ref, kpe_ref, wq_ref, wkn_ref, wv_ref, gq_ref, gk_ref,
                   cos_ref, sin_ref, qc_ref, kc_ref, vc_ref):
    cos = cos_ref[...]
    sin = sin_ref[...]
    kpe = kpe_ref[...]
    ss_pe = jnp.sum(kpe * kpe, axis=-1, keepdims=True)
    k_rot = _rope_rotate(kpe * gk_ref[:, LANES:], cos, sin)

    def q_head(h):
        cols = slice(h * QC_PAD, (h + 1) * QC_PAD)
        return lambda: _dot(cq_ref[...], wq_ref[:, cols]), functools.partial(q_epilogue, h)

    def q_epilogue(h, u):
        ms = jnp.sum(u * u, axis=-1, keepdims=True) * (1.0 / QK_C)
        xn = u * lax.rsqrt(ms + EPS) * gq_ref[:, h * QC_PAD:(h + 1) * QC_PAD]
        qc_ref[:, h * QC_PAD:h * QC_PAD + LANES] = xn[:, :LANES].astype(bf16)
        qc_ref[:, h * QC_PAD + LANES:(h + 1) * QC_PAD] = _rope_rotate(xn[:, LANES:], cos, sin).astype(bf16)

    def k_head(h):
        cols = slice(h * NOPE_C, (h + 1) * NOPE_C)
        return lambda: _dot(ckv_ref[...], wkn_ref[:, cols]), functools.partial(k_epilogue, h)

    def k_epilogue(h, u):
        ms = (jnp.sum(u * u, axis=-1, keepdims=True) + ss_pe) * (1.0 / QK_C)
        r = lax.rsqrt(ms + EPS)
        kc_ref[:, h * QC_PAD:h * QC_PAD + LANES] = (u * r * gk_ref[:, 0:LANES]).astype(bf16)
        kc_ref[:, h * QC_PAD + LANES:(h + 1) * QC_PAD] = (k_rot * r).astype(bf16)

    def v_head(h):
        def epilogue(u):
            vc_ref[:, h * VC_PAD:h * VC_PAD + V_C] = u.astype(bf16)
            vc_ref[:, h * VC_PAD + V_C:(h + 1) * VC_PAD] = jnp.ones((u.shape[0], VC_PAD - V_C), bf16)

        return lambda: _dot(ckv_ref[...], wv_ref[:, h * V_C:(h + 1) * V_C]), epilogue

    _run_items([q_head(h) for h in range(H_C)] + [k_head(h) for h in range(H_C)] + [v_head(h) for h in range(H_C)])


def _mla_up(cqn, ckvn, kpe, wq, wkn, wv, gq, gk, cos, sin):
    tm = PROJ_TM
    nt = S_ALL // tm
    row = lambda i: (i, 0)
    fixed = lambda i: (0, 0)
    return pl.pallas_call(
        _mla_up_kernel,
        grid=(R_ALL // tm,),
        in_specs=[
            pl.BlockSpec((tm, Q_LORA), row),
            pl.BlockSpec((tm, KV_LORA), row),
            pl.BlockSpec((tm, LANES), row),
            pl.BlockSpec((Q_LORA, H_C * QC_PAD), fixed),
            pl.BlockSpec((KV_LORA, H_C * NOPE_C), fixed),
            pl.BlockSpec((KV_LORA, H_C * V_C), fixed),
            pl.BlockSpec((1, H_C * QC_PAD), fixed),
            pl.BlockSpec((1, QC_PAD), fixed),
            pl.BlockSpec((tm, LANES), lambda i: (i % nt, 0)),
            pl.BlockSpec((tm, LANES), lambda i: (i % nt, 0)),
        ],
        out_specs=[
            pl.BlockSpec((tm, H_C * QC_PAD), row),
            pl.BlockSpec((tm, H_C * QC_PAD), row),
            pl.BlockSpec((tm, H_C * VC_PAD), row),
        ],
        out_shape=[
            jax.ShapeDtypeStruct((R_ALL, H_C * QC_PAD), bf16),
            jax.ShapeDtypeStruct((R_ALL, H_C * QC_PAD), bf16),
            jax.ShapeDtypeStruct((R_ALL, H_C * VC_PAD), bf16),
        ],
        compiler_params=_cparams("arbitrary"),
        name="mla_up",
    )(cqn, ckvn, kpe, wq, wkn, wv, gq, gk, cos, sin)


def _pipelined(n, score_fn, value_fn):
    _run_items([(functools.partial(score_fn, j), functools.partial(value_fn, j)) for j in range(n)])


def _col_max(s):
    return jnp.max(s, axis=0, keepdims=True)


def _prob(s, m):
    return jnp.exp2((s - m).astype(bf16))


def _pair_q(qv, half_even, half_odd):
    low = _lane_iota(qv.shape) < HD
    zero = jnp.zeros_like(qv)
    qa = jnp.where(low, qv, zero) if half_even == 0 else jnp.where(low, zero, pltpu.roll(qv, HD, 1))
    qb = jnp.where(low, zero, qv) if half_odd == 1 else jnp.where(low, pltpu.roll(qv, HD, 1), zero)
    return jnp.concatenate([qa, qb], axis=0).astype(bf16)


def _pair_out(acc, l, n, half_even, half_odd):
    o = acc[:LANES] / l
    t = jnp.concatenate([o[half_even * HD:(half_even + 1) * HD, 0:n],
                         o[half_odd * HD:(half_odd + 1) * HD, n:2 * n]], axis=0)
    return t.T


NA_QROWS = 2
NA_Q = NA_QROWS * GRID_W
NA_KROWS = NA_ROWS + NA_QROWS
NA_KEYS = NA_KROWS * GRID_W
NA_VARIANTS = ((0, 0, 0, -1), (0, -2, 0, -3), (0, -4, 1, -5), (0, -4, 0, -5), (0, -6, 0, -7))


def _na_window_row(j):
    return jnp.clip(NA_QROWS * j - NA_ROWS // 2, 0, GRID_ROWS - NA_ROWS)


NA_SUB = 4
NA_BLOCKS = GRID_ROWS // NA_QROWS


def _natten_kernel(q_ref, k_ref, vt_ref, *rest):
    bias_refs = rest[:NA_SUB]
    sz_ref, o_ref, s_scr = rest[NA_SUB:]
    step = pl.program_id(1)
    k0 = [pl.multiple_of(_na_window_row(NA_SUB * step + s) * GRID_W, LANES) for s in range(NA_SUB)]
    n_pair = H_A // 2

    def scores(i):
        s, p = divmod(i, n_pair)
        cols = slice(p * LANES, (p + 1) * LANES)
        q2 = _pair_q(q_ref[0, s * NA_Q:(s + 1) * NA_Q, cols].astype(f32), 0, 1)
        s_loc = _dot_nt(k_ref[0, pl.ds(k0[s], NA_KEYS), cols], q2) + bias_refs[s][0, 0, p]
        s_ctx = _dot_nt(k_ref[0, SEQ:S_ALL, cols], q2)
        s_scr[i % 2, 0:NA_KEYS, :] = s_loc
        s_scr[i % 2, NA_KEYS:, :] = s_ctx
        return jnp.maximum(_col_max(s_loc), _col_max(s_ctx))

    def values(i, m):
        s, p = divmod(i, n_pair)
        cols = slice(p * LANES, (p + 1) * LANES)
        rows = slice(s * NA_Q, (s + 1) * NA_Q)
        acc = (_dot(vt_ref[p, :, pl.ds(k0[s], NA_KEYS)], _prob(s_scr[i % 2, 0:NA_KEYS, :], m))
               + _dot(vt_ref[p, :, SEQ:S_ALL], _prob(s_scr[i % 2, NA_KEYS:, :], m)))
        o = _pair_out(acc, acc[LANES:LANES + 1], NA_Q, 0, 1)
        o_ref[0, rows, cols] = (o * sz_ref[0, rows, cols]).astype(bf16)

    _pipelined(NA_SUB * n_pair, scores, values)


def _natten(qk, vt, bias, layer, sz):
    def variant(s):
        def index(b, step):
            j = NA_SUB * step + s
            v = jnp.where(j <= 1, j, jnp.where(j >= NA_BLOCKS - 2, j - (NA_BLOCKS - 5), 2))
            return (layer, v, 0, 0, 0)
        return index

    tq = NA_SUB * NA_Q
    bias_block = (1, 1, H_A // 2, NA_KEYS, 2 * NA_Q)
    return pl.pallas_call(
        _natten_kernel,
        grid=(BATCH, NA_BLOCKS // NA_SUB),
        in_specs=[
            pl.BlockSpec((1, tq, D_A), lambda b, j: (b, j, 0)),
            pl.BlockSpec((1, S_ALL, D_A), lambda b, j: (b, 0, 1)),
            pl.BlockSpec((H_A // 2, VT_ROWS, S_ALL), lambda b, j: (0, 0, b)),
            *[pl.BlockSpec(bias_block, variant(s)) for s in range(NA_SUB)],
            pl.BlockSpec((1, tq, D_A), lambda b, j: (b, j, 0)),
        ],
        out_specs=pl.BlockSpec((1, tq, D_A), lambda b, j: (b, j, 0)),
        out_shape=jax.ShapeDtypeStruct((BATCH, SEQ, D_A), bf16),
        scratch_shapes=[pltpu.VMEM((2, NA_KEYS + CTX_LEN, 2 * NA_Q), f32)],
        compiler_params=_cparams("arbitrary", "arbitrary"),
        name="natten",
    )(qk, qk, vt, *([bias] * NA_SUB), sz)


def _natten_bias(rpb):
    n_dc = 2 * NA_COLS - 1
    n_dr = NA_KROWS + NA_ROWS - 1
    c = np.arange(LANES) % GRID_W
    kc = np.arange(GRID_W)
    qstart = np.clip(c - NA_COLS // 2, 0, GRID_W - NA_COLS)
    col_ok = (kc[:, None] >= qstart[None, :]) & (kc[:, None] < qstart[None, :] + NA_COLS)
    dc = np.clip(kc[:, None] - c[None, :], -(NA_COLS - 1), NA_COLS - 1) + NA_COLS - 1
    onehot = jnp.asarray((dc[None] == np.arange(n_dc)[:, None, None]).astype(np.float32))
    exp = jnp.einsum("lhrd,dkc->lhrkc", rpb.astype(f32) * LOG2E, onehot, precision=lax.Precision.HIGHEST)
    exp = jnp.pad(exp, ((0, 0), (0, 0), (0, n_dr - exp.shape[2]), (0, 0), (0, 0)))
    col_mask = jnp.asarray(np.where(col_ok, 0.0, NEG_INF).astype(np.float32))
    n_var = len(NA_VARIANTS)
    return pl.pallas_call(
        _natten_bias_kernel,
        grid=(DEPTH, n_var),
        in_specs=[
            pl.BlockSpec((1, H_A, n_dr, GRID_W, LANES), lambda l, v: (l, 0, 0, 0, 0)),
            pl.BlockSpec((GRID_W, LANES), lambda l, v: (0, 0)),
        ],
        out_specs=pl.BlockSpec((1, 1, H_A // 2, NA_KEYS, 2 * NA_Q), lambda l, v: (l, v, 0, 0, 0)),
        out_shape=jax.ShapeDtypeStruct((DEPTH, n_var, H_A // 2, NA_KEYS, 2 * NA_Q), f32),
        compiler_params=_cparams("arbitrary", "arbitrary"),
        name="natten_bias",
    )(exp, col_mask)


def _natten_bias_kernel(exp_ref, mask_ref, o_ref):
    v = pl.program_id(1)

    def pick(column):
        out = jnp.int32(0)
        for i, var in enumerate(NA_VARIANTS):
            out = jnp.where(v == i, var[column], out)
        return out

    lo = (pick(0), pick(2))
    off = (pick(1), pick(3))
    low = _lane_iota((GRID_W, LANES)) < GRID_W
    mask = mask_ref[...]
    for kr in range(NA_KROWS):
        valid = [(kr >= lo[qr]) & (kr < lo[qr] + NA_ROWS) for qr in range(NA_QROWS)]
        for h in range(H_A):
            halves = []
            for qr in range(NA_QROWS):
                blk = exp_ref[0, h, pl.ds(kr + off[qr] + NA_ROWS - 1, 1)][0] + mask
                halves.append(jnp.where(valid[qr], blk, NEG_INF))
            e = h % 2
            o_ref[0, 0, h // 2, kr * GRID_W:(kr + 1) * GRID_W, e * NA_Q:(e + 1) * NA_Q] = jnp.where(
                low, halves[0], halves[1])


SW_Q = 128
SW_SPAN = SW_Q + 2 * SW_WINDOW
G_B = H_B // HKV_B
N_CHUNK_B = H_B // 2


def _gqa_halves(c):
    return ((2 * c) // G_B) % 2, ((2 * c + 1) // G_B) % 2


SW_SUB = 8
SW_BLOCKS = SEQ // SW_Q


def _swa_kernel(q_ref, k_ref, vt_ref, mask_ref, sink_ref, sz_ref, o_ref, s_scr):
    step = pl.program_id(1)
    blocks = [SW_SUB * step + s for s in range(SW_SUB)]
    start = [pl.multiple_of(jnp.clip((n - 1) * SW_Q, 0, SEQ - SW_SPAN), SW_Q) for n in blocks]
    variant = [jnp.where(n == 0, 0, jnp.where(n == SW_BLOCKS - 1, 2, 1)) for n in blocks]

    def scores(i):
        s, c = divmod(i, N_CHUNK_B)
        kp = c // G_B
        cols = slice(c * LANES, (c + 1) * LANES)
        kcols = slice(kp * LANES, (kp + 1) * LANES)
        q2 = _pair_q(q_ref[0, s * SW_Q:(s + 1) * SW_Q, cols].astype(f32), *_gqa_halves(c))
        s_loc = _dot_nt(k_ref[0, pl.ds(start[s], SW_SPAN), kcols], q2) + mask_ref[variant[s]]
        s_ctx = _dot_nt(k_ref[0, SEQ:S_ALL, kcols], q2)
        s_scr[i % 2, 0:SW_SPAN, :] = s_loc
        s_scr[i % 2, SW_SPAN:, :] = s_ctx
        return jnp.maximum(jnp.maximum(_col_max(s_loc), _col_max(s_ctx)), sink_ref[c])

    def values(i, m):
        s, c = divmod(i, N_CHUNK_B)
        kp = c // G_B
        cols = slice(c * LANES, (c + 1) * LANES)
        rows = slice(s * SW_Q, (s + 1) * SW_Q)
        acc = (_dot(vt_ref[kp, :, pl.ds(start[s], SW_SPAN)], _prob(s_scr[i % 2, 0:SW_SPAN, :], m))
               + _dot(vt_ref[kp, :, SEQ:S_ALL], _prob(s_scr[i % 2, SW_SPAN:, :], m)))
        l = acc[LANES:LANES + 1] + jnp.exp2(sink_ref[c] - m)
        o = _pair_out(acc, l, SW_Q, *_gqa_halves(c))
        o_ref[0, rows, cols] = (o * sz_ref[0, rows, cols]).astype(bf16)

    _pipelined(SW_SUB * N_CHUNK_B, scores, values)


def _swa_mask():
    i = np.arange(SW_Q)
    j = np.arange(SW_SPAN)
    out = []
    for d0 in (0, SW_Q, 2 * SW_Q):
        ok = np.abs(d0 + i[None, :] - j[:, None]) <= SW_WINDOW
        m = np.where(ok, 0.0, NEG_INF).astype(np.float32)
        out.append(np.concatenate([m, m], axis=1))
    return jnp.asarray(np.stack(out))


def _sink_cols(sink, n):
    return jnp.repeat(sink.astype(f32).reshape(N_CHUNK_B, 2) * LOG2E, n, axis=1)[:, None, :]


def _swa(qk, vt, sink, sz):
    tq = SW_SUB * SW_Q
    return pl.pallas_call(
        _swa_kernel,
        grid=(BATCH, SW_BLOCKS // SW_SUB),
        in_specs=[
            pl.BlockSpec((1, tq, D_B), lambda b, n: (b, n, 0)),
            pl.BlockSpec((1, S_ALL, KV_B), lambda b, n: (b, 0, D_B // KV_B)),
            pl.BlockSpec((HKV_B // 2, VT_ROWS, S_ALL), lambda b, n: (H_A // HKV_B, 0, b)),
            pl.BlockSpec((3, SW_SPAN, 2 * SW_Q), lambda b, n: (0, 0, 0)),
            pl.BlockSpec((N_CHUNK_B, 1, 2 * SW_Q), lambda b, n: (0, 0, 0)),
            pl.BlockSpec((1, tq, D_B), lambda b, n: (b, n, 0)),
        ],
        out_specs=pl.BlockSpec((1, tq, D_B), lambda b, n: (b, n, 0)),
        out_shape=jax.ShapeDtypeStruct((BATCH, SEQ, D_B), bf16),
        scratch_shapes=[pltpu.VMEM((2, SW_SPAN + CTX_LEN, 2 * SW_Q), f32)],
        compiler_params=_cparams("arbitrary", "arbitrary"),
        name="swa",
    )(qk, qk, vt, _swa_mask(), _sink_cols(sink, SW_Q), sz)


MLA_TQ = SEQ
MLA_SUB = 512
MLA_KTILE = MXU_DIM
MLA_SCALE = float(QK_C) ** -0.5
VC_PAD = 2 * V_C


def _mla_kernel(q_ref, k_ref, v_ref, sz_ref, o_ref, s_scr, p_scr):
    tiles = [slice(c * MLA_KTILE, (c + 1) * MLA_KTILE) for c in range(S_ALL // MLA_KTILE)]

    def scores(j):
        q = q_ref[0, j * MLA_SUB:(j + 1) * MLA_SUB, :]
        m = None
        for kt in tiles:
            s = _dot_nt(q, k_ref[0, kt, :])
            s_scr[j % 2, :, kt] = s
            mc = jnp.maximum(s[:, :LANES], s[:, LANES:])
            m = mc if m is None else jnp.maximum(m, mc)
        return jnp.max(m, axis=-1, keepdims=True)

    def values(j, m):
        for kt in tiles:
            p_scr[:, kt] = _prob(s_scr[j % 2, :, kt], m)
        acc = _dot(p_scr[...], v_ref[0])
        rows = slice(j * MLA_SUB, (j + 1) * MLA_SUB)
        o_ref[0, rows, :] = (acc[:, :V_C] / acc[:, V_C:] * sz_ref[0, rows, :]).astype(bf16)

    _pipelined(MLA_TQ // MLA_SUB, scores, values)


def _mla(qc, kc, vc, sz):
    return pl.pallas_call(
        _mla_kernel,
        grid=(BATCH, H_C),
        in_specs=[
            pl.BlockSpec((1, MLA_TQ, QC_PAD), lambda b, h: (b, 0, h)),
            pl.BlockSpec((1, S_ALL, QC_PAD), lambda b, h: (b, 0, h)),
            pl.BlockSpec((1, S_ALL, VC_PAD), lambda b, h: (b, 0, h)),
            pl.BlockSpec((1, MLA_TQ, V_C), lambda b, h: (b, 0, h)),
        ],
        out_specs=pl.BlockSpec((1, MLA_TQ, V_C), lambda b, h: (b, 0, h)),
        out_shape=jax.ShapeDtypeStruct((BATCH, SEQ, D_C), bf16),
        scratch_shapes=[pltpu.VMEM((2, MLA_SUB, S_ALL), f32), pltpu.VMEM((MLA_SUB, S_ALL), bf16)],
        compiler_params=_cparams("arbitrary", "arbitrary"),
        name="mla",
    )(qc, kc, vc, sz)


def _ctx_kernel(qa_ref, ka_ref, qb_ref, kb_ref, vtab_ref, qc_ref, kc_ref, vc_ref,
                sink_ref, sza_ref, szb_ref, szc_ref, oa_ref, ob_ref, oc_ref):
    n = CTX_LEN

    def item_a(p):
        cols = slice(p * LANES, (p + 1) * LANES)

        def values(s):
            acc = _dot(vtab_ref[p], _prob(s, _col_max(s)))
            o = _pair_out(acc, acc[LANES:LANES + 1], n, 0, 1)
            oa_ref[0, :, cols] = (o * sza_ref[0, :, cols]).astype(bf16)

        return lambda: _dot_nt(ka_ref[0, :, cols], _pair_q(qa_ref[0, :, cols].astype(f32), 0, 1)), values

    def item_b(c):
        kp = c // G_B
        cols = slice(c * LANES, (c + 1) * LANES)

        def values(s):
            m = jnp.maximum(_col_max(s), sink_ref[c])
            acc = _dot(vtab_ref[H_A // 2 + kp], _prob(s, m))
            l = acc[LANES:LANES + 1] + jnp.exp2(sink_ref[c] - m)
            o = _pair_out(acc, l, n, *_gqa_halves(c))
            ob_ref[0, :, cols] = (o * szb_ref[0, :, cols]).astype(bf16)

        return lambda: _dot_nt(kb_ref[0, :, kp * LANES:(kp + 1) * LANES],
                               _pair_q(qb_ref[0, :, cols].astype(f32), *_gqa_halves(c))), values

    def item_c(h):
        qcols = slice(h * QC_PAD, (h + 1) * QC_PAD)
        vcols = slice(h * V_C, (h + 1) * V_C)

        def values(s):
            p = _prob(s, jnp.max(s, axis=-1, keepdims=True))
            acc = _dot(p, vc_ref[0, :, h * VC_PAD:(h + 1) * VC_PAD])
            oc_ref[0, :, vcols] = (acc[:, :V_C] / acc[:, V_C:] * szc_ref[0, :, vcols]).astype(bf16)

        return lambda: _dot_nt(qc_ref[0, :, qcols], kc_ref[0, :, qcols]), values

    _run_items([item_a(p) for p in range(H_A // 2)] + [item_b(c) for c in range(N_CHUNK_B)]
               + [item_c(h) for h in range(H_C)])


def _ctx_attention(qka, qkb, vt_ab, qc, kc, vc, sink, sza, szb, szc):
    cb = SEQ // CTX_LEN

    def spec(width, col):
        return pl.BlockSpec((1, CTX_LEN, width), lambda b: (b, cb, col))

    def vt_spec(tiles):
        return pl.BlockSpec((tiles, VT_ROWS, CTX_LEN), lambda b: (0, 0, b * (S_ALL // CTX_LEN) + cb))

    return pl.pallas_call(
        _ctx_kernel,
        grid=(BATCH,),
        in_specs=[
            spec(D_A, 0), spec(D_A, 1),
            spec(D_B, 0), spec(KV_B, D_B // KV_B),
            vt_spec((H_A + HKV_B) // 2),
            spec(H_C * QC_PAD, 0), spec(H_C * QC_PAD, 0), spec(H_C * VC_PAD, 0),
            pl.BlockSpec((N_CHUNK_B, 1, 2 * CTX_LEN), lambda b: (0, 0, 0)),
            spec(D_A, 0), spec(D_B, 0), spec(D_C, 0),
        ],
        out_specs=[
            pl.BlockSpec((1, CTX_LEN, D_A), lambda b: (b, 0, 0)),
            pl.BlockSpec((1, CTX_LEN, D_B), lambda b: (b, 0, 0)),
            pl.BlockSpec((1, CTX_LEN, D_C), lambda b: (b, 0, 0)),
        ],
        out_shape=[
            jax.ShapeDtypeStruct((BATCH, CTX_LEN, D_A), bf16),
            jax.ShapeDtypeStruct((BATCH, CTX_LEN, D_B), bf16),
            jax.ShapeDtypeStruct((BATCH, CTX_LEN, D_C), bf16),
        ],
        compiler_params=_cparams("arbitrary"),
        name="ctx_attention",
    )(qka, qka, qkb, qkb, vt_ab, qc, kc, vc, _sink_cols(sink, CTX_LEN), sza, szb, szc)


OUT_TM = 512


def _out_kernel(next_norm, x_ref, ga_ref, gb_ref, gc_ref, w_ref, mod_ref, *rest):
    g = jnp.concatenate([ga_ref[0], gb_ref[0], gc_ref[0]], axis=1)
    x_new = x_ref[0] + mod_ref[0, :, 2 * D_MODEL:] * _dot(g, w_ref[0])
    if next_norm:
        nw_ref, modn_ref = rest[0], rest[1]
        o_ref, h_ref = rest[-2], rest[-1]
        o_ref[0] = x_new
        h_ref[0] = _modulated_norm(x_new, nw_ref[...], modn_ref[0]).astype(bf16)
    else:
        rest[-1][0] = x_new


def _out_proj(x, ga, gb, gc, w_out_bf, layer, mod3, is_ctx, next_norm=None):
    t = x.shape[1]
    tm = min(t, OUT_TM)
    mod_row = (lambda b: BATCH) if is_ctx else (lambda b: b)
    row_block0 = SEQ // tm if is_ctx else 0
    in_specs = [
        pl.BlockSpec((1, tm, D_MODEL), lambda b, i: (b, i, 0)),
        pl.BlockSpec((1, tm, D_A), lambda b, i: (b, i, 0)),
        pl.BlockSpec((1, tm, D_B), lambda b, i: (b, i, 0)),
        pl.BlockSpec((1, tm, D_C), lambda b, i: (b, i, 0)),
        pl.BlockSpec((1, D_MIX, D_MODEL), lambda b, i: (layer, 0, 0), pipeline_mode=pl.Buffered(1)),
        pl.BlockSpec((1, 1, 3 * D_MODEL), lambda b, i: (mod_row(b), 0, 0)),
    ]
    args = [x, ga, gb, gc, w_out_bf, mod3]
    out_specs = [pl.BlockSpec((1, tm, D_MODEL), lambda b, i: (b, i, 0))]
    out_shape = [jax.ShapeDtypeStruct(x.shape, f32)]
    aliases = {}
    if next_norm is not None:
        norm_w, mod3_next, h_buf = next_norm
        in_specs += [pl.BlockSpec((1, D_MODEL), lambda b, i: (0, 0)),
                     pl.BlockSpec((1, 1, 3 * D_MODEL), lambda b, i: (mod_row(b), 0, 0))]
        args += [norm_w.reshape(1, D_MODEL), mod3_next]
        if h_buf is not None:
            in_specs.append(pl.BlockSpec(memory_space=pl.ANY))
            args.append(h_buf)
            aliases = {len(args) - 1: 1}
        out_specs.append(pl.BlockSpec((1, tm, D_MODEL), lambda b, i: (b, row_block0 + i, 0)))
        out_shape.append(jax.ShapeDtypeStruct((BATCH, S_ALL, D_MODEL), bf16))
    return pl.pallas_call(
        functools.partial(_out_kernel, next_norm is not None),
        grid=(BATCH, t // tm),
        in_specs=in_specs,
        out_specs=out_specs,
        out_shape=out_shape,
        input_output_aliases=aliases,
        compiler_params=_cparams("arbitrary", "arbitrary"),
        name="out_proj_ctx" if is_ctx else "out_proj",
    )(*args)


def _rope_tables():
    t = jnp.arange(SEQ)
    row = (t // GRID_W).astype(f32)
    col = (t % GRID_W).astype(f32)
    n_freq = ROPE_DIM // 4
    inv = ROPE_BASE ** (-jnp.arange(n_freq, dtype=f32) / n_freq)
    ar = row[:, None] * inv
    ac = col[:, None] * inv
    ang = jnp.concatenate([ar, ar, ac, ac], axis=-1)
    cos = jnp.cos(ang).astype(f32)
    sin = jnp.sin(ang).astype(f32)
    sign = jnp.asarray(np.where((np.arange(ROPE_DIM) % 32) < 16, -1.0, 1.0), dtype=f32)
    sin = sin * sign
    cos = jnp.concatenate([cos, jnp.ones((CTX_LEN, ROPE_DIM), f32)], axis=0)
    sin = jnp.concatenate([sin, jnp.zeros((CTX_LEN, ROPE_DIM), f32)], axis=0)
    pair = (jnp.concatenate([cos, cos], axis=1), jnp.concatenate([sin, sin], axis=1))
    single = (jnp.concatenate([cos, jnp.ones_like(cos)], axis=1), jnp.concatenate([sin, jnp.zeros_like(sin)], axis=1))
    return pair, single


def _pad_heads(w, width, padded):
    lead = w.shape[:-1]
    w = w.reshape(lead + (H_C, width))
    w = jnp.pad(w, [(0, 0)] * len(lead) + [(0, 0), (0, padded - width)])
    return w.reshape(lead + (H_C * padded,))


def kernel(x, c, ctx, c_ctx, norm_w, w_ada, b_ada, w_in, qn_a, kn_a, rpb_a, qn_b, kn_b, sink_b,
           qa_norm, kva_norm, w_qb, w_kvb, qn_c, kn_c, w_out):
    (cos2, sin2), (cos1, sin1) = _rope_tables()
    cvec = jnp.concatenate([c, c_ctx[None, :], jnp.zeros((MOD_ROWS - BATCH - 1, D_MODEL), f32)], axis=0)
    mod = _modulation(cvec, w_ada, b_ada)
    q_scale_ab = HD ** -0.5 * LOG2E
    q_scale_c = MLA_SCALE * LOG2E
    w_out_bf = w_out.astype(bf16)
    na_bias = _natten_bias(rpb_a)

    o_qa, o_ka, o_va, o_qb, o_kb, o_vb, o_cq, o_ckv, o_kpe, o_z = np.cumsum(
        (0, D_A, D_A, D_A, D_B, KV_B, KV_B, Q_LORA, KV_LORA, ROPE_DIM)).tolist()

    xc = ctx
    mod3_all = [mod[l].reshape(MOD_ROWS, 1, 3 * D_MODEL) for l in range(DEPTH)]
    h3 = None
    for l in range(DEPTH):
        last = l == DEPTH - 1
        mod3 = mod3_all[l]
        wl = w_in[l]
        wz = wl[:, o_z:].astype(bf16)
        if l == 0:
            sza, szb, szc, h = _proj_gate_norm(x, xc, norm_w[0], mod3, wz)
        else:
            h = h3.reshape(R_ALL, D_MODEL)
            sza, szb, szc = _proj_gate(h, wz)
        h3 = h.reshape(BATCH, S_ALL, D_MODEL)

        gain_hn = jnp.concatenate([jnp.tile(qn_a[l] * q_scale_ab, H_A), jnp.tile(kn_a[l], H_A),
                                   jnp.tile(qn_b[l] * q_scale_ab, H_B), jnp.tile(kn_b[l], HKV_B)])
        w_hn = jnp.concatenate([wl[:, o_qa:o_va], wl[:, o_qb:o_vb]], axis=1).astype(bf16)
        w_lora = jnp.pad(wl[:, o_cq:o_z], ((0, 0), (0, LANES - ROPE_DIM))).astype(bf16)
        wt_v = jnp.concatenate([wl[:, o_va:o_qb], wl[:, o_vb:o_cq]], axis=1).T.astype(bf16)
        qka, qkb, cqn, ckvn, kpe, vt_ab = _proj_main(h, w_hn, w_lora, wt_v, gain_hn, qa_norm[l], kva_norm[l],
                                                      cos2, sin2)

        wq = _pad_heads(w_qb[l], QK_C, QC_PAD).astype(bf16)
        wkv = w_kvb[l].reshape(KV_LORA, H_C, NOPE_C + V_C)
        wkn = wkv[:, :, :NOPE_C].reshape(KV_LORA, H_C * NOPE_C).astype(bf16)
        wv = wkv[:, :, NOPE_C:].reshape(KV_LORA, H_C * V_C).astype(bf16)
        gq = _pad_heads(jnp.tile(qn_c[l] * q_scale_c, H_C), QK_C, QC_PAD).reshape(1, H_C * QC_PAD)
        gk = jnp.pad(kn_c[l], (0, QC_PAD - QK_C)).reshape(1, QC_PAD)
        qc, kc, vc = _mla_up(cqn, ckvn, kpe, wq, wkn, wv, gq, gk, cos1, sin1)

        def b3(a):
            return a.reshape(BATCH, S_ALL, a.shape[-1])

        qka, qkb, qc, kc, vc, sza, szb, szc = map(b3, (qka, qkb, qc, kc, vc, sza, szb, szc))

        ga = _natten(qka, vt_ab, na_bias, l, sza)
        gb = _swa(qkb, vt_ab, sink_b[l], szb)
        gc = _mla(qc, kc, vc, szc)
        if last:
            (x,) = _out_proj(x, ga, gb, gc, w_out_bf, l, mod3, False)
        else:
            ga_c, gb_c, gc_c = _ctx_attention(qka, qkb, vt_ab, qc, kc, vc, sink_b[l], sza, szb, szc)
            x, h3 = _out_proj(x, ga, gb, gc, w_out_bf, l, mod3, False,
                              next_norm=(norm_w[l + 1], mod3_all[l + 1], h3))
            xc, h3 = _out_proj(xc, ga_c, gb_c, gc_c, w_out_bf, l, mod3, True,
                               next_norm=(norm_w[l + 1], mod3_all[l + 1], h3))
    return x
```

```python
import functools
import math

import numpy as np
import jax
import jax.numpy as jnp
from jax import lax
from jax.experimental import pallas as pl
from jax.experimental.pallas import tpu as pltpu

D_MODEL = 2048
BATCH = 8
SEQ = 2048
DEPTH = 2
GRID_W = 64
CTX_LEN = 256
HD = 64
H_A = 8
D_A = H_A * HD
H_B = 12
HKV_B = 4
D_B = H_B * HD
KV_B = HKV_B * HD
H_C = 6
NOPE_C = 128
ROPE_DIM = 64
QK_C = NOPE_C + ROPE_DIM
V_C = 128
D_C = H_C * V_C
Q_LORA = 768
KV_LORA = 512
D_MIX = D_A + D_B + D_C
NA_ROWS = 8
NA_COLS = 16
SW_WINDOW = 128
ROPE_BASE = 10000.0
EPS = 1e-6
NEG_INF = -1e30

S_ALL = SEQ + CTX_LEN
R_ALL = BATCH * S_ALL
GRID_ROWS = SEQ // GRID_W
LANES = 128
MXU_DIM = 256
BF16_ROWS = 16
QC_PAD = 256
VT_ROWS = LANES + BF16_ROWS
MOD_ROWS = 16
VMEM_LIMIT = 48 * 1024 * 1024
LOG2E = math.log2(math.e)

bf16 = jnp.bfloat16
f32 = jnp.float32


def _cparams(*sem):
    return pltpu.CompilerParams(dimension_semantics=sem, vmem_limit_bytes=VMEM_LIMIT)


def _silu(v):
    return v / (1.0 + jnp.exp(-v))


def _lane_iota(shape):
    return lax.broadcasted_iota(jnp.int32, shape, len(shape) - 1)


def _rope_rotate(xn, cos, sin_signed):
    lane = _lane_iota(xn.shape)
    take_next = (lane % 32) < 16
    rot = jnp.where(take_next, pltpu.roll(xn, LANES - 16, 1), pltpu.roll(xn, 16, 1))
    return xn * cos + rot * sin_signed


def _dot_nt(a, b):
    return lax.dot_general(a, b, (((1,), (1,)), ((), ())), preferred_element_type=f32)


def _dot(a, b):
    return jnp.dot(a, b, preferred_element_type=f32)


def _mod_kernel(c_ref, w_ref, b_ref, o_ref):
    sc = _silu(c_ref[...]).astype(bf16)
    o_ref[0] = _dot(sc, w_ref[0].astype(bf16)) + b_ref[0]


def _modulation(cvec, w_ada, b_ada):
    tn = 1024
    n = 3 * D_MODEL
    return pl.pallas_call(
        _mod_kernel,
        grid=(DEPTH, n // tn),
        in_specs=[
            pl.BlockSpec((MOD_ROWS, D_MODEL), lambda l, j: (0, 0)),
            pl.BlockSpec((1, D_MODEL, tn), lambda l, j: (l, 0, j)),
            pl.BlockSpec((1, 1, tn), lambda l, j: (l, 0, j)),
        ],
        out_specs=pl.BlockSpec((1, MOD_ROWS, tn), lambda l, j: (l, 0, j)),
        out_shape=jax.ShapeDtypeStruct((DEPTH, MOD_ROWS, n), f32),
        compiler_params=_cparams("arbitrary", "arbitrary"),
        name="modulation",
    )(cvec, w_ada, b_ada.reshape(DEPTH, 1, n))


H_ROWS = CTX_LEN
N_LAT_BLOCKS = SEQ // H_ROWS


def _modulated_norm(x, nw, mod):
    ms = jnp.mean(x * x, axis=-1, keepdims=True)
    y = x * lax.rsqrt(ms + EPS) * nw
    return y * (1.0 + mod[:, D_MODEL:2 * D_MODEL]) + mod[:, 0:D_MODEL]


PROJ_TM = 768


def _group_mean_sq(u):
    sq = u * u
    low = _lane_iota((u.shape[0], LANES)) < HD
    parts = []
    for c in range(0, u.shape[1], LANES):
        t = sq[:, c:c + LANES]
        s_low = jnp.sum(jnp.where(low, t, 0.0), axis=-1, keepdims=True)
        s_high = jnp.sum(jnp.where(low, 0.0, t), axis=-1, keepdims=True)
        parts.append(jnp.where(low, s_low, s_high) * (1.0 / HD))
    return jnp.concatenate(parts, axis=1)


def _run_items(items):
    u = items[0][0]()
    for i, (_, epilogue) in enumerate(items):
        u_next = items[i + 1][0]() if i + 1 < len(items) else None
        epilogue(u)
        u = u_next


def _store_vt_tile(vt_ref, p, ut):
    vt_ref[p, 0:LANES, :] = ut.astype(bf16)
    vt_ref[p, LANES:VT_ROWS, :] = jnp.ones((VT_ROWS - LANES, ut.shape[1]), bf16)


PROJ_TN = 512
N_HN = 2 * D_A + D_B + KV_B
N_LORA = Q_LORA + KV_LORA + LANES
N_VT = D_A + KV_B


def _proj_main_kernel(h_ref, whn_ref, wl_ref, wv_ref, ghn_ref, cos_ref, sin_ref, gcq_ref, gckv_ref,
                      qka_ref, qkb_ref, cqn_ref, ckvn_ref, kpe_ref, vt_ref):
    def hn_item(k):
        cols = slice(k * PROJ_TN, (k + 1) * PROJ_TN)
        rope = k * PROJ_TN >= 2 * D_A
        o_ref = qkb_ref if rope else qka_ref
        o0 = k * PROJ_TN - (2 * D_A if rope else 0)

        def epilogue(u):
            for c0 in range(0, PROJ_TN, MXU_DIM):
                uc = u[:, c0:c0 + MXU_DIM]
                ms = _group_mean_sq(uc)
                xn = uc * lax.rsqrt(ms + EPS) * ghn_ref[:, k * PROJ_TN + c0:k * PROJ_TN + c0 + MXU_DIM]
                if rope:
                    for c1 in range(0, MXU_DIM, LANES):
                        o_ref[:, o0 + c0 + c1:o0 + c0 + c1 + LANES] = _rope_rotate(
                            xn[:, c1:c1 + LANES], cos_ref[...], sin_ref[...]).astype(bf16)
                else:
                    o_ref[:, o0 + c0:o0 + c0 + MXU_DIM] = xn.astype(bf16)

        return lambda: _dot(h_ref[...], whn_ref[:, cols]), epilogue

    def rowrms_item(c0, width, gain_ref, o_ref):
        def epilogue(u):
            ms = jnp.mean(u * u, axis=-1, keepdims=True)
            o_ref[...] = (u * lax.rsqrt(ms + EPS) * gain_ref[...]).astype(bf16)

        return lambda: _dot(h_ref[...], wl_ref[:, c0:c0 + width]), epilogue

    def kpe_item():
        def epilogue(u):
            kpe_ref[...] = u

        return lambda: _dot(h_ref[...], wl_ref[:, Q_LORA + KV_LORA:N_LORA]), epilogue

    def vt_item(j):
        def epilogue(u):
            for i in range(2):
                _store_vt_tile(vt_ref, 2 * j + i, u[:, i * LANES:(i + 1) * LANES].T)

        return lambda: _dot(h_ref[...], wv_ref[:, j * MXU_DIM:(j + 1) * MXU_DIM]), epilogue

    _run_items([hn_item(k) for k in range(N_HN // PROJ_TN)]
               + [rowrms_item(0, Q_LORA, gcq_ref, cqn_ref), rowrms_item(Q_LORA, KV_LORA, gckv_ref, ckvn_ref),
                  kpe_item()]
               + [vt_item(j) for j in range(N_VT // MXU_DIM)])


def _resident(shape):
    return pl.BlockSpec(shape, lambda i: (0,) * len(shape), pipeline_mode=pl.Buffered(1))


def _proj_main(h2d, whn, wl, wv, ghn, gcq, gckv, cos, sin):
    tm = PROJ_TM
    nt = S_ALL // tm
    row = lambda i: (i, 0)
    tiles = N_VT // LANES
    return pl.pallas_call(
        _proj_main_kernel,
        grid=(R_ALL // tm,),
        in_specs=[
            pl.BlockSpec((tm, D_MODEL), row),
            _resident((D_MODEL, N_HN)), _resident((D_MODEL, N_LORA)), _resident((D_MODEL, N_VT)),
            _resident((1, N_HN)),
            pl.BlockSpec((tm, LANES), lambda i: (i % nt, 0)),
            pl.BlockSpec((tm, LANES), lambda i: (i % nt, 0)),
            _resident((1, Q_LORA)), _resident((1, KV_LORA)),
        ],
        out_specs=[
            pl.BlockSpec((tm, 2 * D_A), row),
            pl.BlockSpec((tm, D_B + KV_B), row),
            pl.BlockSpec((tm, Q_LORA), row),
            pl.BlockSpec((tm, KV_LORA), row),
            pl.BlockSpec((tm, LANES), row),
            pl.BlockSpec((tiles, VT_ROWS, tm), lambda i: (0, 0, i)),
        ],
        out_shape=[
            jax.ShapeDtypeStruct((R_ALL, 2 * D_A), bf16),
            jax.ShapeDtypeStruct((R_ALL, D_B + KV_B), bf16),
            jax.ShapeDtypeStruct((R_ALL, Q_LORA), bf16),
            jax.ShapeDtypeStruct((R_ALL, KV_LORA), bf16),
            jax.ShapeDtypeStruct((R_ALL, LANES), f32),
            jax.ShapeDtypeStruct((tiles, VT_ROWS, R_ALL), bf16),
        ],
        compiler_params=_cparams("arbitrary"),
        name="proj_main",
    )(h2d, whn, wl, wv, ghn.reshape(1, N_HN), cos, sin,
      gcq.reshape(1, Q_LORA), gckv.reshape(1, KV_LORA))


N_CHUNK_TILE = PROJ_TM // H_ROWS
TILES_PER_BATCH = S_ALL // PROJ_TM


def _proj_gate_norm_kernel(*refs):
    x_refs = refs[:N_CHUNK_TILE]
    ctx_ref, nw_ref, modl_ref, modc_ref, wz_ref, sza_ref, szb_ref, szc_ref, h_ref = refs[N_CHUNK_TILE:]
    last_tile = pl.program_id(0) % TILES_PER_BATCH == TILES_PER_BATCH - 1
    for j, x_ref in enumerate(x_refs):
        x, mod = x_ref[0], modl_ref[0]
        if j == N_CHUNK_TILE - 1:
            x = jnp.where(last_tile, ctx_ref[0], x)
            mod = jnp.where(last_tile, modc_ref[0], mod)
        h_ref[j * H_ROWS:(j + 1) * H_ROWS, :] = _modulated_norm(x, nw_ref[...], mod).astype(bf16)
    _proj_gate_kernel(h_ref, wz_ref, sza_ref, szb_ref, szc_ref)


def _proj_gate_kernel(h_ref, wz_ref, sza_ref, szb_ref, szc_ref):
    outs = ((sza_ref, 0, D_A), (szb_ref, D_A, D_B), (szc_ref, D_A + D_B, D_C))

    def item(k):
        c0 = k * PROJ_TN

        def epilogue(u):
            sz = _silu(u)
            for o_ref, start, width in outs:
                lo, hi = max(c0, start), min(c0 + PROJ_TN, start + width)
                if lo < hi:
                    o_ref[:, lo - start:hi - start] = sz[:, lo - c0:hi - c0]

        return lambda: _dot(h_ref[...], wz_ref[:, c0:c0 + PROJ_TN]), epilogue

    _run_items([item(k) for k in range(D_MIX // PROJ_TN)])


def _proj_gate_norm(x, xc, norm_w, mod3, wz):
    tm = PROJ_TM
    row = lambda i: (i, 0)

    def chunk(j):
        return pl.BlockSpec((1, H_ROWS, D_MODEL), lambda i: (
            i // TILES_PER_BATCH, jnp.minimum((i % TILES_PER_BATCH) * N_CHUNK_TILE + j, N_LAT_BLOCKS - 1), 0))

    return pl.pallas_call(
        _proj_gate_norm_kernel,
        grid=(R_ALL // tm,),
        in_specs=[
            *[chunk(j) for j in range(N_CHUNK_TILE)],
            pl.BlockSpec((1, H_ROWS, D_MODEL), lambda i: (i // TILES_PER_BATCH, 0, 0)),
            pl.BlockSpec((1, D_MODEL), lambda i: (0, 0)),
            pl.BlockSpec((1, 1, 3 * D_MODEL), lambda i: (i // TILES_PER_BATCH, 0, 0)),
            pl.BlockSpec((1, 1, 3 * D_MODEL), lambda i: (BATCH, 0, 0)),
            _resident((D_MODEL, D_MIX)),
        ],
        out_specs=[pl.BlockSpec((tm, D_A), row), pl.BlockSpec((tm, D_B), row), pl.BlockSpec((tm, D_C), row),
                   pl.BlockSpec((tm, D_MODEL), row)],
        out_shape=[jax.ShapeDtypeStruct((R_ALL, D_A), f32), jax.ShapeDtypeStruct((R_ALL, D_B), f32),
                   jax.ShapeDtypeStruct((R_ALL, D_C), f32), jax.ShapeDtypeStruct((R_ALL, D_MODEL), bf16)],
        compiler_params=_cparams("arbitrary"),
        name="proj_gate_norm",
    )(*([x] * N_CHUNK_TILE), xc, norm_w.reshape(1, D_MODEL), mod3, mod3, wz)


def _proj_gate(h2d, wz):
    tm = PROJ_TM
    row = lambda i: (i, 0)
    return pl.pallas_call(
        _proj_gate_kernel,
        grid=(R_ALL // tm,),
        in_specs=[pl.BlockSpec((tm, D_MODEL), row), _resident((D_MODEL, D_MIX))],
        out_specs=[pl.BlockSpec((tm, D_A), row), pl.BlockSpec((tm, D_B), row), pl.BlockSpec((tm, D_C), row)],
        out_shape=[jax.ShapeDtypeStruct((R_ALL, D_A), f32), jax.ShapeDtypeStruct((R_ALL, D_B), f32),
                   jax.ShapeDtypeStruct((R_ALL, D_C), f32)],
        compiler_params=_cparams("arbitrary"),
        name="proj_gate",
    )(h2d, wz)


def _mla_up_kernel(cq_ref, ckv_ref, kpe_ref, wq_ref, wkn_ref, wv_ref, gq_ref, gk_ref,
                   cos_ref, sin_ref, qc_ref, kc_ref, vc_ref):
    cos = cos_ref[...]
    sin = sin_ref[...]
    kpe = kpe_ref[...]
    ss_pe = jnp.sum(kpe * kpe, axis=-1, keepdims=True)
    k_rot = _rope_rotate(kpe * gk_ref[:, LANES:], cos, sin)

    def q_head(h):
        cols = slice(h * QC_PAD, (h + 1) * QC_PAD)
        return lambda: _dot(cq_ref[...], wq_ref[:, cols]), functools.partial(q_epilogue, h)

    def q_epilogue(h, u):
        ms = jnp.sum(u * u, axis=-1, keepdims=True) * (1.0 / QK_C)
        xn = u * lax.rsqrt(ms + EPS) * gq_ref[:, h * QC_PAD:(h + 1) * QC_PAD]
        qc_ref[:, h * QC_PAD:h * QC_PAD + LANES] = xn[:, :LANES].astype(bf16)
        qc_ref[:, h * QC_PAD + LANES:(h + 1) * QC_PAD] = _rope_rotate(xn[:, LANES:], cos, sin).astype(bf16)

    def k_head(h):
        cols = slice(h * NOPE_C, (h + 1) * NOPE_C)
        return lambda: _dot(ckv_ref[...], wkn_ref[:, cols]), functools.partial(k_epilogue, h)

    def k_epilogue(h, u):
        ms = (jnp.sum(u * u, axis=-1, keepdims=True) + ss_pe) * (1.0 / QK_C)
        r = lax.rsqrt(ms + EPS)
        kc_ref[:, h * QC_PAD:h * QC_PAD + LANES] = (u * r * gk_ref[:, 0:LANES]).astype(bf16)
        kc_ref[:, h * QC_PAD + LANES:(h + 1) * QC_PAD] = (k_rot * r).astype(bf16)

    def v_head(h):
        def epilogue(u):
            vc_ref[:, h * VC_PAD:h * VC_PAD + V_C] = u.astype(bf16)
            vc_ref[:, h * VC_PAD + V_C:(h + 1) * VC_PAD] = jnp.ones((u.shape[0], VC_PAD - V_C), bf16)

        return lambda: _dot(ckv_ref[...], wv_ref[:, h * V_C:(h + 1) * V_C]), epilogue

    _run_items([q_head(h) for h in range(H_C)] + [k_head(h) for h in range(H_C)] + [v_head(h) for h in range(H_C)])


def _mla_up(cqn, ckvn, kpe, wq, wkn, wv, gq, gk, cos, sin):
    tm = PROJ_TM
    nt = S_ALL // tm
    row = lambda i: (i, 0)
    fixed = lambda i: (0, 0)
    return pl.pallas_call(
        _mla_up_kernel,
        grid=(R_ALL // tm,),
        in_specs=[
            pl.BlockSpec((tm, Q_LORA), row),
            pl.BlockSpec((tm, KV_LORA), row),
            pl.BlockSpec((tm, LANES), row),
            pl.BlockSpec((Q_LORA, H_C * QC_PAD), fixed),
            pl.BlockSpec((KV_LORA, H_C * NOPE_C), fixed),
            pl.BlockSpec((KV_LORA, H_C * V_C), fixed),
            pl.BlockSpec((1, H_C * QC_PAD), fixed),
            pl.BlockSpec((1, QC_PAD), fixed),
            pl.BlockSpec((tm, LANES), lambda i: (i % nt, 0)),
            pl.BlockSpec((tm, LANES), lambda i: (i % nt, 0)),
        ],
        out_specs=[
            pl.BlockSpec((tm, H_C * QC_PAD), row),
            pl.BlockSpec((tm, H_C * QC_PAD), row),
            pl.BlockSpec((tm, H_C * VC_PAD), row),
        ],
        out_shape=[
            jax.ShapeDtypeStruct((R_ALL, H_C * QC_PAD), bf16),
            jax.ShapeDtypeStruct((R_ALL, H_C * QC_PAD), bf16),
            jax.ShapeDtypeStruct((R_ALL, H_C * VC_PAD), bf16),
        ],
        compiler_params=_cparams("arbitrary"),
        name="mla_up",
    )(cqn, ckvn, kpe, wq, wkn, wv, gq, gk, cos, sin)


def _pipelined(n, score_fn, value_fn):
    _run_items([(functools.partial(score_fn, j), functools.partial(value_fn, j)) for j in range(n)])


def _col_max(s):
    return jnp.max(s, axis=0, keepdims=True)


def _prob(s, m):
    return jnp.exp2((s - m).astype(bf16))


def _pair_q(qv, half_even, half_odd):
    low = _lane_iota(qv.shape) < HD
    zero = jnp.zeros_like(qv)
    qa = jnp.where(low, qv, zero) if half_even == 0 else jnp.where(low, zero, pltpu.roll(qv, HD, 1))
    qb = jnp.where(low, zero, qv) if half_odd == 1 else jnp.where(low, pltpu.roll(qv, HD, 1), zero)
    return jnp.concatenate([qa, qb], axis=0).astype(bf16)


def _pair_out(acc, l, n, half_even, half_odd):
    o = acc[:LANES] / l
    t = jnp.concatenate([o[half_even * HD:(half_even + 1) * HD, 0:n],
                         o[half_odd * HD:(half_odd + 1) * HD, n:2 * n]], axis=0)
    return t.T


NA_QROWS = 2
NA_Q = NA_QROWS * GRID_W
NA_KROWS = NA_ROWS + NA_QROWS
NA_KEYS = NA_KROWS * GRID_W
NA_VARIANTS = ((0, 0, 0, -1), (0, -2, 0, -3), (0, -4, 1, -5), (0, -4, 0, -5), (0, -6, 0, -7))


def _na_window_row(j):
    return jnp.clip(NA_QROWS * j - NA_ROWS // 2, 0, GRID_ROWS - NA_ROWS)


NA_SUB = 4
NA_BLOCKS = GRID_ROWS // NA_QROWS


def _natten_kernel(q_ref, k_ref, vt_ref, *rest):
    bias_refs = rest[:NA_SUB]
    sz_ref, o_ref, s_scr = rest[NA_SUB:]
    step = pl.program_id(1)
    k0 = [pl.multiple_of(_na_window_row(NA_SUB * step + s) * GRID_W, LANES) for s in range(NA_SUB)]
    n_pair = H_A // 2

    def scores(i):
        s, p = divmod(i, n_pair)
        cols = slice(p * LANES, (p + 1) * LANES)
        q2 = _pair_q(q_ref[0, s * NA_Q:(s + 1) * NA_Q, cols].astype(f32), 0, 1)
        s_loc = _dot_nt(k_ref[0, pl.ds(k0[s], NA_KEYS), cols], q2) + bias_refs[s][0, 0, p]
        s_ctx = _dot_nt(k_ref[0, SEQ:S_ALL, cols], q2)
        s_scr[i % 2, 0:NA_KEYS, :] = s_loc
        s_scr[i % 2, NA_KEYS:, :] = s_ctx
        return jnp.maximum(_col_max(s_loc), _col_max(s_ctx))

    def values(i, m):
        s, p = divmod(i, n_pair)
        cols = slice(p * LANES, (p + 1) * LANES)
        rows = slice(s * NA_Q, (s + 1) * NA_Q)
        acc = (_dot(vt_ref[p, :, pl.ds(k0[s], NA_KEYS)], _prob(s_scr[i % 2, 0:NA_KEYS, :], m))
               + _dot(vt_ref[p, :, SEQ:S_ALL], _prob(s_scr[i % 2, NA_KEYS:, :], m)))
        o = _pair_out(acc, acc[LANES:LANES + 1], NA_Q, 0, 1)
        o_ref[0, rows, cols] = (o * sz_ref[0, rows, cols]).astype(bf16)

    _pipelined(NA_SUB * n_pair, scores, values)


def _natten(qk, vt, bias, layer, sz):
    def variant(s):
        def index(b, step):
            j = NA_SUB * step + s
            v = jnp.where(j <= 1, j, jnp.where(j >= NA_BLOCKS - 2, j - (NA_BLOCKS - 5), 2))
            return (layer, v, 0, 0, 0)
        return index

    tq = NA_SUB * NA_Q
    bias_block = (1, 1, H_A // 2, NA_KEYS, 2 * NA_Q)
    return pl.pallas_call(
        _natten_kernel,
        grid=(BATCH, NA_BLOCKS // NA_SUB),
        in_specs=[
            pl.BlockSpec((1, tq, D_A), lambda b, j: (b, j, 0)),
            pl.BlockSpec((1, S_ALL, D_A), lambda b, j: (b, 0, 1)),
            pl.BlockSpec((H_A // 2, VT_ROWS, S_ALL), lambda b, j: (0, 0, b)),
            *[pl.BlockSpec(bias_block, variant(s)) for s in range(NA_SUB)],
            pl.BlockSpec((1, tq, D_A), lambda b, j: (b, j, 0)),
        ],
        out_specs=pl.BlockSpec((1, tq, D_A), lambda b, j: (b, j, 0)),
        out_shape=jax.ShapeDtypeStruct((BATCH, SEQ, D_A), bf16),
        scratch_shapes=[pltpu.VMEM((2, NA_KEYS + CTX_LEN, 2 * NA_Q), f32)],
        compiler_params=_cparams("arbitrary", "arbitrary"),
        name="natten",
    )(qk, qk, vt, *([bias] * NA_SUB), sz)


def _natten_bias(rpb):
    n_dc = 2 * NA_COLS - 1
    n_dr = NA_KROWS + NA_ROWS - 1
    c = np.arange(LANES) % GRID_W
    kc = np.arange(GRID_W)
    qstart = np.clip(c - NA_COLS // 2, 0, GRID_W - NA_COLS)
    col_ok = (kc[:, None] >= qstart[None, :]) & (kc[:, None] < qstart[None, :] + NA_COLS)
    dc = np.clip(kc[:, None] - c[None, :], -(NA_COLS - 1), NA_COLS - 1) + NA_COLS - 1
    onehot = jnp.asarray((dc[None] == np.arange(n_dc)[:, None, None]).astype(np.float32))
    exp = jnp.einsum("lhrd,dkc->lhrkc", rpb.astype(f32) * LOG2E, onehot, precision=lax.Precision.HIGHEST)
    exp = jnp.pad(exp, ((0, 0), (0, 0), (0, n_dr - exp.shape[2]), (0, 0), (0, 0)))
    col_mask = jnp.asarray(np.where(col_ok, 0.0, NEG_INF).astype(np.float32))
    n_var = len(NA_VARIANTS)
    return pl.pallas_call(
        _natten_bias_kernel,
        grid=(DEPTH, n_var),
        in_specs=[
            pl.BlockSpec((1, H_A, n_dr, GRID_W, LANES), lambda l, v: (l, 0, 0, 0, 0)),
            pl.BlockSpec((GRID_W, LANES), lambda l, v: (0, 0)),
        ],
        out_specs=pl.BlockSpec((1, 1, H_A // 2, NA_KEYS, 2 * NA_Q), lambda l, v: (l, v, 0, 0, 0)),
        out_shape=jax.ShapeDtypeStruct((DEPTH, n_var, H_A // 2, NA_KEYS, 2 * NA_Q), f32),
        compiler_params=_cparams("arbitrary", "arbitrary"),
        name="natten_bias",
    )(exp, col_mask)


def _natten_bias_kernel(exp_ref, mask_ref, o_ref):
    v = pl.program_id(1)

    def pick(column):
        out = jnp.int32(0)
        for i, var in enumerate(NA_VARIANTS):
            out = jnp.where(v == i, var[column], out)
        return out

    lo = (pick(0), pick(2))
    off = (pick(1), pick(3))
    low = _lane_iota((GRID_W, LANES)) < GRID_W
    mask = mask_ref[...]
    for kr in range(NA_KROWS):
        valid = [(kr >= lo[qr]) & (kr < lo[qr] + NA_ROWS) for qr in range(NA_QROWS)]
        for h in range(H_A):
            halves = []
            for qr in range(NA_QROWS):
                blk = exp_ref[0, h, pl.ds(kr + off[qr] + NA_ROWS - 1, 1)][0] + mask
                halves.append(jnp.where(valid[qr], blk, NEG_INF))
            e = h % 2
            o_ref[0, 0, h // 2, kr * GRID_W:(kr + 1) * GRID_W, e * NA_Q:(e + 1) * NA_Q] = jnp.where(
                low, halves[0], halves[1])


SW_Q = 128
SW_SPAN = SW_Q + 2 * SW_WINDOW
G_B = H_B // HKV_B
N_CHUNK_B = H_B // 2


def _gqa_halves(c):
    return ((2 * c) // G_B) % 2, ((2 * c + 1) // G_B) % 2


SW_SUB = 8
SW_BLOCKS = SEQ // SW_Q


def _swa_kernel(q_ref, k_ref, vt_ref, mask_ref, sink_ref, sz_ref, o_ref, s_scr):
    step = pl.program_id(1)
    blocks = [SW_SUB * step + s for s in range(SW_SUB)]
    start = [pl.multiple_of(jnp.clip((n - 1) * SW_Q, 0, SEQ - SW_SPAN), SW_Q) for n in blocks]
    variant = [jnp.where(n == 0, 0, jnp.where(n == SW_BLOCKS - 1, 2, 1)) for n in blocks]

    def scores(i):
        s, c = divmod(i, N_CHUNK_B)
        kp = c // G_B
        cols = slice(c * LANES, (c + 1) * LANES)
        kcols = slice(kp * LANES, (kp + 1) * LANES)
        q2 = _pair_q(q_ref[0, s * SW_Q:(s + 1) * SW_Q, cols].astype(f32), *_gqa_halves(c))
        s_loc = _dot_nt(k_ref[0, pl.ds(start[s], SW_SPAN), kcols], q2) + mask_ref[variant[s]]
        s_ctx = _dot_nt(k_ref[0, SEQ:S_ALL, kcols], q2)
        s_scr[i % 2, 0:SW_SPAN, :] = s_loc
        s_scr[i % 2, SW_SPAN:, :] = s_ctx
        return jnp.maximum(jnp.maximum(_col_max(s_loc), _col_max(s_ctx)), sink_ref[c])

    def values(i, m):
        s, c = divmod(i, N_CHUNK_B)
        kp = c // G_B
        cols = slice(c * LANES, (c + 1) * LANES)
        rows = slice(s * SW_Q, (s + 1) * SW_Q)
        acc = (_dot(vt_ref[kp, :, pl.ds(start[s], SW_SPAN)], _prob(s_scr[i % 2, 0:SW_SPAN, :], m))
               + _dot(vt_ref[kp, :, SEQ:S_ALL], _prob(s_scr[i % 2, SW_SPAN:, :], m)))
        l = acc[LANES:LANES + 1] + jnp.exp2(sink_ref[c] - m)
        o = _pair_out(acc, l, SW_Q, *_gqa_halves(c))
        o_ref[0, rows, cols] = (o * sz_ref[0, rows, cols]).astype(bf16)

    _pipelined(SW_SUB * N_CHUNK_B, scores, values)


def _swa_mask():
    i = np.arange(SW_Q)
    j = np.arange(SW_SPAN)
    out = []
    for d0 in (0, SW_Q, 2 * SW_Q):
        ok = np.abs(d0 + i[None, :] - j[:, None]) <= SW_WINDOW
        m = np.where(ok, 0.0, NEG_INF).astype(np.float32)
        out.append(np.concatenate([m, m], axis=1))
    return jnp.asarray(np.stack(out))


def _sink_cols(sink, n):
    return jnp.repeat(sink.astype(f32).reshape(N_CHUNK_B, 2) * LOG2E, n, axis=1)[:, None, :]


def _swa(qk, vt, sink, sz):
    tq = SW_SUB * SW_Q
    return pl.pallas_call(
        _swa_kernel,
        grid=(BATCH, SW_BLOCKS // SW_SUB),
        in_specs=[
            pl.BlockSpec((1, tq, D_B), lambda b, n: (b, n, 0)),
            pl.BlockSpec((1, S_ALL, KV_B), lambda b, n: (b, 0, D_B // KV_B)),
            pl.BlockSpec((HKV_B // 2, VT_ROWS, S_ALL), lambda b, n: (H_A // HKV_B, 0, b)),
            pl.BlockSpec((3, SW_SPAN, 2 * SW_Q), lambda b, n: (0, 0, 0)),
            pl.BlockSpec((N_CHUNK_B, 1, 2 * SW_Q), lambda b, n: (0, 0, 0)),
            pl.BlockSpec((1, tq, D_B), lambda b, n: (b, n, 0)),
        ],
        out_specs=pl.BlockSpec((1, tq, D_B), lambda b, n: (b, n, 0)),
        out_shape=jax.ShapeDtypeStruct((BATCH, SEQ, D_B), bf16),
        scratch_shapes=[pltpu.VMEM((2, SW_SPAN + CTX_LEN, 2 * SW_Q), f32)],
        compiler_params=_cparams("arbitrary", "arbitrary"),
        name="swa",
    )(qk, qk, vt, _swa_mask(), _sink_cols(sink, SW_Q), sz)


MLA_TQ = SEQ
MLA_SUB = 512
MLA_KTILE = MXU_DIM
MLA_SCALE = float(QK_C) ** -0.5
VC_PAD = 2 * V_C


def _mla_kernel(q_ref, k_ref, v_ref, sz_ref, o_ref, s_scr, p_scr):
    tiles = [slice(c * MLA_KTILE, (c + 1) * MLA_KTILE) for c in range(S_ALL // MLA_KTILE)]

    def scores(j):
        q = q_ref[0, j * MLA_SUB:(j + 1) * MLA_SUB, :]
        m = None
        for kt in tiles:
            s = _dot_nt(q, k_ref[0, kt, :])
            s_scr[j % 2, :, kt] = s
            mc = jnp.maximum(s[:, :LANES], s[:, LANES:])
            m = mc if m is None else jnp.maximum(m, mc)
        return jnp.max(m, axis=-1, keepdims=True)

    def values(j, m):
        for kt in tiles:
            p_scr[:, kt] = _prob(s_scr[j % 2, :, kt], m)
        acc = _dot(p_scr[...], v_ref[0])
        rows = slice(j * MLA_SUB, (j + 1) * MLA_SUB)
        o_ref[0, rows, :] = (acc[:, :V_C] / acc[:, V_C:] * sz_ref[0, rows, :]).astype(bf16)

    _pipelined(MLA_TQ // MLA_SUB, scores, values)


def _mla(qc, kc, vc, sz):
    return pl.pallas_call(
        _mla_kernel,
        grid=(BATCH, H_C),
        in_specs=[
            pl.BlockSpec((1, MLA_TQ, QC_PAD), lambda b, h: (b, 0, h)),
            pl.BlockSpec((1, S_ALL, QC_PAD), lambda b, h: (b, 0, h)),
            pl.BlockSpec((1, S_ALL, VC_PAD), lambda b, h: (b, 0, h)),
            pl.BlockSpec((1, MLA_TQ, V_C), lambda b, h: (b, 0, h)),
        ],
        out_specs=pl.BlockSpec((1, MLA_TQ, V_C), lambda b, h: (b, 0, h)),
        out_shape=jax.ShapeDtypeStruct((BATCH, SEQ, D_C), bf16),
        scratch_shapes=[pltpu.VMEM((2, MLA_SUB, S_ALL), f32), pltpu.VMEM((MLA_SUB, S_ALL), bf16)],
        compiler_params=_cparams("arbitrary", "arbitrary"),
        name="mla",
    )(qc, kc, vc, sz)


def _ctx_kernel(qa_ref, ka_ref, qb_ref, kb_ref, vtab_ref, qc_ref, kc_ref, vc_ref,
                sink_ref, sza_ref, szb_ref, szc_ref, oa_ref, ob_ref, oc_ref):
    n = CTX_LEN

    def item_a(p):
        cols = slice(p * LANES, (p + 1) * LANES)

        def values(s):
            acc = _dot(vtab_ref[p], _prob(s, _col_max(s)))
            o = _pair_out(acc, acc[LANES:LANES + 1], n, 0, 1)
            oa_ref[0, :, cols] = (o * sza_ref[0, :, cols]).astype(bf16)

        return lambda: _dot_nt(ka_ref[0, :, cols], _pair_q(qa_ref[0, :, cols].astype(f32), 0, 1)), values

    def item_b(c):
        kp = c // G_B
        cols = slice(c * LANES, (c + 1) * LANES)

        def values(s):
            m = jnp.maximum(_col_max(s), sink_ref[c])
            acc = _dot(vtab_ref[H_A // 2 + kp], _prob(s, m))
            l = acc[LANES:LANES + 1] + jnp.exp2(sink_ref[c] - m)
            o = _pair_out(acc, l, n, *_gqa_halves(c))
            ob_ref[0, :, cols] = (o * szb_ref[0, :, cols]).astype(bf16)

        return lambda: _dot_nt(kb_ref[0, :, kp * LANES:(kp + 1) * LANES],
                               _pair_q(qb_ref[0, :, cols].astype(f32), *_gqa_halves(c))), values

    def item_c(h):
        qcols = slice(h * QC_PAD, (h + 1) * QC_PAD)
        vcols = slice(h * V_C, (h + 1) * V_C)

        def values(s):
            p = _prob(s, jnp.max(s, axis=-1, keepdims=True))
            acc = _dot(p, vc_ref[0, :, h * VC_PAD:(h + 1) * VC_PAD])
            oc_ref[0, :, vcols] = (acc[:, :V_C] / acc[:, V_C:] * szc_ref[0, :, vcols]).astype(bf16)

        return lambda: _dot_nt(qc_ref[0, :, qcols], kc_ref[0, :, qcols]), values

    _run_items([item_a(p) for p in range(H_A // 2)] + [item_b(c) for c in range(N_CHUNK_B)]
               + [item_c(h) for h in range(H_C)])


def _ctx_attention(qka, qkb, vt_ab, qc, kc, vc, sink, sza, szb, szc):
    cb = SEQ // CTX_LEN

    def spec(width, col):
        return pl.BlockSpec((1, CTX_LEN, width), lambda b: (b, cb, col))

    def vt_spec(tiles):
        return pl.BlockSpec((tiles, VT_ROWS, CTX_LEN), lambda b: (0, 0, b * (S_ALL // CTX_LEN) + cb))

    return pl.pallas_call(
        _ctx_kernel,
        grid=(BATCH,),
        in_specs=[
            spec(D_A, 0), spec(D_A, 1),
            spec(D_B, 0), spec(KV_B, D_B // KV_B),
            vt_spec((H_A + HKV_B) // 2),
            spec(H_C * QC_PAD, 0), spec(H_C * QC_PAD, 0), spec(H_C * VC_PAD, 0),
            pl.BlockSpec((N_CHUNK_B, 1, 2 * CTX_LEN), lambda b: (0, 0, 0)),
            spec(D_A, 0), spec(D_B, 0), spec(D_C, 0),
        ],
        out_specs=[
            pl.BlockSpec((1, CTX_LEN, D_A), lambda b: (b, 0, 0)),
            pl.BlockSpec((1, CTX_LEN, D_B), lambda b: (b, 0, 0)),
            pl.BlockSpec((1, CTX_LEN, D_C), lambda b: (b, 0, 0)),
        ],
        out_shape=[
            jax.ShapeDtypeStruct((BATCH, CTX_LEN, D_A), bf16),
            jax.ShapeDtypeStruct((BATCH, CTX_LEN, D_B), bf16),
            jax.ShapeDtypeStruct((BATCH, CTX_LEN, D_C), bf16),
        ],
        compiler_params=_cparams("arbitrary"),
        name="ctx_attention",
    )(qka, qka, qkb, qkb, vt_ab, qc, kc, vc, _sink_cols(sink, CTX_LEN), sza, szb, szc)


OUT_TM = 512


def _out_kernel(next_norm, x_ref, ga_ref, gb_ref, gc_ref, w_ref, mod_ref, *rest):
    g = jnp.concatenate([ga_ref[0], gb_ref[0], gc_ref[0]], axis=1)
    x_new = x_ref[0] + mod_ref[0, :, 2 * D_MODEL:] * _dot(g, w_ref[0])
    if next_norm:
        nw_ref, modn_ref = rest[0], rest[1]
        o_ref, h_ref = rest[-2], rest[-1]
        o_ref[0] = x_new
        h_ref[0] = _modulated_norm(x_new, nw_ref[...], modn_ref[0]).astype(bf16)
    else:
        rest[-1][0] = x_new


def _out_proj(x, ga, gb, gc, w_out_bf, layer, mod3, is_ctx, next_norm=None):
    t = x.shape[1]
    tm = min(t, OUT_TM)
    mod_row = (lambda b: BATCH) if is_ctx else (lambda b: b)
    row_block0 = SEQ // tm if is_ctx else 0
    in_specs = [
        pl.BlockSpec((1, tm, D_MODEL), lambda b, i: (b, i, 0)),
        pl.BlockSpec((1, tm, D_A), lambda b, i: (b, i, 0)),
        pl.BlockSpec((1, tm, D_B), lambda b, i: (b, i, 0)),
        pl.BlockSpec((1, tm, D_C), lambda b, i: (b, i, 0)),
        pl.BlockSpec((1, D_MIX, D_MODEL), lambda b, i: (layer, 0, 0), pipeline_mode=pl.Buffered(1)),
        pl.BlockSpec((1, 1, 3 * D_MODEL), lambda b, i: (mod_row(b), 0, 0)),
    ]
    args = [x, ga, gb, gc, w_out_bf, mod3]
    out_specs = [pl.BlockSpec((1, tm, D_MODEL), lambda b, i: (b, i, 0))]
    out_shape = [jax.ShapeDtypeStruct(x.shape, f32)]
    aliases = {}
    if next_norm is not None:
        norm_w, mod3_next, h_buf = next_norm
        in_specs += [pl.BlockSpec((1, D_MODEL), lambda b, i: (0, 0)),
                     pl.BlockSpec((1, 1, 3 * D_MODEL), lambda b, i: (mod_row(b), 0, 0))]
        args += [norm_w.reshape(1, D_MODEL), mod3_next]
        if h_buf is not None:
            in_specs.append(pl.BlockSpec(memory_space=pl.ANY))
            args.append(h_buf)
            aliases = {len(args) - 1: 1}
        out_specs.append(pl.BlockSpec((1, tm, D_MODEL), lambda b, i: (b, row_block0 + i, 0)))
        out_shape.append(jax.ShapeDtypeStruct((BATCH, S_ALL, D_MODEL), bf16))
    return pl.pallas_call(
        functools.partial(_out_kernel, next_norm is not None),
        grid=(BATCH, t // tm),
        in_specs=in_specs,
        out_specs=out_specs,
        out_shape=out_shape,
        input_output_aliases=aliases,
        compiler_params=_cparams("arbitrary", "arbitrary"),
        name="out_proj_ctx" if is_ctx else "out_proj",
    )(*args)


def _rope_tables():
    t = jnp.arange(SEQ)
    row = (t // GRID_W).astype(f32)
    col = (t % GRID_W).astype(f32)
    n_freq = ROPE_DIM // 4
    inv = ROPE_BASE ** (-jnp.arange(n_freq, dtype=f32) / n_freq)
    ar = row[:, None] * inv
    ac = col[:, None] * inv
    ang = jnp.concatenate([ar, ar, ac, ac], axis=-1)
    cos = jnp.cos(ang).astype(f32)
    sin = jnp.sin(ang).astype(f32)
    sign = jnp.asarray(np.where((np.arange(ROPE_DIM) % 32) < 16, -1.0, 1.0), dtype=f32)
    sin = sin * sign
    cos = jnp.concatenate([cos, jnp.ones((CTX_LEN, ROPE_DIM), f32)], axis=0)
    sin = jnp.concatenate([sin, jnp.zeros((CTX_LEN, ROPE_DIM), f32)], axis=0)
    pair = (jnp.concatenate([cos, cos], axis=1), jnp.concatenate([sin, sin], axis=1))
    single = (jnp.concatenate([cos, jnp.ones_like(cos)], axis=1), jnp.concatenate([sin, jnp.zeros_like(sin)], axis=1))
    return pair, single


def _pad_heads(w, width, padded):
    lead = w.shape[:-1]
    w = w.reshape(lead + (H_C, width))
    w = jnp.pad(w, [(0, 0)] * len(lead) + [(0, 0), (0, padded - width)])
    return w.reshape(lead + (H_C * padded,))


def kernel(x, c, ctx, c_ctx, norm_w, w_ada, b_ada, w_in, qn_a, kn_a, rpb_a, qn_b, kn_b, sink_b,
           qa_norm, kva_norm, w_qb, w_kvb, qn_c, kn_c, w_out):
    (cos2, sin2), (cos1, sin1) = _rope_tables()
    cvec = jnp.concatenate([c, c_ctx[None, :], jnp.zeros((MOD_ROWS - BATCH - 1, D_MODEL), f32)], axis=0)
    mod = _modulation(cvec, w_ada, b_ada)
    q_scale_ab = HD ** -0.5 * LOG2E
    q_scale_c = MLA_SCALE * LOG2E
    w_out_bf = w_out.astype(bf16)
    na_bias = _natten_bias(rpb_a)

    o_qa, o_ka, o_va, o_qb, o_kb, o_vb, o_cq, o_ckv, o_kpe, o_z = np.cumsum(
        (0, D_A, D_A, D_A, D_B, KV_B, KV_B, Q_LORA, KV_LORA, ROPE_DIM)).tolist()

    xc = ctx
    mod3_all = [mod[l].reshape(MOD_ROWS, 1, 3 * D_MODEL) for l in range(DEPTH)]
    h3 = None
    for l in range(DEPTH):
        last = l == DEPTH - 1
        mod3 = mod3_all[l]
        wl = w_in[l]
        wz = wl[:, o_z:].astype(bf16)
        if l == 0:
            sza, szb, szc, h = _proj_gate_norm(x, xc, norm_w[0], mod3, wz)
        else:
            h = h3.reshape(R_ALL, D_MODEL)
            sza, szb, szc = _proj_gate(h, wz)
        h3 = h.reshape(BATCH, S_ALL, D_MODEL)

        gain_hn = jnp.concatenate([jnp.tile(qn_a[l] * q_scale_ab, H_A), jnp.tile(kn_a[l], H_A),
                                   jnp.tile(qn_b[l] * q_scale_ab, H_B), jnp.tile(kn_b[l], HKV_B)])
        w_hn = jnp.concatenate([wl[:, o_qa:o_va], wl[:, o_qb:o_vb]], axis=1).astype(bf16)
        w_lora = jnp.pad(wl[:, o_cq:o_z], ((0, 0), (0, LANES - ROPE_DIM))).astype(bf16)
        w_v = jnp.concatenate([wl[:, o_va:o_qb], wl[:, o_vb:o_cq]], axis=1).astype(bf16)
        qka, qkb, cqn, ckvn, kpe, vt_ab = _proj_main(h, w_hn, w_lora, w_v, gain_hn, qa_norm[l], kva_norm[l],
                                                      cos2, sin2)

        wq = _pad_heads(w_qb[l], QK_C, QC_PAD).astype(bf16)
        wkv = w_kvb[l].reshape(KV_LORA, H_C, NOPE_C + V_C)
        wkn = wkv[:, :, :NOPE_C].reshape(KV_LORA, H_C * NOPE_C).astype(bf16)
        wv = wkv[:, :, NOPE_C:].reshape(KV_LORA, H_C * V_C).astype(bf16)
        gq = _pad_heads(jnp.tile(qn_c[l] * q_scale_c, H_C), QK_C, QC_PAD).reshape(1, H_C * QC_PAD)
        gk = jnp.pad(kn_c[l], (0, QC_PAD - QK_C)).reshape(1, QC_PAD)
        qc, kc, vc = _mla_up(cqn, ckvn, kpe, wq, wkn, wv, gq, gk, cos1, sin1)

        def b3(a):
            return a.reshape(BATCH, S_ALL, a.shape[-1])

        qka, qkb, qc, kc, vc, sza, szb, szc = map(b3, (qka, qkb, qc, kc, vc, sza, szb, szc))

        ga = _natten(qka, vt_ab, na_bias, l, sza)
        gb = _swa(qkb, vt_ab, sink_b[l], szb)
        gc = _mla(qc, kc, vc, szc)
        if last:
            (x,) = _out_proj(x, ga, gb, gc, w_out_bf, l, mod3, False)
        else:
            ga_c, gb_c, gc_c = _ctx_attention(qka, qkb, vt_ab, qc, kc, vc, sink_b[l], sza, szb, szc)
            x, h3 = _out_proj(x, ga, gb, gc, w_out_bf, l, mod3, False,
                              next_norm=(norm_w[l + 1], mod3_all[l + 1], h3))
            xc, h3 = _out_proj(xc, ga_c, gb_c, gc_c, w_out_bf, l, mod3, True,
                               next_norm=(norm_w[l + 1], mod3_all[l + 1], h3))
    return x
```

```python
import functools
import math

import numpy as np
import jax
import jax.numpy as jnp
from jax import lax
from jax.experimental import pallas as pl
from jax.experimental.pallas import tpu as pltpu

D_MODEL = 2048
BATCH = 8
SEQ = 2048
DEPTH = 2
GRID_W = 64
CTX_LEN = 256
HD = 64
H_A = 8
D_A = H_A * HD
H_B = 12
HKV_B = 4
D_B = H_B * HD
KV_B = HKV_B * HD
H_C = 6
NOPE_C = 128
ROPE_DIM = 64
QK_C = NOPE_C + ROPE_DIM
V_C = 128
D_C = H_C * V_C
Q_LORA = 768
KV_LORA = 512
D_MIX = D_A + D_B + D_C
NA_ROWS = 8
NA_COLS = 16
SW_WINDOW = 128
ROPE_BASE = 10000.0
EPS = 1e-6
NEG_INF = -1e30

S_ALL = SEQ + CTX_LEN
R_ALL = BATCH * S_ALL
GRID_ROWS = SEQ // GRID_W
LANES = 128
MXU_DIM = 256
BF16_ROWS = 16
QC_PAD = 256
VT_ROWS = LANES + BF16_ROWS
MOD_ROWS = 16
VMEM_LIMIT = 48 * 1024 * 1024
LOG2E = math.log2(math.e)

bf16 = jnp.bfloat16
f32 = jnp.float32


def _cparams(*sem):
    return pltpu.CompilerParams(dimension_semantics=sem, vmem_limit_bytes=VMEM_LIMIT)


def _silu(v):
    return v / (1.0 + jnp.exp(-v))


def _lane_iota(shape):
    return lax.broadcasted_iota(jnp.int32, shape, len(shape) - 1)


def _rope_rotate(xn, cos, sin_signed):
    lane = _lane_iota(xn.shape)
    take_next = (lane % 32) < 16
    rot = jnp.where(take_next, pltpu.roll(xn, LANES - 16, 1), pltpu.roll(xn, 16, 1))
    return xn * cos + rot * sin_signed


def _dot_nt(a, b):
    return lax.dot_general(a, b, (((1,), (1,)), ((), ())), preferred_element_type=f32)


def _dot(a, b):
    return jnp.dot(a, b, preferred_element_type=f32)


def _mod_kernel(c_ref, w_ref, b_ref, o_ref):
    sc = _silu(c_ref[...]).astype(bf16)
    o_ref[0] = _dot(sc, w_ref[0].astype(bf16)) + b_ref[0]


def _modulation(cvec, w_ada, b_ada):
    tn = 1024
    n = 3 * D_MODEL
    return pl.pallas_call(
        _mod_kernel,
        grid=(DEPTH, n // tn),
        in_specs=[
            pl.BlockSpec((MOD_ROWS, D_MODEL), lambda l, j: (0, 0)),
            pl.BlockSpec((1, D_MODEL, tn), lambda l, j: (l, 0, j)),
            pl.BlockSpec((1, 1, tn), lambda l, j: (l, 0, j)),
        ],
        out_specs=pl.BlockSpec((1, MOD_ROWS, tn), lambda l, j: (l, 0, j)),
        out_shape=jax.ShapeDtypeStruct((DEPTH, MOD_ROWS, n), f32),
        compiler_params=_cparams("arbitrary", "arbitrary"),
        name="modulation",
    )(cvec, w_ada, b_ada.reshape(DEPTH, 1, n))


H_ROWS = CTX_LEN
N_LAT_BLOCKS = SEQ // H_ROWS


def _modulated_norm(x, nw, mod):
    ms = jnp.mean(x * x, axis=-1, keepdims=True)
    y = x * lax.rsqrt(ms + EPS) * nw
    return y * (1.0 + mod[:, D_MODEL:2 * D_MODEL]) + mod[:, 0:D_MODEL]


PROJ_TM = 768


def _group_mean_sq(u):
    sq = u * u
    low = _lane_iota((u.shape[0], LANES)) < HD
    parts = []
    for c in range(0, u.shape[1], LANES):
        t = sq[:, c:c + LANES]
        s_low = jnp.sum(jnp.where(low, t, 0.0), axis=-1, keepdims=True)
        s_high = jnp.sum(jnp.where(low, 0.0, t), axis=-1, keepdims=True)
        parts.append(jnp.where(low, s_low, s_high) * (1.0 / HD))
    return jnp.concatenate(parts, axis=1)


def _run_items(items):
    u = items[0][0]()
    for i, (_, epilogue) in enumerate(items):
        u_next = items[i + 1][0]() if i + 1 < len(items) else None
        epilogue(u)
        u = u_next


def _store_vt_tile(vt_ref, p, ut):
    vt_ref[p, 0:LANES, :] = ut.astype(bf16)
    vt_ref[p, LANES:VT_ROWS, :] = jnp.ones((VT_ROWS - LANES, ut.shape[1]), bf16)


PROJ_TN = 512
N_HN = 2 * D_A + D_B + KV_B
N_LORA = Q_LORA + KV_LORA + LANES
N_VT = D_A + KV_B


def _proj_main_kernel(h_ref, whn_ref, wl_ref, wv_ref, ghn_ref, cos_ref, sin_ref, gcq_ref, gckv_ref,
                      qka_ref, qkb_ref, cqn_ref, ckvn_ref, kpe_ref, vt_ref):
    def hn_item(k):
        cols = slice(k * PROJ_TN, (k + 1) * PROJ_TN)
        rope = k * PROJ_TN >= 2 * D_A
        o_ref = qkb_ref if rope else qka_ref
        o0 = k * PROJ_TN - (2 * D_A if rope else 0)

        def epilogue(u):
            for c0 in range(0, PROJ_TN, MXU_DIM):
                uc = u[:, c0:c0 + MXU_DIM]
                ms = _group_mean_sq(uc)
                xn = uc * lax.rsqrt(ms + EPS) * ghn_ref[:, k * PROJ_TN + c0:k * PROJ_TN + c0 + MXU_DIM]
                if rope:
                    for c1 in range(0, MXU_DIM, LANES):
                        o_ref[:, o0 + c0 + c1:o0 + c0 + c1 + LANES] = _rope_rotate(
                            xn[:, c1:c1 + LANES], cos_ref[...], sin_ref[...]).astype(bf16)
                else:
                    o_ref[:, o0 + c0:o0 + c0 + MXU_DIM] = xn.astype(bf16)

        return lambda: _dot(h_ref[...], whn_ref[:, cols]), epilogue

    def rowrms_item(c0, width, gain_ref, o_ref):
        def epilogue(u):
            ms = jnp.mean(u * u, axis=-1, keepdims=True)
            o_ref[...] = (u * lax.rsqrt(ms + EPS) * gain_ref[...]).astype(bf16)

        return lambda: _dot(h_ref[...], wl_ref[:, c0:c0 + width]), epilogue

    def kpe_item():
        def epilogue(u):
            kpe_ref[...] = u

        return lambda: _dot(h_ref[...], wl_ref[:, Q_LORA + KV_LORA:N_LORA]), epilogue

    def vt_item(j):
        def epilogue(u):
            for i in range(2):
                _store_vt_tile(vt_ref, 2 * j + i, u[:, i * LANES:(i + 1) * LANES].T)

        return lambda: _dot(h_ref[...], wv_ref[:, j * MXU_DIM:(j + 1) * MXU_DIM]), epilogue

    _run_items([hn_item(k) for k in range(N_HN // PROJ_TN)]
               + [rowrms_item(0, Q_LORA, gcq_ref, cqn_ref), rowrms_item(Q_LORA, KV_LORA, gckv_ref, ckvn_ref),
                  kpe_item()]
               + [vt_item(j) for j in range(N_VT // MXU_DIM)])


def _resident(shape):
    return pl.BlockSpec(shape, lambda i: (0,) * len(shape), pipeline_mode=pl.Buffered(1))


def _proj_main(h2d, whn, wl, wv, ghn, gcq, gckv, cos, sin):
    tm = PROJ_TM
    nt = S_ALL // tm
    row = lambda i: (i, 0)
    tiles = N_VT // LANES
    return pl.pallas_call(
        _proj_main_kernel,
        grid=(R_ALL // tm,),
        in_specs=[
            pl.BlockSpec((tm, D_MODEL), row),
            _resident((D_MODEL, N_HN)), _resident((D_MODEL, N_LORA)), _resident((D_MODEL, N_VT)),
            _resident((1, N_HN)),
            pl.BlockSpec((tm, LANES), lambda i: (i % nt, 0)),
            pl.BlockSpec((tm, LANES), lambda i: (i % nt, 0)),
            _resident((1, Q_LORA)), _resident((1, KV_LORA)),
        ],
        out_specs=[
            pl.BlockSpec((tm, 2 * D_A), row),
            pl.BlockSpec((tm, D_B + KV_B), row),
            pl.BlockSpec((tm, Q_LORA), row),
            pl.BlockSpec((tm, KV_LORA), row),
            pl.BlockSpec((tm, LANES), row),
            pl.BlockSpec((tiles, VT_ROWS, tm), lambda i: (0, 0, i)),
        ],
        out_shape=[
            jax.ShapeDtypeStruct((R_ALL, 2 * D_A), bf16),
            jax.ShapeDtypeStruct((R_ALL, D_B + KV_B), bf16),
            jax.ShapeDtypeStruct((R_ALL, Q_LORA), bf16),
            jax.ShapeDtypeStruct((R_ALL, KV_LORA), bf16),
            jax.ShapeDtypeStruct((R_ALL, LANES), f32),
            jax.ShapeDtypeStruct((tiles, VT_ROWS, R_ALL), bf16),
        ],
        compiler_params=_cparams("arbitrary"),
        name="proj_main",
    )(h2d, whn, wl, wv, ghn.reshape(1, N_HN), cos, sin,
      gcq.reshape(1, Q_LORA), gckv.reshape(1, KV_LORA))


N_CHUNK_TILE = PROJ_TM // H_ROWS
TILES_PER_BATCH = S_ALL // PROJ_TM


def _proj_gate_norm_kernel(*refs):
    x_refs = refs[:N_CHUNK_TILE]
    ctx_ref, nw_ref, modl_ref, modc_ref, wz_ref, sza_ref, szb_ref, szc_ref, h_ref = refs[N_CHUNK_TILE:]
    last_tile = pl.program_id(0) % TILES_PER_BATCH == TILES_PER_BATCH - 1
    for j, x_ref in enumerate(x_refs):
        x, mod = x_ref[0], modl_ref[0]
        if j == N_CHUNK_TILE - 1:
            x = jnp.where(last_tile, ctx_ref[0], x)
            mod = jnp.where(last_tile, modc_ref[0], mod)
        h_ref[j * H_ROWS:(j + 1) * H_ROWS, :] = _modulated_norm(x, nw_ref[...], mod).astype(bf16)
    _proj_gate_kernel(h_ref, wz_ref, sza_ref, szb_ref, szc_ref)


def _proj_gate_kernel(h_ref, wz_ref, sza_ref, szb_ref, szc_ref):
    outs = ((sza_ref, 0, D_A), (szb_ref, D_A, D_B), (szc_ref, D_A + D_B, D_C))

    def item(k):
        c0 = k * PROJ_TN

        def epilogue(u):
            sz = _silu(u)
            for o_ref, start, width in outs:
                lo, hi = max(c0, start), min(c0 + PROJ_TN, start + width)
                if lo < hi:
                    o_ref[:, lo - start:hi - start] = sz[:, lo - c0:hi - c0]

        return lambda: _dot(h_ref[...], wz_ref[:, c0:c0 + PROJ_TN]), epilogue

    _run_items([item(k) for k in range(D_MIX // PROJ_TN)])


def _proj_gate_norm(x, xc, norm_w, mod3, wz):
    tm = PROJ_TM
    row = lambda i: (i, 0)

    def chunk(j):
        return pl.BlockSpec((1, H_ROWS, D_MODEL), lambda i: (
            i // TILES_PER_BATCH, jnp.minimum((i % TILES_PER_BATCH) * N_CHUNK_TILE + j, N_LAT_BLOCKS - 1), 0))

    return pl.pallas_call(
        _proj_gate_norm_kernel,
        grid=(R_ALL // tm,),
        in_specs=[
            *[chunk(j) for j in range(N_CHUNK_TILE)],
            pl.BlockSpec((1, H_ROWS, D_MODEL), lambda i: (i // TILES_PER_BATCH, 0, 0)),
            pl.BlockSpec((1, D_MODEL), lambda i: (0, 0)),
            pl.BlockSpec((1, 1, 3 * D_MODEL), lambda i: (i // TILES_PER_BATCH, 0, 0)),
            pl.BlockSpec((1, 1, 3 * D_MODEL), lambda i: (BATCH, 0, 0)),
            _resident((D_MODEL, D_MIX)),
        ],
        out_specs=[pl.BlockSpec((tm, D_A), row), pl.BlockSpec((tm, D_B), row), pl.BlockSpec((tm, D_C), row),
                   pl.BlockSpec((tm, D_MODEL), row)],
        out_shape=[jax.ShapeDtypeStruct((R_ALL, D_A), f32), jax.ShapeDtypeStruct((R_ALL, D_B), f32),
                   jax.ShapeDtypeStruct((R_ALL, D_C), f32), jax.ShapeDtypeStruct((R_ALL, D_MODEL), bf16)],
        compiler_params=_cparams("arbitrary"),
        name="proj_gate_norm",
    )(*([x] * N_CHUNK_TILE), xc, norm_w.reshape(1, D_MODEL), mod3, mod3, wz)


def _proj_gate(h2d, wz):
    tm = PROJ_TM
    row = lambda i: (i, 0)
    return pl.pallas_call(
        _proj_gate_kernel,
        grid=(R_ALL // tm,),
        in_specs=[pl.BlockSpec((tm, D_MODEL), row), _resident((D_MODEL, D_MIX))],
        out_specs=[pl.BlockSpec((tm, D_A), row), pl.BlockSpec((tm, D_B), row), pl.BlockSpec((tm, D_C), row)],
        out_shape=[jax.ShapeDtypeStruct((R_ALL, D_A), f32), jax.ShapeDtypeStruct((R_ALL, D_B), f32),
                   jax.ShapeDtypeStruct((R_ALL, D_C), f32)],
        compiler_params=_cparams("arbitrary"),
        name="proj_gate",
    )(h2d, wz)


def _mla_up_kernel(cq_ref, ckv_ref, kpe_ref, wq_ref, wkn_ref, wv_ref, gq_ref, gk_ref,
                   cos_ref, sin_ref, qc_ref, kc_ref, vc_ref):
    cos = cos_ref[...]
    sin = sin_ref[...]
    kpe = kpe_ref[...]
    ss_pe = jnp.sum(kpe * kpe, axis=-1, keepdims=True)
    k_rot = _rope_rotate(kpe * gk_ref[:, LANES:], cos, sin)

    def q_head(h):
        cols = slice(h * QC_PAD, (h + 1) * QC_PAD)
        return lambda: _dot(cq_ref[...], wq_ref[:, cols]), functools.partial(q_epilogue, h)

    def q_epilogue(h, u):
        ms = jnp.sum(u * u, axis=-1, keepdims=True) * (1.0 / QK_C)
        xn = u * lax.rsqrt(ms + EPS) * gq_ref[:, h * QC_PAD:(h + 1) * QC_PAD]
        qc_ref[:, h * QC_PAD:h * QC_PAD + LANES] = xn[:, :LANES].astype(bf16)
        qc_ref[:, h * QC_PAD + LANES:(h + 1) * QC_PAD] = _rope_rotate(xn[:, LANES:], cos, sin).astype(bf16)

    def k_head(h):
        cols = slice(h * NOPE_C, (h + 1) * NOPE_C)
        return lambda: _dot(ckv_ref[...], wkn_ref[:, cols]), functools.partial(k_epilogue, h)

    def k_epilogue(h, u):
        ms = (jnp.sum(u * u, axis=-1, keepdims=True) + ss_pe) * (1.0 / QK_C)
        r = lax.rsqrt(ms + EPS)
        kc_ref[:, h * QC_PAD:h * QC_PAD + LANES] = (u * r * gk_ref[:, 0:LANES]).astype(bf16)
        kc_ref[:, h * QC_PAD + LANES:(h + 1) * QC_PAD] = (k_rot * r).astype(bf16)

    def v_head(h):
        def epilogue(u):
            vc_ref[:, h * VC_PAD:h * VC_PAD + V_C] = u.astype(bf16)
            vc_ref[:, h * VC_PAD + V_C:(h + 1) * VC_PAD] = jnp.ones((u.shape[0], VC_PAD - V_C), bf16)

        return lambda: _dot(ckv_ref[...], wv_ref[:, h * V_C:(h + 1) * V_C]), epilogue

    _run_items([q_head(h) for h in range(H_C)] + [k_head(h) for h in range(H_C)] + [v_head(h) for h in range(H_C)])


MLA_UP_TM = 384


def _mla_up(cqn, ckvn, kpe, wq, wkn, wv, gq, gk, cos, sin):
    tm = MLA_UP_TM
    nt = S_ALL // tm
    row = lambda i: (i, 0)
    fixed = lambda i: (0, 0)
    return pl.pallas_call(
        _mla_up_kernel,
        grid=(R_ALL // tm,),
        in_specs=[
            pl.BlockSpec((tm, Q_LORA), row),
            pl.BlockSpec((tm, KV_LORA), row),
            pl.BlockSpec((tm, LANES), row),
            pl.BlockSpec((Q_LORA, H_C * QC_PAD), fixed),
            pl.BlockSpec((KV_LORA, H_C * NOPE_C), fixed),
            pl.BlockSpec((KV_LORA, H_C * V_C), fixed),
            pl.BlockSpec((1, H_C * QC_PAD), fixed),
            pl.BlockSpec((1, QC_PAD), fixed),
            pl.BlockSpec((tm, LANES), lambda i: (i % nt, 0)),
            pl.BlockSpec((tm, LANES), lambda i: (i % nt, 0)),
        ],
        out_specs=[
            pl.BlockSpec((tm, H_C * QC_PAD), row),
            pl.BlockSpec((tm, H_C * QC_PAD), row),
            pl.BlockSpec((tm, H_C * VC_PAD), row),
        ],
        out_shape=[
            jax.ShapeDtypeStruct((R_ALL, H_C * QC_PAD), bf16),
            jax.ShapeDtypeStruct((R_ALL, H_C * QC_PAD), bf16),
            jax.ShapeDtypeStruct((R_ALL, H_C * VC_PAD), bf16),
        ],
        compiler_params=_cparams("arbitrary"),
        name="mla_up",
    )(cqn, ckvn, kpe, wq, wkn, wv, gq, gk, cos, sin)


def _pipelined(n, score_fn, value_fn):
    _run_items([(functools.partial(score_fn, j), functools.partial(value_fn, j)) for j in range(n)])


def _col_max(s):
    return jnp.max(s, axis=0, keepdims=True)


def _prob(s, m):
    return jnp.exp2((s - m).astype(bf16))


def _pair_q(qv, half_even, half_odd):
    low = _lane_iota(qv.shape) < HD
    zero = jnp.zeros_like(qv)
    qa = jnp.where(low, qv, zero) if half_even == 0 else jnp.where(low, zero, pltpu.roll(qv, HD, 1))
    qb = jnp.where(low, zero, qv) if half_odd == 1 else jnp.where(low, pltpu.roll(qv, HD, 1), zero)
    return jnp.concatenate([qa, qb], axis=0).astype(bf16)


def _pair_out(acc, l, n, half_even, half_odd):
    o = acc[:LANES] / l
    t = jnp.concatenate([o[half_even * HD:(half_even + 1) * HD, 0:n],
                         o[half_odd * HD:(half_odd + 1) * HD, n:2 * n]], axis=0)
    return t.T


NA_QROWS = 2
NA_Q = NA_QROWS * GRID_W
NA_KROWS = NA_ROWS + NA_QROWS
NA_KEYS = NA_KROWS * GRID_W
NA_VARIANTS = ((0, 0, 0, -1), (0, -2, 0, -3), (0, -4, 1, -5), (0, -4, 0, -5), (0, -6, 0, -7))


def _na_window_row(j):
    return jnp.clip(NA_QROWS * j - NA_ROWS // 2, 0, GRID_ROWS - NA_ROWS)


NA_SUB = 4
NA_BLOCKS = GRID_ROWS // NA_QROWS


def _natten_kernel(q_ref, k_ref, vt_ref, *rest):
    bias_refs = rest[:NA_SUB]
    sz_ref, o_ref, s_scr = rest[NA_SUB:]
    step = pl.program_id(1)
    k0 = [pl.multiple_of(_na_window_row(NA_SUB * step + s) * GRID_W, LANES) for s in range(NA_SUB)]
    n_pair = H_A // 2

    def scores(i):
        s, p = divmod(i, n_pair)
        cols = slice(p * LANES, (p + 1) * LANES)
        q2 = _pair_q(q_ref[0, s * NA_Q:(s + 1) * NA_Q, cols].astype(f32), 0, 1)
        s_loc = _dot_nt(k_ref[0, pl.ds(k0[s], NA_KEYS), cols], q2) + bias_refs[s][0, 0, p]
        s_ctx = _dot_nt(k_ref[0, SEQ:S_ALL, cols], q2)
        s_scr[i % 2, 0:NA_KEYS, :] = s_loc
        s_scr[i % 2, NA_KEYS:, :] = s_ctx
        return jnp.maximum(_col_max(s_loc), _col_max(s_ctx))

    def values(i, m):
        s, p = divmod(i, n_pair)
        cols = slice(p * LANES, (p + 1) * LANES)
        rows = slice(s * NA_Q, (s + 1) * NA_Q)
        acc = (_dot(vt_ref[p, :, pl.ds(k0[s], NA_KEYS)], _prob(s_scr[i % 2, 0:NA_KEYS, :], m))
               + _dot(vt_ref[p, :, SEQ:S_ALL], _prob(s_scr[i % 2, NA_KEYS:, :], m)))
        o = _pair_out(acc, acc[LANES:LANES + 1], NA_Q, 0, 1)
        o_ref[0, rows, cols] = (o * sz_ref[0, rows, cols]).astype(bf16)

    _pipelined(NA_SUB * n_pair, scores, values)


def _natten(qk, vt, bias, layer, sz):
    def variant(s):
        def index(b, step):
            j = NA_SUB * step + s
            v = jnp.where(j <= 1, j, jnp.where(j >= NA_BLOCKS - 2, j - (NA_BLOCKS - 5), 2))
            return (layer, v, 0, 0, 0)
        return index

    tq = NA_SUB * NA_Q
    bias_block = (1, 1, H_A // 2, NA_KEYS, 2 * NA_Q)
    return pl.pallas_call(
        _natten_kernel,
        grid=(BATCH, NA_BLOCKS // NA_SUB),
        in_specs=[
            pl.BlockSpec((1, tq, D_A), lambda b, j: (b, j, 0)),
            pl.BlockSpec((1, S_ALL, D_A), lambda b, j: (b, 0, 1)),
            pl.BlockSpec((H_A // 2, VT_ROWS, S_ALL), lambda b, j: (0, 0, b)),
            *[pl.BlockSpec(bias_block, variant(s)) for s in range(NA_SUB)],
            pl.BlockSpec((1, tq, D_A), lambda b, j: (b, j, 0)),
        ],
        out_specs=pl.BlockSpec((1, tq, D_A), lambda b, j: (b, j, 0)),
        out_shape=jax.ShapeDtypeStruct((BATCH, SEQ, D_A), bf16),
        scratch_shapes=[pltpu.VMEM((2, NA_KEYS + CTX_LEN, 2 * NA_Q), f32)],
        compiler_params=_cparams("arbitrary", "arbitrary"),
        name="natten",
    )(qk, qk, vt, *([bias] * NA_SUB), sz)


def _natten_bias(rpb):
    n_dc = 2 * NA_COLS - 1
    n_dr = NA_KROWS + NA_ROWS - 1
    c = np.arange(LANES) % GRID_W
    kc = np.arange(GRID_W)
    qstart = np.clip(c - NA_COLS // 2, 0, GRID_W - NA_COLS)
    col_ok = (kc[:, None] >= qstart[None, :]) & (kc[:, None] < qstart[None, :] + NA_COLS)
    dc = np.clip(kc[:, None] - c[None, :], -(NA_COLS - 1), NA_COLS - 1) + NA_COLS - 1
    onehot = jnp.asarray((dc[None] == np.arange(n_dc)[:, None, None]).astype(np.float32))
    exp = jnp.einsum("lhrd,dkc->lhrkc", rpb.astype(f32) * LOG2E, onehot, precision=lax.Precision.HIGHEST)
    exp = jnp.pad(exp, ((0, 0), (0, 0), (0, n_dr - exp.shape[2]), (0, 0), (0, 0)))
    col_mask = jnp.asarray(np.where(col_ok, 0.0, NEG_INF).astype(np.float32))
    n_var = len(NA_VARIANTS)
    return pl.pallas_call(
        _natten_bias_kernel,
        grid=(DEPTH, n_var),
        in_specs=[
            pl.BlockSpec((1, H_A, n_dr, GRID_W, LANES), lambda l, v: (l, 0, 0, 0, 0)),
            pl.BlockSpec((GRID_W, LANES), lambda l, v: (0, 0)),
        ],
        out_specs=pl.BlockSpec((1, 1, H_A // 2, NA_KEYS, 2 * NA_Q), lambda l, v: (l, v, 0, 0, 0)),
        out_shape=jax.ShapeDtypeStruct((DEPTH, n_var, H_A // 2, NA_KEYS, 2 * NA_Q), f32),
        compiler_params=_cparams("arbitrary", "arbitrary"),
        name="natten_bias",
    )(exp, col_mask)


def _natten_bias_kernel(exp_ref, mask_ref, o_ref):
    v = pl.program_id(1)

    def pick(column):
        out = jnp.int32(0)
        for i, var in enumerate(NA_VARIANTS):
            out = jnp.where(v == i, var[column], out)
        return out

    lo = (pick(0), pick(2))
    off = (pick(1), pick(3))
    low = _lane_iota((GRID_W, LANES)) < GRID_W
    mask = mask_ref[...]
    for kr in range(NA_KROWS):
        valid = [(kr >= lo[qr]) & (kr < lo[qr] + NA_ROWS) for qr in range(NA_QROWS)]
        for h in range(H_A):
            halves = []
            for qr in range(NA_QROWS):
                blk = exp_ref[0, h, pl.ds(kr + off[qr] + NA_ROWS - 1, 1)][0] + mask
                halves.append(jnp.where(valid[qr], blk, NEG_INF))
            e = h % 2
            o_ref[0, 0, h // 2, kr * GRID_W:(kr + 1) * GRID_W, e * NA_Q:(e + 1) * NA_Q] = jnp.where(
                low, halves[0], halves[1])


SW_Q = 128
SW_SPAN = SW_Q + 2 * SW_WINDOW
G_B = H_B // HKV_B
N_CHUNK_B = H_B // 2


def _gqa_halves(c):
    return ((2 * c) // G_B) % 2, ((2 * c + 1) // G_B) % 2


SW_SUB = 8
SW_BLOCKS = SEQ // SW_Q


def _swa_kernel(q_ref, k_ref, vt_ref, mask_ref, sink_ref, sz_ref, o_ref, s_scr):
    step = pl.program_id(1)
    blocks = [SW_SUB * step + s for s in range(SW_SUB)]
    start = [pl.multiple_of(jnp.clip((n - 1) * SW_Q, 0, SEQ - SW_SPAN), SW_Q) for n in blocks]
    variant = [jnp.where(n == 0, 0, jnp.where(n == SW_BLOCKS - 1, 2, 1)) for n in blocks]

    def scores(i):
        s, c = divmod(i, N_CHUNK_B)
        kp = c // G_B
        cols = slice(c * LANES, (c + 1) * LANES)
        kcols = slice(kp * LANES, (kp + 1) * LANES)
        q2 = _pair_q(q_ref[0, s * SW_Q:(s + 1) * SW_Q, cols].astype(f32), *_gqa_halves(c))
        s_loc = _dot_nt(k_ref[0, pl.ds(start[s], SW_SPAN), kcols], q2) + mask_ref[variant[s]]
        s_ctx = _dot_nt(k_ref[0, SEQ:S_ALL, kcols], q2)
        s_scr[i % 2, 0:SW_SPAN, :] = s_loc
        s_scr[i % 2, SW_SPAN:, :] = s_ctx
        return jnp.maximum(jnp.maximum(_col_max(s_loc), _col_max(s_ctx)), sink_ref[c])

    def values(i, m):
        s, c = divmod(i, N_CHUNK_B)
        kp = c // G_B
        cols = slice(c * LANES, (c + 1) * LANES)
        rows = slice(s * SW_Q, (s + 1) * SW_Q)
        acc = (_dot(vt_ref[kp, :, pl.ds(start[s], SW_SPAN)], _prob(s_scr[i % 2, 0:SW_SPAN, :], m))
               + _dot(vt_ref[kp, :, SEQ:S_ALL], _prob(s_scr[i % 2, SW_SPAN:, :], m)))
        l = acc[LANES:LANES + 1] + jnp.exp2(sink_ref[c] - m)
        o = _pair_out(acc, l, SW_Q, *_gqa_halves(c))
        o_ref[0, rows, cols] = (o * sz_ref[0, rows, cols]).astype(bf16)

    _pipelined(SW_SUB * N_CHUNK_B, scores, values)


def _swa_mask():
    i = np.arange(SW_Q)
    j = np.arange(SW_SPAN)
    out = []
    for d0 in (0, SW_Q, 2 * SW_Q):
        ok = np.abs(d0 + i[None, :] - j[:, None]) <= SW_WINDOW
        m = np.where(ok, 0.0, NEG_INF).astype(np.float32)
        out.append(np.concatenate([m, m], axis=1))
    return jnp.asarray(np.stack(out))


def _sink_cols(sink, n):
    return jnp.repeat(sink.astype(f32).reshape(N_CHUNK_B, 2) * LOG2E, n, axis=1)[:, None, :]


def _swa(qk, vt, sink, sz):
    tq = SW_SUB * SW_Q
    return pl.pallas_call(
        _swa_kernel,
        grid=(BATCH, SW_BLOCKS // SW_SUB),
        in_specs=[
            pl.BlockSpec((1, tq, D_B), lambda b, n: (b, n, 0)),
            pl.BlockSpec((1, S_ALL, KV_B), lambda b, n: (b, 0, D_B // KV_B)),
            pl.BlockSpec((HKV_B // 2, VT_ROWS, S_ALL), lambda b, n: (H_A // HKV_B, 0, b)),
            pl.BlockSpec((3, SW_SPAN, 2 * SW_Q), lambda b, n: (0, 0, 0)),
            pl.BlockSpec((N_CHUNK_B, 1, 2 * SW_Q), lambda b, n: (0, 0, 0)),
            pl.BlockSpec((1, tq, D_B), lambda b, n: (b, n, 0)),
        ],
        out_specs=pl.BlockSpec((1, tq, D_B), lambda b, n: (b, n, 0)),
        out_shape=jax.ShapeDtypeStruct((BATCH, SEQ, D_B), bf16),
        scratch_shapes=[pltpu.VMEM((2, SW_SPAN + CTX_LEN, 2 * SW_Q), f32)],
        compiler_params=_cparams("arbitrary", "arbitrary"),
        name="swa",
    )(qk, qk, vt, _swa_mask(), _sink_cols(sink, SW_Q), sz)


MLA_TQ = SEQ
MLA_SUB = 512
MLA_KTILE = MXU_DIM
MLA_SCALE = float(QK_C) ** -0.5
VC_PAD = 2 * V_C


def _mla_kernel(q_ref, k_ref, v_ref, sz_ref, o_ref, s_scr, p_scr):
    tiles = [slice(c * MLA_KTILE, (c + 1) * MLA_KTILE) for c in range(S_ALL // MLA_KTILE)]

    def scores(j):
        q = q_ref[0, j * MLA_SUB:(j + 1) * MLA_SUB, :]
        m = None
        for kt in tiles:
            s = _dot_nt(q, k_ref[0, kt, :])
            s_scr[j % 2, :, kt] = s
            mc = jnp.maximum(s[:, :LANES], s[:, LANES:])
            m = mc if m is None else jnp.maximum(m, mc)
        return jnp.max(m, axis=-1, keepdims=True)

    def values(j, m):
        for kt in tiles:
            p_scr[:, kt] = _prob(s_scr[j % 2, :, kt], m)
        acc = _dot(p_scr[...], v_ref[0])
        rows = slice(j * MLA_SUB, (j + 1) * MLA_SUB)
        o_ref[0, rows, :] = (acc[:, :V_C] / acc[:, V_C:] * sz_ref[0, rows, :]).astype(bf16)

    _pipelined(MLA_TQ // MLA_SUB, scores, values)


def _mla(qc, kc, vc, sz):
    return pl.pallas_call(
        _mla_kernel,
        grid=(BATCH, H_C),
        in_specs=[
            pl.BlockSpec((1, MLA_TQ, QC_PAD), lambda b, h: (b, 0, h)),
            pl.BlockSpec((1, S_ALL, QC_PAD), lambda b, h: (b, 0, h)),
            pl.BlockSpec((1, S_ALL, VC_PAD), lambda b, h: (b, 0, h)),
            pl.BlockSpec((1, MLA_TQ, V_C), lambda b, h: (b, 0, h)),
        ],
        out_specs=pl.BlockSpec((1, MLA_TQ, V_C), lambda b, h: (b, 0, h)),
        out_shape=jax.ShapeDtypeStruct((BATCH, SEQ, D_C), bf16),
        scratch_shapes=[pltpu.VMEM((2, MLA_SUB, S_ALL), f32), pltpu.VMEM((MLA_SUB, S_ALL), bf16)],
        compiler_params=_cparams("arbitrary", "arbitrary"),
        name="mla",
    )(qc, kc, vc, sz)


def _ctx_kernel(qa_ref, ka_ref, qb_ref, kb_ref, vtab_ref, qc_ref, kc_ref, vc_ref,
                sink_ref, sza_ref, szb_ref, szc_ref, oa_ref, ob_ref, oc_ref):
    n = CTX_LEN

    def item_a(p):
        cols = slice(p * LANES, (p + 1) * LANES)

        def values(s):
            acc = _dot(vtab_ref[p], _prob(s, _col_max(s)))
            o = _pair_out(acc, acc[LANES:LANES + 1], n, 0, 1)
            oa_ref[0, :, cols] = (o * sza_ref[0, :, cols]).astype(bf16)

        return lambda: _dot_nt(ka_ref[0, :, cols], _pair_q(qa_ref[0, :, cols].astype(f32), 0, 1)), values

    def item_b(c):
        kp = c // G_B
        cols = slice(c * LANES, (c + 1) * LANES)

        def values(s):
            m = jnp.maximum(_col_max(s), sink_ref[c])
            acc = _dot(vtab_ref[H_A // 2 + kp], _prob(s, m))
            l = acc[LANES:LANES + 1] + jnp.exp2(sink_ref[c] - m)
            o = _pair_out(acc, l, n, *_gqa_halves(c))
            ob_ref[0, :, cols] = (o * szb_ref[0, :, cols]).astype(bf16)

        return lambda: _dot_nt(kb_ref[0, :, kp * LANES:(kp + 1) * LANES],
                               _pair_q(qb_ref[0, :, cols].astype(f32), *_gqa_halves(c))), values

    def item_c(h):
        qcols = slice(h * QC_PAD, (h + 1) * QC_PAD)
        vcols = slice(h * V_C, (h + 1) * V_C)

        def values(s):
            p = _prob(s, jnp.max(s, axis=-1, keepdims=True))
            acc = _dot(p, vc_ref[0, :, h * VC_PAD:(h + 1) * VC_PAD])
            oc_ref[0, :, vcols] = (acc[:, :V_C] / acc[:, V_C:] * szc_ref[0, :, vcols]).astype(bf16)

        return lambda: _dot_nt(qc_ref[0, :, qcols], kc_ref[0, :, qcols]), values

    _run_items([item_a(p) for p in range(H_A // 2)] + [item_b(c) for c in range(N_CHUNK_B)]
               + [item_c(h) for h in range(H_C)])


def _ctx_attention(qka, qkb, vt_ab, qc, kc, vc, sink, sza, szb, szc):
    cb = SEQ // CTX_LEN

    def spec(width, col):
        return pl.BlockSpec((1, CTX_LEN, width), lambda b: (b, cb, col))

    def vt_spec(tiles):
        return pl.BlockSpec((tiles, VT_ROWS, CTX_LEN), lambda b: (0, 0, b * (S_ALL // CTX_LEN) + cb))

    return pl.pallas_call(
        _ctx_kernel,
        grid=(BATCH,),
        in_specs=[
            spec(D_A, 0), spec(D_A, 1),
            spec(D_B, 0), spec(KV_B, D_B // KV_B),
            vt_spec((H_A + HKV_B) // 2),
            spec(H_C * QC_PAD, 0), spec(H_C * QC_PAD, 0), spec(H_C * VC_PAD, 0),
            pl.BlockSpec((N_CHUNK_B, 1, 2 * CTX_LEN), lambda b: (0, 0, 0)),
            spec(D_A, 0), spec(D_B, 0), spec(D_C, 0),
        ],
        out_specs=[
            pl.BlockSpec((1, CTX_LEN, D_A), lambda b: (b, 0, 0)),
            pl.BlockSpec((1, CTX_LEN, D_B), lambda b: (b, 0, 0)),
            pl.BlockSpec((1, CTX_LEN, D_C), lambda b: (b, 0, 0)),
        ],
        out_shape=[
            jax.ShapeDtypeStruct((BATCH, CTX_LEN, D_A), bf16),
            jax.ShapeDtypeStruct((BATCH, CTX_LEN, D_B), bf16),
            jax.ShapeDtypeStruct((BATCH, CTX_LEN, D_C), bf16),
        ],
        compiler_params=_cparams("arbitrary"),
        name="ctx_attention",
    )(qka, qka, qkb, qkb, vt_ab, qc, kc, vc, _sink_cols(sink, CTX_LEN), sza, szb, szc)


OUT_TM = 512


def _out_kernel(next_norm, x_ref, ga_ref, gb_ref, gc_ref, w_ref, mod_ref, *rest):
    g = jnp.concatenate([ga_ref[0], gb_ref[0], gc_ref[0]], axis=1)
    x_new = x_ref[0] + mod_ref[0, :, 2 * D_MODEL:] * _dot(g, w_ref[0])
    if next_norm:
        nw_ref, modn_ref = rest[0], rest[1]
        o_ref, h_ref = rest[-2], rest[-1]
        o_ref[0] = x_new
        h_ref[0] = _modulated_norm(x_new, nw_ref[...], modn_ref[0]).astype(bf16)
    else:
        rest[-1][0] = x_new


def _out_proj(x, ga, gb, gc, w_out_bf, layer, mod3, is_ctx, next_norm=None):
    t = x.shape[1]
    tm = min(t, OUT_TM)
    mod_row = (lambda b: BATCH) if is_ctx else (lambda b: b)
    row_block0 = SEQ // tm if is_ctx else 0
    in_specs = [
        pl.BlockSpec((1, tm, D_MODEL), lambda b, i: (b, i, 0)),
        pl.BlockSpec((1, tm, D_A), lambda b, i: (b, i, 0)),
        pl.BlockSpec((1, tm, D_B), lambda b, i: (b, i, 0)),
        pl.BlockSpec((1, tm, D_C), lambda b, i: (b, i, 0)),
        pl.BlockSpec((1, D_MIX, D_MODEL), lambda b, i: (layer, 0, 0), pipeline_mode=pl.Buffered(1)),
        pl.BlockSpec((1, 1, 3 * D_MODEL), lambda b, i: (mod_row(b), 0, 0)),
    ]
    args = [x, ga, gb, gc, w_out_bf, mod3]
    out_specs = [pl.BlockSpec((1, tm, D_MODEL), lambda b, i: (b, i, 0))]
    out_shape = [jax.ShapeDtypeStruct(x.shape, f32)]
    aliases = {}
    if next_norm is not None:
        norm_w, mod3_next, h_buf = next_norm
        in_specs += [pl.BlockSpec((1, D_MODEL), lambda b, i: (0, 0)),
                     pl.BlockSpec((1, 1, 3 * D_MODEL), lambda b, i: (mod_row(b), 0, 0))]
        args += [norm_w.reshape(1, D_MODEL), mod3_next]
        if h_buf is not None:
            in_specs.append(pl.BlockSpec(memory_space=pl.ANY))
            args.append(h_buf)
            aliases = {len(args) - 1: 1}
        out_specs.append(pl.BlockSpec((1, tm, D_MODEL), lambda b, i: (b, row_block0 + i, 0)))
        out_shape.append(jax.ShapeDtypeStruct((BATCH, S_ALL, D_MODEL), bf16))
    return pl.pallas_call(
        functools.partial(_out_kernel, next_norm is not None),
        grid=(BATCH, t // tm),
        in_specs=in_specs,
        out_specs=out_specs,
        out_shape=out_shape,
        input_output_aliases=aliases,
        compiler_params=_cparams("arbitrary", "arbitrary"),
        name="out_proj_ctx" if is_ctx else "out_proj",
    )(*args)


def _rope_tables():
    t = jnp.arange(SEQ)
    row = (t // GRID_W).astype(f32)
    col = (t % GRID_W).astype(f32)
    n_freq = ROPE_DIM // 4
    inv = ROPE_BASE ** (-jnp.arange(n_freq, dtype=f32) / n_freq)
    ar = row[:, None] * inv
    ac = col[:, None] * inv
    ang = jnp.concatenate([ar, ar, ac, ac], axis=-1)
    cos = jnp.cos(ang).astype(f32)
    sin = jnp.sin(ang).astype(f32)
    sign = jnp.asarray(np.where((np.arange(ROPE_DIM) % 32) < 16, -1.0, 1.0), dtype=f32)
    sin = sin * sign
    cos = jnp.concatenate([cos, jnp.ones((CTX_LEN, ROPE_DIM), f32)], axis=0)
    sin = jnp.concatenate([sin, jnp.zeros((CTX_LEN, ROPE_DIM), f32)], axis=0)
    pair = (jnp.concatenate([cos, cos], axis=1), jnp.concatenate([sin, sin], axis=1))
    single = (jnp.concatenate([cos, jnp.ones_like(cos)], axis=1), jnp.concatenate([sin, jnp.zeros_like(sin)], axis=1))
    return pair, single


def _pad_heads(w, width, padded):
    lead = w.shape[:-1]
    w = w.reshape(lead + (H_C, width))
    w = jnp.pad(w, [(0, 0)] * len(lead) + [(0, 0), (0, padded - width)])
    return w.reshape(lead + (H_C * padded,))


def kernel(x, c, ctx, c_ctx, norm_w, w_ada, b_ada, w_in, qn_a, kn_a, rpb_a, qn_b, kn_b, sink_b,
           qa_norm, kva_norm, w_qb, w_kvb, qn_c, kn_c, w_out):
    (cos2, sin2), (cos1, sin1) = _rope_tables()
    cvec = jnp.concatenate([c, c_ctx[None, :], jnp.zeros((MOD_ROWS - BATCH - 1, D_MODEL), f32)], axis=0)
    mod = _modulation(cvec, w_ada, b_ada)
    q_scale_ab = HD ** -0.5 * LOG2E
    q_scale_c = MLA_SCALE * LOG2E
    w_out_bf = w_out.astype(bf16)
    na_bias = _natten_bias(rpb_a)

    o_qa, o_ka, o_va, o_qb, o_kb, o_vb, o_cq, o_ckv, o_kpe, o_z = np.cumsum(
        (0, D_A, D_A, D_A, D_B, KV_B, KV_B, Q_LORA, KV_LORA, ROPE_DIM)).tolist()

    xc = ctx
    mod3_all = [mod[l].reshape(MOD_ROWS, 1, 3 * D_MODEL) for l in range(DEPTH)]
    h3 = None
    for l in range(DEPTH):
        last = l == DEPTH - 1
        mod3 = mod3_all[l]
        wl = w_in[l]
        wz = wl[:, o_z:].astype(bf16)
        if l == 0:
            sza, szb, szc, h = _proj_gate_norm(x, xc, norm_w[0], mod3, wz)
        else:
            h = h3.reshape(R_ALL, D_MODEL)
            sza, szb, szc = _proj_gate(h, wz)
        h3 = h.reshape(BATCH, S_ALL, D_MODEL)

        gain_hn = jnp.concatenate([jnp.tile(qn_a[l] * q_scale_ab, H_A), jnp.tile(kn_a[l], H_A),
                                   jnp.tile(qn_b[l] * q_scale_ab, H_B), jnp.tile(kn_b[l], HKV_B)])
        w_hn = jnp.concatenate([wl[:, o_qa:o_va], wl[:, o_qb:o_vb]], axis=1).astype(bf16)
        w_lora = jnp.pad(wl[:, o_cq:o_z], ((0, 0), (0, LANES - ROPE_DIM))).astype(bf16)
        w_v = jnp.concatenate([wl[:, o_va:o_qb], wl[:, o_vb:o_cq]], axis=1).astype(bf16)
        qka, qkb, cqn, ckvn, kpe, vt_ab = _proj_main(h, w_hn, w_lora, w_v, gain_hn, qa_norm[l], kva_norm[l],
                                                      cos2, sin2)

        wq = _pad_heads(w_qb[l], QK_C, QC_PAD).astype(bf16)
        wkv = w_kvb[l].reshape(KV_LORA, H_C, NOPE_C + V_C)
        wkn = wkv[:, :, :NOPE_C].reshape(KV_LORA, H_C * NOPE_C).astype(bf16)
        wv = wkv[:, :, NOPE_C:].reshape(KV_LORA, H_C * V_C).astype(bf16)
        gq = _pad_heads(jnp.tile(qn_c[l] * q_scale_c, H_C), QK_C, QC_PAD).reshape(1, H_C * QC_PAD)
        gk = jnp.pad(kn_c[l], (0, QC_PAD - QK_C)).reshape(1, QC_PAD)
        qc, kc, vc = _mla_up(cqn, ckvn, kpe, wq, wkn, wv, gq, gk, cos1, sin1)

        def b3(a):
            return a.reshape(BATCH, S_ALL, a.shape[-1])

        qka, qkb, qc, kc, vc, sza, szb, szc = map(b3, (qka, qkb, qc, kc, vc, sza, szb, szc))

        ga = _natten(qka, vt_ab, na_bias, l, sza)
        gb = _swa(qkb, vt_ab, sink_b[l], szb)
        gc = _mla(qc, kc, vc, szc)
        if last:
            (x,) = _out_proj(x, ga, gb, gc, w_out_bf, l, mod3, False)
        else:
            ga_c, gb_c, gc_c = _ctx_attention(qka, qkb, vt_ab, qc, kc, vc, sink_b[l], sza, szb, szc)
            x, h3 = _out_proj(x, ga, gb, gc, w_out_bf, l, mod3, False,
                              next_norm=(norm_w[l + 1], mod3_all[l + 1], h3))
            xc, h3 = _out_proj(xc, ga_c, gb_c, gc_c, w_out_bf, l, mod3, True,
                               next_norm=(norm_w[l + 1], mod3_all[l + 1], h3))
    return x
```
